```python
import math
import jax, jax.numpy as jnp
from jax import lax
import numpy as np

D_MODEL = 2048
BATCH = 8
SEQ = 4096
DEPTH = 2
DEC_BATCH = 8
DEC_SEQ = 2048
PAST_LEN = 128

GRID_W = 64
HEAD_DIM = 128
MIX_W = 3 * D_MODEL // 4
MIX_HEADS = MIX_W // HEAD_DIM
MEM_W = D_MODEL // 4
MEM_HEADS = MEM_W // HEAD_DIM
MEM_TOKENS = 256
IN_W = 3 * MIX_W + MEM_W
NA_MAX_KH = 8
NA_KW = 16
NA_QW = 16
NA_NCB = GRID_W // NA_QW
NA_KBW = 2 * NA_KW
FILTER_EMB = 33
FILTER_BANDS = (FILTER_EMB - 1) // 2
FILTER_ORDER = 64
HYENA_TARGET = 1e-2
FAST_DECAY_PCT = 0.3
SLOW_DECAY_PCT = 1.5
SHORT_CONV = 3
D_FF = 4 * D_MODEL
N_MIXERS = 2
N_NA = (DEPTH + 1) // 2
N_HY = DEPTH // 2
RMS_EPS = 1e-6

kernel_name = "hybrid_natten_hyena_encoder"


def _rmsnorm(x, g):
    xf = x.astype(jnp.float32)
    y = xf * lax.rsqrt(jnp.mean(xf * xf, axis=-1, keepdims=True) + RMS_EPS)
    return (y * g.astype(jnp.float32)).astype(x.dtype)


def _neighbourhood_attention(qkv, rpb):
    B, L, _ = qkv.shape
    rows = L // GRID_W
    kh = min(NA_MAX_KH, rows)
    q, k, v = jnp.split(qkv, 3, axis=-1)
    q = (q * (HEAD_DIM ** -0.5)).reshape(B, rows, GRID_W, MIX_HEADS, HEAD_DIM)
    k = k.reshape(B, rows, GRID_W, MIX_HEADS, HEAD_DIM)
    v = v.reshape(B, rows, GRID_W, MIX_HEADS, HEAD_DIM)
    qc = np.arange(GRID_W).reshape(NA_NCB, NA_QW)
    kcs = np.clip(np.arange(NA_NCB) * NA_QW - NA_KW // 2, 0, GRID_W - NA_KBW)
    kc = kcs[:, None] + np.arange(NA_KBW)
    cs = np.clip(qc - NA_KW // 2, 0, GRID_W - NA_KW)
    valid = (kc[:, None, :] >= cs[..., None]) & (kc[:, None, :] < cs[..., None] + NA_KW)
    dc_idx = np.clip(kc[:, None, :] - qc[..., None] + NA_KW - 1, 0, 2 * NA_KW - 2)
    rpb_c = rpb[:, :, dc_idx]
    valid_b = valid[:, :, None, :]

    def one_row(r):
        rs = jnp.clip(r - kh // 2, 0, rows - kh)
        q_r = lax.dynamic_index_in_dim(q, r, axis=1, keepdims=False)
        q_r = q_r.reshape(B, NA_NCB, NA_QW, MIX_HEADS, HEAD_DIM)
        k_r = lax.dynamic_slice_in_dim(k, rs, kh, axis=1)[:, :, kc]
        v_r = lax.dynamic_slice_in_dim(v, rs, kh, axis=1)[:, :, kc]
        s = jnp.einsum('bnqhd,bknjhd->bhnqkj', q_r, k_r).astype(jnp.float32)
        dr_idx = rs + jnp.arange(kh) - r + NA_MAX_KH - 1
        bias = jnp.take(rpb_c, dr_idx, axis=1).transpose(0, 2, 3, 1, 4)
        s = jnp.where(valid_b, s + bias[None].astype(jnp.float32), -jnp.inf)
        p = jax.nn.softmax(s.reshape(B, MIX_HEADS, NA_NCB, NA_QW, kh * NA_KBW), axis=-1)
        p = p.reshape(s.shape).astype(v_r.dtype)
        o = jnp.einsum('bhnqkj,bknjhd->bnqhd', p, v_r)
        return o.reshape(B, GRID_W, MIX_W)

    out = lax.map(one_row, jnp.arange(rows))
    return out.transpose(1, 0, 2, 3).reshape(B, L, MIX_W)


def _hyena_filter(L, w1, b1, w2, b2, w3, freq, decay):
    f32 = jnp.float32
    t = jnp.linspace(0.0, 1.0, L, dtype=f32)[:, None]
    wpos = 2.0 * math.pi * jnp.arange(L, dtype=f32) / L
    fb = jnp.linspace(1e-4, FILTER_BANDS - 1, FILTER_BANDS, dtype=f32)
    ang = wpos[:, None] * fb[None, :]
    z = jnp.concatenate([t, jnp.cos(ang), -jnp.sin(ang)], axis=-1)
    fr = freq.astype(f32)
    h = jnp.sin(fr * (z @ w1.astype(f32) + b1.astype(f32)))
    h = jnp.sin(fr * (h @ w2.astype(f32) + b2.astype(f32)))
    h = (h @ w3.astype(f32)).reshape(L, 2, MIX_W)
    h = h * jnp.exp(-t[:, :, None] * jnp.abs(decay.astype(f32))[None])
    return h[:, 0], h[:, 1]


def _hyena(u, short_w, short_b, w1, b1, w2, b2, w3, freq, decay, d_bias):
    B, L, _ = u.shape
    up = jnp.pad(u, ((0, 0), (1, 1), (0, 0)))
    u = up[:, :-2] * short_w[0] + up[:, 1:-1] * short_w[1] + up[:, 2:] * short_w[2] + short_b
    x0, x1, v = jnp.split(u, 3, axis=-1)
    h_f, h_b = _hyena_filter(L, w1, b1, w2, b2, w3, freq, decay)
    kern = jnp.concatenate([h_f[:1] + h_b[:1], h_f[1:], jnp.zeros((1, MIX_W), jnp.float32), h_b[1:][::-1]], axis=0)
    z = (x1 * v).astype(jnp.float32)
    n = 2 * L
    zf = jnp.fft.rfft(z, n=n, axis=1)
    kf = jnp.fft.rfft(kern, n=n, axis=0)
    y = jnp.fft.irfft(zf * kf[None], n=n, axis=1)[:, :L]
    y = y + d_bias.astype(jnp.float32) * z
    return (x0.astype(jnp.float32) * y).astype(u.dtype)


def _memory_attention(q, mem, g, w_kv):
    B, L, _ = q.shape
    kv = _rmsnorm(mem, g) @ w_kv
    km, vm = jnp.split(kv, 2, axis=-1)
    km = km.reshape(B, -1, MEM_HEADS, HEAD_DIM)
    vm = vm.reshape(B, -1, MEM_HEADS, HEAD_DIM)
    q = (q * (HEAD_DIM ** -0.5)).reshape(B, L, MEM_HEADS, HEAD_DIM)
    s = jnp.einsum('blhd,bmhd->bhlm', q, km).astype(jnp.float32)
    p = jax.nn.softmax(s, axis=-1).astype(vm.dtype)
    return jnp.einsum('bhlm,bmhd->blhd', p, vm).reshape(B, L, MEM_W)


def _trunk(x, mem, weights):
    (g_mix_pre, w_in, na_rpb, hy_short_w, hy_short_b, hy_w1, hy_b1, hy_w2, hy_b2, hy_w3, hy_freq,
     hy_decay, hy_d, g_mem, w_mem_kv, w_out, g_mix_post, g_mlp_pre, w_up, w_down, g_mlp_post) = weights
    for i in range(DEPTH):
        h = _rmsnorm(x, g_mix_pre[i])
        proj = h @ w_in[i]
        mix_in, q_mem = proj[..., :3 * MIX_W], proj[..., 3 * MIX_W:]
        j = i // N_MIXERS
        if i % N_MIXERS == 0:
            tok = _neighbourhood_attention(mix_in, na_rpb[j])
        else:
            tok = _hyena(mix_in, hy_short_w[j], hy_short_b[j], hy_w1[j], hy_b1[j], hy_w2[j], hy_b2[j],
                         hy_w3[j], hy_freq[j], hy_decay[j], hy_d[j])
        mo = _memory_attention(q_mem, mem, g_mem[i], w_mem_kv[i])
        o = jnp.concatenate([tok, mo], axis=-1) @ w_out[i]
        x = x + _rmsnorm(o, g_mix_post[i])
        h = _rmsnorm(x, g_mlp_pre[i])
        h = jnp.square(jax.nn.relu(h @ w_up[i])) @ w_down[i]
        x = x + _rmsnorm(h, g_mlp_post[i])
    return x


def setup_inputs(seed: int = 0) -> dict:
    key = jax.random.key(seed)
    ks = jax.random.split(key, 28)
    f32 = jnp.float32

    def nrm(k, shape, scale):
        return jax.random.normal(k, shape, f32) * scale

    def gain(k, shape):
        return 1.0 + 0.02 * jax.random.normal(k, shape, f32)

    lo = abs(math.log(HYENA_TARGET) / SLOW_DECAY_PCT)
    hi = abs(math.log(HYENA_TARGET) / FAST_DECAY_PCT)
    base = np.linspace(lo, hi, MIX_W).astype(np.float32)
    base_decay = jnp.asarray(np.stack([base, base[::-1].copy()]))
    hy_decay = base_decay[None] * (1.0 + 0.1 * jax.random.normal(ks[16], (N_HY, 2, MIX_W), f32))
    return {
        "x_prompt": jax.random.normal(ks[0], (BATCH, SEQ, D_MODEL), f32),
        "x_sample": jax.random.normal(ks[1], (DEC_BATCH, DEC_SEQ, D_MODEL), f32),
        "mem_prompt": jax.random.normal(ks[2], (BATCH, MEM_TOKENS, D_MODEL), f32),
        "mem_sample": jax.random.normal(ks[3], (DEC_BATCH, MEM_TOKENS, D_MODEL), f32),
        "g_mix_pre": gain(ks[4], (DEPTH, D_MODEL)),
        "w_in": nrm(ks[5], (DEPTH, D_MODEL, IN_W), D_MODEL ** -0.5),
        "na_rpb": nrm(ks[6], (N_NA, MIX_HEADS, 2 * NA_MAX_KH - 1, 2 * NA_KW - 1), 0.02),
        "hy_short_w": nrm(ks[7], (N_HY, SHORT_CONV, 3 * MIX_W), SHORT_CONV ** -0.5),
        "hy_short_b": nrm(ks[8], (N_HY, 3 * MIX_W), 0.02),
        "hy_w1": nrm(ks[9], (N_HY, FILTER_EMB, FILTER_ORDER), FILTER_EMB ** -0.5),
        "hy_b1": nrm(ks[10], (N_HY, FILTER_ORDER), 0.02),
        "hy_w2": nrm(ks[11], (N_HY, FILTER_ORDER, FILTER_ORDER), FILTER_ORDER ** -0.5),
        "hy_b2": nrm(ks[12], (N_HY, FILTER_ORDER), 0.02),
        "hy_w3": nrm(ks[13], (N_HY, FILTER_ORDER, 2 * MIX_W), 0.05 * FILTER_ORDER ** -0.5),
        "hy_freq": 1.0 + 0.01 * jax.random.normal(ks[14], (N_HY, FILTER_ORDER), f32),
        "hy_decay": hy_decay,
        "hy_d": nrm(ks[15], (N_HY, MIX_W), 1.0),
        "g_mem": gain(ks[17], (DEPTH, D_MODEL)),
        "w_mem_kv": nrm(ks[18], (DEPTH, D_MODEL, 2 * MEM_W), D_MODEL ** -0.5),
        "w_out": nrm(ks[19], (DEPTH, MIX_W + MEM_W, D_MODEL), (MIX_W + MEM_W) ** -0.5),
        "g_mix_post": gain(ks[20], (DEPTH, D_MODEL)),
        "g_mlp_pre": gain(ks[21], (DEPTH, D_MODEL)),
        "w_up": nrm(ks[22], (DEPTH, D_MODEL, D_FF), D_MODEL ** -0.5),
        "w_down": nrm(ks[23], (DEPTH, D_FF, D_MODEL), D_FF ** -0.5),
        "g_mlp_post": gain(ks[24], (DEPTH, D_MODEL)),
    }


def reference(x_prompt, x_sample, mem_prompt, mem_sample, g_mix_pre, w_in, na_rpb, hy_short_w, hy_short_b,
              hy_w1, hy_b1, hy_w2, hy_b2, hy_w3, hy_freq, hy_decay, hy_d, g_mem, w_mem_kv, w_out,
              g_mix_post, g_mlp_pre, w_up, w_down, g_mlp_post):
    weights = (g_mix_pre, w_in, na_rpb, hy_short_w, hy_short_b, hy_w1, hy_b1, hy_w2, hy_b2, hy_w3, hy_freq,
               hy_decay, hy_d, g_mem, w_mem_kv, w_out, g_mix_post, g_mlp_pre, w_up, w_down, g_mlp_post)
    y_prompt = _trunk(x_prompt, mem_prompt, weights)
    y_sample = _trunk(x_sample, mem_sample, weights)
    return (y_prompt, y_sample)
```

```python
import functools
import math

import numpy as np
import jax
import jax.numpy as jnp
from jax import lax
from jax.experimental import pallas as pl
from jax.experimental.pallas import tpu as pltpu

F32 = jnp.float32
BF16 = jnp.bfloat16

GRID_W = 64
HEAD_DIM = 128
NA_MAX_KH = 8
NA_KW = 16
FILTER_EMB = 33
FILTER_BANDS = (FILTER_EMB - 1) // 2
RMS_EPS = 1e-6
NEG_BIG = -1e30

VMEM_LIMIT_BYTES = 56 * 1024 * 1024
NA_ROWS_PER_STEP = 4
NA_HEADS_PER_STEP = 2
NA_KEY_ROWS = NA_ROWS_PER_STEP + NA_MAX_KH


def _params(*sem):
    return pltpu.CompilerParams(dimension_semantics=sem, vmem_limit_bytes=VMEM_LIMIT_BYTES)


def _rmsnorm(x, g):
    ms = jnp.mean(x * x, axis=-1, keepdims=True)
    return x * lax.rsqrt(ms + RMS_EPS) * g


def _dot(a, b):
    return jnp.dot(a, b, preferred_element_type=F32)


def _dot_nt(a, b):
    return lax.dot_general(a, b, (((1,), (1,)), ((), ())), preferred_element_type=F32)


def _softmax_pv(s, v):
    m = jnp.max(s, axis=-1, keepdims=True)
    p = jnp.exp(s - m)
    l = jnp.sum(p, axis=-1, keepdims=True)
    return _dot(p.astype(BF16), v) / l


def _norm_matmul_kernel(x_ref, g_ref, w_ref, s_ref, o_ref, h_ref):
    @pl.when(pl.program_id(1) == 0)
    def _():
        h_ref[...] = _rmsnorm(x_ref[...], g_ref[...]).astype(h_ref.dtype)

    o_ref[...] = (_dot(h_ref[...], w_ref[...]) * s_ref[...]).astype(o_ref.dtype)


def _norm_matmul(x, g, w, colscale, tm, tn):
    T, D = x.shape
    N = w.shape[1]
    return pl.pallas_call(
        _norm_matmul_kernel,
        grid=(T // tm, N // tn),
        in_specs=[
            pl.BlockSpec((tm, D), lambda i, j: (i, 0)),
            pl.BlockSpec((1, D), lambda i, j: (0, 0)),
            pl.BlockSpec((D, tn), lambda i, j: (0, j)),
            pl.BlockSpec((1, tn), lambda i, j: (0, j)),
        ],
        out_specs=pl.BlockSpec((tm, tn), lambda i, j: (i, j)),
        out_shape=jax.ShapeDtypeStruct((T, N), BF16),
        scratch_shapes=[pltpu.VMEM((tm, D), BF16)],
        compiler_params=_params("parallel", "arbitrary"),
        name="norm_in_proj",
    )(x, g, w, colscale)


def _na_bias_table(rpb, rows):
    R, KR, W = NA_ROWS_PER_STEP, NA_KEY_ROWS, GRID_W
    kh = min(NA_MAX_KH, rows)
    nblk = rows // R
    qc = np.arange(W)
    cs = np.clip(qc - NA_KW // 2, 0, W - NA_KW)
    kc = np.arange(W)
    col_ok = (kc[None, :] >= cs[:, None]) & (kc[None, :] < cs[:, None] + NA_KW)
    dc = np.clip(kc[None, :] - qc[:, None] + NA_KW - 1, 0, 2 * NA_KW - 2)
    dr_all, ok_all = [], []
    for blk in (0, min(1, nblk - 1), nblk - 1):
        ks = int(np.clip(blk * R - kh // 2, 0, rows - KR))
        r = blk * R + np.arange(R)
        rs = np.clip(r - kh // 2, 0, rows - kh)
        key = ks + np.arange(KR)
        row_ok = (key[None, :] >= rs[:, None]) & (key[None, :] < rs[:, None] + kh)
        dr = np.clip(key[None, :] - r[:, None] + NA_MAX_KH - 1, 0, 2 * NA_MAX_KH - 2)
        dr_all.append(dr)
        ok_all.append(row_ok)
    dr = np.stack(dr_all)
    row_ok = np.stack(ok_all)
    per_dr = jnp.where(col_ok, rpb.astype(F32)[:, :, dc], NEG_BIG)
    blk = jnp.where(row_ok[None, :, :, :, None, None], per_dr[:, dr], NEG_BIG)
    return blk.transpose(1, 0, 2, 4, 3, 5).reshape(3, rpb.shape[0], R * W, KR * W)


def _na_kernel(q_ref, k_ref, v_ref, b_ref, o_ref, *, rows):
    R, KR, W = NA_ROWS_PER_STEP, NA_KEY_ROWS, GRID_W
    i = pl.program_id(2)
    ks = jnp.clip(i * R - min(NA_MAX_KH, rows) // 2, 0, rows - KR) * W
    ks = pl.multiple_of(ks, W)
    for h in range(NA_HEADS_PER_STEP):
        cols = slice(h * HEAD_DIM, (h + 1) * HEAD_DIM)
        q = q_ref[:, cols]
        k = k_ref[pl.ds(ks, KR * W), cols]
        v = v_ref[pl.ds(ks, KR * W), cols]
        s = _dot_nt(q, k) + b_ref[0, h]
        o_ref[:, cols] = _softmax_pv(s, v).astype(o_ref.dtype)


def _neighbourhood_attention(proj, bias, B, L, mix_w):
    R, KR, W = NA_ROWS_PER_STEP, NA_KEY_ROWS, GRID_W
    rows = L // W
    nblk = rows // R
    hw = NA_HEADS_PER_STEP * HEAD_DIM
    ngrp = mix_w // hw
    assert rows % R == 0 and rows >= KR and nblk >= 2 and R >= NA_MAX_KH // 2

    def variant(i):
        return jnp.where(i == 0, 0, jnp.where(i == nblk - 1, 2, 1))

    return pl.pallas_call(
        functools.partial(_na_kernel, rows=rows),
        grid=(B, ngrp, nblk),
        in_specs=[
            pl.BlockSpec((R * W, hw), lambda b, g, i: (b * nblk + i, g)),
            pl.BlockSpec((L, hw), lambda b, g, i: (b, ngrp + g)),
            pl.BlockSpec((L, hw), lambda b, g, i: (b, 2 * ngrp + g)),
            pl.BlockSpec((1, NA_HEADS_PER_STEP, R * W, KR * W), lambda b, g, i: (variant(i), g, 0, 0)),
        ],
        out_specs=pl.BlockSpec((R * W, hw), lambda b, g, i: (b * nblk + i, g)),
        out_shape=jax.ShapeDtypeStruct((B * L, mix_w), BF16),
        compiler_params=_params("parallel", "parallel", "arbitrary"),
        name="neighbourhood_attention",
    )(proj, proj, proj, bias)


def _position_features(L, shift):
    pos = jnp.arange(L, dtype=F32) + shift
    t = pos / (L - 1)
    wpos = 2.0 * math.pi * pos / L
    fb = jnp.linspace(1e-4, FILTER_BANDS - 1, FILTER_BANDS, dtype=F32)
    ang = wpos[:, None] * fb[None, :]
    z = jnp.concatenate([t[:, None], jnp.cos(ang), -jnp.sin(ang)], axis=-1)
    return jnp.pad(z, ((0, 0), (0, 128 - FILTER_EMB)))


def _hy_filter_kernel(zf_ref, zb_ref, w1_ref, b1_ref, w2_ref, b2_ref, w3_ref, fr_ref, dec_ref,
                      hs_ref, hd_ref, *, L, mix_w):
    hp = lax.Precision.HIGHEST
    fr = fr_ref[...]

    def hidden(z):
        h = jnp.sin(fr * (jnp.dot(z, w1_ref[...], precision=hp, preferred_element_type=F32) + b1_ref[...]))
        return jnp.sin(fr * (jnp.dot(h, w2_ref[...], precision=hp, preferred_element_type=F32) + b2_ref[...]))

    def taps(h, z, d):
        w3 = w3_ref[:, d * mix_w:(d + 1) * mix_w]
        win = jnp.exp(-z[:, 0:1] * jnp.abs(dec_ref[d:d + 1, :]))
        return jnp.dot(h, w3, precision=hp, preferred_element_type=F32) * win

    tl = zf_ref.shape[0]
    row = pl.program_id(0) * tl + lax.broadcasted_iota(jnp.int32, (tl, 1), 0)
    zf, zb = zf_ref[...], zb_ref[...]
    hf_hidden = hidden(zf)
    h_fwd = taps(hf_hidden, zf, 0) + jnp.where(row == 0, taps(hf_hidden, zf, 1), 0.0)
    h_bwd = jnp.where(row + 1 < L, taps(hidden(zb), zb, 1), 0.0)
    hs_ref[...] = (h_fwd + h_bwd).astype(hs_ref.dtype)
    hd_ref[...] = (h_bwd - h_fwd).astype(hd_ref.dtype)


def _hyena_filter_taps(L, w1, b1, w2, b2, w3, freq, decay, tl):
    mix_w = decay.shape[1]
    order = w2.shape[0]
    zf = _position_features(L, 0)
    zb = _position_features(L, 1)
    w1p = jnp.pad(w1.astype(F32), ((0, 128 - FILTER_EMB), (0, 0)))
    full = lambda shape: pl.BlockSpec(shape, lambda i: (0, 0))
    return pl.pallas_call(
        functools.partial(_hy_filter_kernel, L=L, mix_w=mix_w),
        grid=(L // tl,),
        in_specs=[
            pl.BlockSpec((tl, 128), lambda i: (i, 0)),
            pl.BlockSpec((tl, 128), lambda i: (i, 0)),
            full((128, order)), full((1, order)), full((order, order)), full((1, order)),
            full((order, 2 * mix_w)), full((1, order)), full((2, mix_w)),
        ],
        out_specs=[pl.BlockSpec((tl, mix_w), lambda i: (i, 0))] * 2,
        out_shape=[jax.ShapeDtypeStruct((L, mix_w), BF16)] * 2,
        compiler_params=_params("parallel"),
        name="hyena_filter_taps",
    )(zf, zb, w1p, b1.astype(F32)[None], w2.astype(F32), b2.astype(F32)[None], w3.astype(F32),
      freq.astype(F32)[None], decay.astype(F32))


def _dft_matrices(L):
    odd = 2 * jnp.arange(L, dtype=jnp.int32) + 1
    m = (odd[:, None] * odd[None, :]) % (8 * L)
    theta = m.astype(F32) * (math.pi / (4 * L))
    half = odd.astype(F32) * (math.pi / (4 * L))
    return (jnp.cos(theta).astype(BF16), jnp.sin(theta).astype(BF16),
            jnp.cos(half)[:, None], jnp.sin(half)[:, None])


def _filter_spectrum_kernel(mc_ref, ms_ref, hs_ref, hd_ref, cr_ref, sr_ref, gr_ref, gi_ref, *, scale):
    a = _dot(mc_ref[...], hs_ref[...])
    b = _dot(ms_ref[...], hd_ref[...])
    cr, sr = cr_ref[...] * scale, sr_ref[...] * scale
    gr_ref[...] = cr * a - sr * b
    gi_ref[...] = sr * a + cr * b


def _filter_spectrum(mc, ms, hs, hd, cr, sr, tk, cb):
    L, C = hs.shape
    return pl.pallas_call(
        functools.partial(_filter_spectrum_kernel, scale=1.0 / L),
        grid=(L // tk, C // cb),
        in_specs=[
            pl.BlockSpec((tk, L), lambda k, c: (k, 0)),
            pl.BlockSpec((tk, L), lambda k, c: (k, 0)),
            pl.BlockSpec((L, cb), lambda k, c: (0, c)),
            pl.BlockSpec((L, cb), lambda k, c: (0, c)),
            pl.BlockSpec((tk, 1), lambda k, c: (k, 0)),
            pl.BlockSpec((tk, 1), lambda k, c: (k, 0)),
        ],
        out_specs=[pl.BlockSpec((tk, cb), lambda k, c: (k, c))] * 2,
        out_shape=[jax.ShapeDtypeStruct((L, C), F32)] * 2,
        compiler_params=_params("parallel", "arbitrary"),
        name="hyena_filter_spectrum",
    )(mc, ms, hs, hd, cr, sr)


def _short_conv(ref, w_ref, b_ref, start, size, L):
    halo = 16
    cur = ref[pl.ds(start, size), :].astype(F32)
    prev_start = pl.multiple_of(jnp.maximum(start - halo, 0), halo)
    next_start = pl.multiple_of(jnp.minimum(start + size, L - halo), halo)
    prev = ref[pl.ds(prev_start, halo), :][halo - 1:halo, :].astype(F32)
    nxt = ref[pl.ds(next_start, halo), :][0:1, :].astype(F32)
    prev = jnp.where(start == 0, 0.0, prev)
    nxt = jnp.where(start + size == L, 0.0, nxt)
    row = lax.broadcasted_iota(jnp.int32, (size, 1), 0)
    up = jnp.where(row == 0, prev, pltpu.roll(cur, 1, axis=0))
    dn = jnp.where(row == size - 1, nxt, pltpu.roll(cur, size - 1, axis=0))
    return up * w_ref[0:1, :] + cur * w_ref[1:2, :] + dn * w_ref[2:3, :] + b_ref[...]


def _hy_gate_kernel(x0_ref, x1_ref, v_ref, w0_ref, w1_ref, wv_ref, b0_ref, b1_ref, bv_ref,
                    x0c_ref, z_ref, *, chunk):
    L = x0_ref.shape[0]

    def body(c, carry):
        start = pl.multiple_of(c * chunk, chunk)
        rows = pl.ds(start, chunk)
        x0c_ref[rows, :] = _short_conv(x0_ref, w0_ref, b0_ref, start, chunk, L).astype(x0c_ref.dtype)
        x1c = _short_conv(x1_ref, w1_ref, b1_ref, start, chunk, L)
        vc = _short_conv(v_ref, wv_ref, bv_ref, start, chunk, L)
        z_ref[rows, :] = (x1c * vc).astype(z_ref.dtype)
        return carry

    lax.fori_loop(0, L // chunk, body, 0)


def _hyena_gate(proj, short_w, short_b, B, L, mix_w, cb, chunk):
    nc = mix_w // cb
    col = lambda part: pl.BlockSpec((L, cb), lambda b, c: (b, part * nc + c))
    wspec = lambda part: pl.BlockSpec((3, cb), lambda b, c: (0, part * nc + c))
    bspec = lambda part: pl.BlockSpec((1, cb), lambda b, c: (0, part * nc + c))
    return pl.pallas_call(
        functools.partial(_hy_gate_kernel, chunk=chunk),
        grid=(B, nc),
        in_specs=[col(0), col(1), col(2), wspec(0), wspec(1), wspec(2), bspec(0), bspec(1), bspec(2)],
        out_specs=[pl.BlockSpec((L, cb), lambda b, c: (b, c))] * 2,
        out_shape=[jax.ShapeDtypeStruct((B * L, mix_w), BF16)] * 2,
        compiler_params=_params("parallel", "parallel"),
        name="hyena_gate",
    )(proj, proj, proj, short_w, short_w, short_w, short_b, short_b, short_b)


def _dft_fwd_kernel(mc_ref, ms_ref, z_ref, gr_ref, gi_ref, y_ref):
    z = z_ref[...]
    zr = _dot(mc_ref[...], z)
    zs = _dot(ms_ref[...], z)
    gr, gi = gr_ref[...], gi_ref[...]
    y_ref[0] = (zr * gr + zs * gi).astype(y_ref.dtype)
    y_ref[1] = (zr * gi - zs * gr).astype(y_ref.dtype)


def _dft_forward(mc, ms, z, gr, gi, B, L, tk, cb):
    C = z.shape[1]
    out = pl.pallas_call(
        _dft_fwd_kernel,
        grid=(L // tk, C // cb, B),
        in_specs=[
            pl.BlockSpec((tk, L), lambda k, c, b: (k, 0)),
            pl.BlockSpec((tk, L), lambda k, c, b: (k, 0)),
            pl.BlockSpec((L, cb), lambda k, c, b: (b, c)),
            pl.BlockSpec((tk, cb), lambda k, c, b: (k, c)),
            pl.BlockSpec((tk, cb), lambda k, c, b: (k, c)),
        ],
        out_specs=pl.BlockSpec((None, 2, tk, cb), lambda k, c, b: (b, 0, k, c)),
        out_shape=jax.ShapeDtypeStruct((B, 2, L, C), BF16),
        compiler_params=_params("parallel", "parallel", "arbitrary"),
        name="hyena_dft_forward",
    )(mc, ms, z, gr, gi)
    return out


def _dft_inv_kernel(mc_ref, ms_ref, y_ref, z_ref, x0_ref, d_ref, o_ref):
    y = _dot(mc_ref[...], y_ref[0]) - _dot(ms_ref[...], y_ref[1])
    y = y + d_ref[...] * z_ref[...].astype(F32)
    o_ref[...] = (x0_ref[...].astype(F32) * y).astype(o_ref.dtype)


def _dft_inverse(mc, ms, y, z, x0c, d, B, L, tt, cb):
    C = z.shape[1]
    nt = L // tt
    return pl.pallas_call(
        _dft_inv_kernel,
        grid=(nt, C // cb, B),
        in_specs=[
            pl.BlockSpec((tt, L), lambda t, c, b: (t, 0)),
            pl.BlockSpec((tt, L), lambda t, c, b: (t, 0)),
            pl.BlockSpec((None, 2, L, cb), lambda t, c, b: (b, 0, 0, c)),
            pl.BlockSpec((tt, cb), lambda t, c, b: (b * nt + t, c)),
            pl.BlockSpec((tt, cb), lambda t, c, b: (b * nt + t, c)),
            pl.BlockSpec((1, cb), lambda t, c, b: (0, c)),
        ],
        out_specs=pl.BlockSpec((tt, cb), lambda t, c, b: (b * nt + t, c)),
        out_shape=jax.ShapeDtypeStruct((B * L, C), BF16),
        compiler_params=_params("parallel", "parallel", "arbitrary"),
        name="hyena_dft_inverse",
    )(mc, ms, y, z, x0c, d)


def _hyena(proj, B, L, mix_w, short_w, short_b, w1, b1, w2, b2, w3, freq, decay, d_bias):
    tk = min(512, L)
    cb = 512
    hs, hd = _hyena_filter_taps(L, w1, b1, w2, b2, w3, freq, decay, tl=min(512, L))
    mc, ms, cr, sr = _dft_matrices(L)
    gr, gi = _filter_spectrum(mc, ms, hs, hd, cr, sr, tk, cb)
    x0c, z = _hyena_gate(proj, short_w.astype(F32), short_b.astype(F32)[None], B, L, mix_w,
                         cb=256, chunk=min(256, L))
    y = _dft_forward(mc, ms, z, gr, gi, B, L, tk, cb)
    return _dft_inverse(mc, ms, y, z, x0c, d_bias.astype(F32)[None], B, L, tk, cb)


def _mem_kv_kernel(m_ref, g_ref, w_ref, o_ref):
    h = _rmsnorm(m_ref[...], g_ref[...]).astype(BF16)
    o_ref[...] = _dot(h, w_ref[...]).astype(o_ref.dtype)


def _mem_kv(mem, g, w):
    BM, D = mem.shape
    N = w.shape[1]
    M = 256
    return pl.pallas_call(
        _mem_kv_kernel,
        grid=(BM // M,),
        in_specs=[
            pl.BlockSpec((M, D), lambda b: (b, 0)),
            pl.BlockSpec((1, D), lambda b: (0, 0)),
            pl.BlockSpec((D, N), lambda b: (0, 0)),
        ],
        out_specs=pl.BlockSpec((M, N), lambda b: (b, 0)),
        out_shape=jax.ShapeDtypeStruct((BM, N), BF16),
        compiler_params=_params("parallel"),
        name="memory_kv",
    )(mem, g, w)


def _mix_out_kernel(tok_ref, q_ref, kv_ref, w_ref, x_ref, g_ref, o_ref, mo_ref):
    mem_w = q_ref.shape[1]
    mix_w = tok_ref.shape[1]
    for h in range(mem_w // HEAD_DIM):
        cols = slice(h * HEAD_DIM, (h + 1) * HEAD_DIM)
        vcols = slice(mem_w + h * HEAD_DIM, mem_w + (h + 1) * HEAD_DIM)
        s = _dot_nt(q_ref[:, cols], kv_ref[:, cols])
        mo_ref[:, cols] = _softmax_pv(s, kv_ref[:, vcols]).astype(mo_ref.dtype)
    o = _dot(tok_ref[...], w_ref[:mix_w, :]) + _dot(mo_ref[...], w_ref[mix_w:, :])
    o_ref[...] = x_ref[...] + _rmsnorm(o, g_ref[...])


def _mix_out(tok, proj, kv, w_out, x, g, L, tm):
    T, D = x.shape
    mix_w = tok.shape[1]
    mem_w = w_out.shape[0] - mix_w
    M = kv.shape[0] // (T // L)
    qblk = (proj.shape[1] - mem_w) // mem_w
    per_b = L // tm
    return pl.pallas_call(
        _mix_out_kernel,
        grid=(T // tm,),
        in_specs=[
            pl.BlockSpec((tm, mix_w), lambda i: (i, 0)),
            pl.BlockSpec((tm, mem_w), lambda i: (i, qblk)),
            pl.BlockSpec((M, 2 * mem_w), lambda i: (i // per_b, 0)),
            pl.BlockSpec((mix_w + mem_w, D), lambda i: (0, 0)),
            pl.BlockSpec((tm, D), lambda i: (i, 0)),
            pl.BlockSpec((1, D), lambda i: (0, 0)),
        ],
        out_specs=pl.BlockSpec((tm, D), lambda i: (i, 0)),
        out_shape=jax.ShapeDtypeStruct((T, D), F32),
        scratch_shapes=[pltpu.VMEM((tm, mem_w), BF16)],
        compiler_params=_params("parallel"),
        name="mix_out_proj",
    )(tok, proj, kv, w_out, x, g)


def _mlp_kernel(x_ref, gpre_ref, wu_ref, wd_ref, gpost_ref, o_ref, h_ref, acc_ref):
    f = pl.program_id(1)
    nf = pl.num_programs(1)

    @pl.when(f == 0)
    def _():
        h_ref[...] = _rmsnorm(x_ref[...], gpre_ref[...]).astype(h_ref.dtype)

    a = jnp.square(jnp.maximum(_dot(h_ref[...], wu_ref[...]), 0.0)).astype(BF16)
    d = _dot(a, wd_ref[...])

    @pl.when(f == 0)
    def _():
        acc_ref[...] = d

    @pl.when(f > 0)
    def _():
        acc_ref[...] += d

    @pl.when(f == nf - 1)
    def _():
        o_ref[...] = x_ref[...] + _rmsnorm(acc_ref[...], gpost_ref[...])


def _mlp(x, gpre, w_up, w_down, gpost, tm, tf):
    T, D = x.shape
    F = w_up.shape[1]
    return pl.pallas_call(
        _mlp_kernel,
        grid=(T // tm, F // tf),
        in_specs=[
            pl.BlockSpec((tm, D), lambda i, f: (i, 0)),
            pl.BlockSpec((1, D), lambda i, f: (0, 0)),
            pl.BlockSpec((D, tf), lambda i, f: (0, f)),
            pl.BlockSpec((tf, D), lambda i, f: (f, 0)),
            pl.BlockSpec((1, D), lambda i, f: (0, 0)),
        ],
        out_specs=pl.BlockSpec((tm, D), lambda i, f: (i, 0)),
        out_shape=jax.ShapeDtypeStruct((T, D), F32),
        scratch_shapes=[pltpu.VMEM((tm, D), BF16), pltpu.VMEM((tm, D), F32)],
        compiler_params=_params("parallel", "arbitrary"),
        name="relu2_mlp",
    )(x, gpre, w_up, w_down, gpost)


def _trunk(x, mem, wts):
    (g_mix_pre, w_in, na_rpb, hy_short_w, hy_short_b, hy_w1, hy_b1, hy_w2, hy_b2, hy_w3, hy_freq,
     hy_decay, hy_d, g_mem, w_mem_kv, w_out, g_mix_post, g_mlp_pre, w_up, w_down, g_mlp_post) = wts
    B, L, D = x.shape
    depth = w_in.shape[0]
    in_w = w_in.shape[2]
    mem_w = w_mem_kv.shape[2] // 2
    mix_w = (in_w - mem_w) // 3
    x = x.reshape(B * L, D)
    mem = mem.reshape(-1, D)
    qk_scale = HEAD_DIM ** -0.5
    row = lambda v: v.astype(F32)[None]
    for i in range(depth):
        j = i // 2
        na_layer = i % 2 == 0
        colscale = jnp.concatenate([
            jnp.full((mix_w,), qk_scale if na_layer else 1.0, F32),
            jnp.ones((2 * mix_w,), F32),
            jnp.full((mem_w,), qk_scale, F32)])[None]
        proj = _norm_matmul(x, row(g_mix_pre[i]), w_in[i].astype(BF16), colscale, tm=min(1024, L), tn=1024)
        if na_layer:
            bias = _na_bias_table(na_rpb[j], L // GRID_W)
            tok = _neighbourhood_attention(proj, bias, B, L, mix_w)
        else:
            tok = _hyena(proj, B, L, mix_w, hy_short_w[j], hy_short_b[j], hy_w1[j], hy_b1[j], hy_w2[j],
                         hy_b2[j], hy_w3[j], hy_freq[j], hy_decay[j], hy_d[j])
        kv = _mem_kv(mem, row(g_mem[i]), w_mem_kv[i].astype(BF16))
        x = _mix_out(tok, proj, kv, w_out[i].astype(BF16), x, row(g_mix_post[i]), L, tm=min(512, L))
        x = _mlp(x, row(g_mlp_pre[i]), w_up[i].astype(BF16), w_down[i].astype(BF16), row(g_mlp_post[i]),
                 tm=min(512, L), tf=1024)
    return x.reshape(B, L, D)


def kernel(x_prompt, x_sample, mem_prompt, mem_sample, g_mix_pre, w_in, na_rpb, hy_short_w, hy_short_b,
           hy_w1, hy_b1, hy_w2, hy_b2, hy_w3, hy_freq, hy_decay, hy_d, g_mem, w_mem_kv, w_out,
           g_mix_post, g_mlp_pre, w_up, w_down, g_mlp_post):
    wts = (g_mix_pre, w_in, na_rpb, hy_short_w, hy_short_b, hy_w1, hy_b1, hy_w2, hy_b2, hy_w3, hy_freq,
           hy_decay, hy_d, g_mem, w_mem_kv, w_out, g_mix_post, g_mlp_pre, w_up, w_down, g_mlp_post)
    return (_trunk(x_prompt, mem_prompt, wts), _trunk(x_sample, mem_sample, wts))
```

```python
import functools
import math

import numpy as np
import jax
import jax.numpy as jnp
from jax import lax
from jax.experimental import pallas as pl
from jax.experimental.pallas import tpu as pltpu

F32 = jnp.float32
BF16 = jnp.bfloat16

GRID_W = 64
HEAD_DIM = 128
NA_MAX_KH = 8
NA_KW = 16
FILTER_EMB = 33
FILTER_BANDS = (FILTER_EMB - 1) // 2
RMS_EPS = 1e-6
NEG_BIG = -1e30

VMEM_LIMIT_BYTES = 56 * 1024 * 1024
NA_ROWS_PER_STEP = 4
NA_HEADS_PER_STEP = 6
NA_KEY_ROWS = NA_ROWS_PER_STEP + NA_MAX_KH
MLP_CHUNKS = 2
DFT_LANE_SPLIT = 64


def _params(*sem):
    return pltpu.CompilerParams(dimension_semantics=sem, vmem_limit_bytes=VMEM_LIMIT_BYTES)


def _rmsnorm(x, g):
    ms = jnp.mean(x * x, axis=-1, keepdims=True)
    return x * lax.rsqrt(ms + RMS_EPS) * g


def _dot(a, b):
    return jnp.dot(a, b, preferred_element_type=F32)


def _dot_nt(a, b):
    return lax.dot_general(a, b, (((1,), (1,)), ((), ())), preferred_element_type=F32)


def _softmax_pv(s, v):
    m = jnp.max(s, axis=-1, keepdims=True)
    p = jnp.exp(s - m)
    l = jnp.sum(p, axis=-1, keepdims=True)
    return _dot(p.astype(BF16), v) / l


def _norm_matmul_kernel(x_ref, g_ref, w_ref, s_ref, o_ref, h_ref):
    @pl.when(pl.program_id(1) == 0)
    def _():
        h_ref[...] = _rmsnorm(x_ref[...], g_ref[...]).astype(h_ref.dtype)

    o_ref[...] = (_dot(h_ref[...], w_ref[...]) * s_ref[...]).astype(o_ref.dtype)


def _norm_matmul(x, g, w, colscale, tm, tn):
    T, D = x.shape
    N = w.shape[1]
    return pl.pallas_call(
        _norm_matmul_kernel,
        grid=(T // tm, N // tn),
        in_specs=[
            pl.BlockSpec((tm, D), lambda i, j: (i, 0)),
            pl.BlockSpec((1, D), lambda i, j: (0, 0)),
            pl.BlockSpec((D, tn), lambda i, j: (0, j)),
            pl.BlockSpec((1, tn), lambda i, j: (0, j)),
        ],
        out_specs=pl.BlockSpec((tm, tn), lambda i, j: (i, j)),
        out_shape=jax.ShapeDtypeStruct((T, N), BF16),
        scratch_shapes=[pltpu.VMEM((tm, D), BF16)],
        compiler_params=_params("parallel", "arbitrary"),
        name="norm_in_proj",
    )(x, g, w, colscale)


def _na_bias_table(rpb, rows):
    R, KR, W = NA_ROWS_PER_STEP, NA_KEY_ROWS, GRID_W
    kh = min(NA_MAX_KH, rows)
    nblk = rows // R
    qc = np.arange(W)
    cs = np.clip(qc - NA_KW // 2, 0, W - NA_KW)
    kc = np.arange(W)
    col_ok = (kc[None, :] >= cs[:, None]) & (kc[None, :] < cs[:, None] + NA_KW)
    dc = np.clip(kc[None, :] - qc[:, None] + NA_KW - 1, 0, 2 * NA_KW - 2)
    dr_all, ok_all = [], []
    for blk in (0, min(1, nblk - 1), nblk - 1):
        ks = int(np.clip(blk * R - kh // 2, 0, rows - KR))
        r = blk * R + np.arange(R)
        rs = np.clip(r - kh // 2, 0, rows - kh)
        key = ks + np.arange(KR)
        row_ok = (key[None, :] >= rs[:, None]) & (key[None, :] < rs[:, None] + kh)
        dr = np.clip(key[None, :] - r[:, None] + NA_MAX_KH - 1, 0, 2 * NA_MAX_KH - 2)
        dr_all.append(dr)
        ok_all.append(row_ok)
    dr = np.stack(dr_all)
    row_ok = np.stack(ok_all)
    per_dr = jnp.where(col_ok, rpb.astype(F32)[:, :, dc], NEG_BIG)
    blk = jnp.where(row_ok[None, :, :, :, None, None], per_dr[:, dr], NEG_BIG)
    return blk.transpose(1, 0, 2, 4, 3, 5).reshape(3, rpb.shape[0], R * W, KR * W)


def _na_kernel(q_ref, k_ref, v_ref, b_ref, o_ref, *, rows):
    R, KR, W = NA_ROWS_PER_STEP, NA_KEY_ROWS, GRID_W
    i = pl.program_id(2)
    ks = jnp.clip(i * R - min(NA_MAX_KH, rows) // 2, 0, rows - KR) * W
    ks = pl.multiple_of(ks, W)
    for h in range(NA_HEADS_PER_STEP):
        cols = slice(h * HEAD_DIM, (h + 1) * HEAD_DIM)
        q = q_ref[:, cols]
        k = k_ref[pl.ds(ks, KR * W), cols]
        v = v_ref[pl.ds(ks, KR * W), cols]
        s = _dot_nt(q, k) + b_ref[0, h]
        o_ref[:, cols] = _softmax_pv(s, v).astype(o_ref.dtype)


def _neighbourhood_attention(proj, bias, B, L, mix_w):
    R, KR, W = NA_ROWS_PER_STEP, NA_KEY_ROWS, GRID_W
    rows = L // W
    nblk = rows // R
    hw = NA_HEADS_PER_STEP * HEAD_DIM
    ngrp = mix_w // hw
    assert rows % R == 0 and rows >= KR and nblk >= 2 and R >= NA_MAX_KH // 2

    def variant(i):
        return jnp.where(i == 0, 0, jnp.where(i == nblk - 1, 2, 1))

    return pl.pallas_call(
        functools.partial(_na_kernel, rows=rows),
        grid=(B, ngrp, nblk),
        in_specs=[
            pl.BlockSpec((R * W, hw), lambda b, g, i: (b * nblk + i, g)),
            pl.BlockSpec((L, hw), lambda b, g, i: (b, ngrp + g)),
            pl.BlockSpec((L, hw), lambda b, g, i: (b, 2 * ngrp + g)),
            pl.BlockSpec((1, NA_HEADS_PER_STEP, R * W, KR * W), lambda b, g, i: (variant(i), g, 0, 0)),
        ],
        out_specs=pl.BlockSpec((R * W, hw), lambda b, g, i: (b * nblk + i, g)),
        out_shape=jax.ShapeDtypeStruct((B * L, mix_w), BF16),
        compiler_params=_params("parallel", "parallel", "arbitrary"),
        name="neighbourhood_attention",
    )(proj, proj, proj, bias)


def _position_features(L, shift):
    pos = jnp.arange(L, dtype=F32) + shift
    t = pos / (L - 1)
    wpos = 2.0 * math.pi * pos / L
    fb = jnp.linspace(1e-4, FILTER_BANDS - 1, FILTER_BANDS, dtype=F32)
    ang = wpos[:, None] * fb[None, :]
    z = jnp.concatenate([t[:, None], jnp.cos(ang), -jnp.sin(ang)], axis=-1)
    return jnp.pad(z, ((0, 0), (0, 128 - FILTER_EMB)))


def _hy_filter_kernel(zf_ref, zb_ref, w1_ref, b1_ref, w2_ref, b2_ref, w3_ref, fr_ref, dec_ref,
                      hs_ref, hd_ref, *, L, mix_w):
    hp = lax.Precision.HIGHEST
    fr = fr_ref[...]

    def hidden(z):
        h = jnp.sin(fr * (jnp.dot(z, w1_ref[...], precision=hp, preferred_element_type=F32) + b1_ref[...]))
        return jnp.sin(fr * (jnp.dot(h, w2_ref[...], precision=hp, preferred_element_type=F32) + b2_ref[...]))

    def taps(h, z, d):
        w3 = w3_ref[:, d * mix_w:(d + 1) * mix_w]
        win = jnp.exp(-z[:, 0:1] * jnp.abs(dec_ref[d:d + 1, :]))
        return jnp.dot(h, w3, precision=hp, preferred_element_type=F32) * win

    tl = zf_ref.shape[0]
    row = pl.program_id(0) * tl + lax.broadcasted_iota(jnp.int32, (tl, 1), 0)
    zf, zb = zf_ref[...], zb_ref[...]
    hf_hidden = hidden(zf)
    h_fwd = taps(hf_hidden, zf, 0) + jnp.where(row == 0, taps(hf_hidden, zf, 1), 0.0)
    h_bwd = jnp.where(row + 1 < L, taps(hidden(zb), zb, 1), 0.0)
    hs_ref[...] = (h_fwd + h_bwd).astype(hs_ref.dtype)
    hd_ref[...] = (h_bwd - h_fwd).astype(hd_ref.dtype)


def _hyena_filter_taps(L, w1, b1, w2, b2, w3, freq, decay, tl):
    mix_w = decay.shape[1]
    order = w2.shape[0]
    zf = _position_features(L, 0)
    zb = _position_features(L, 1)
    w1p = jnp.pad(w1.astype(F32), ((0, 128 - FILTER_EMB), (0, 0)))
    full = lambda shape: pl.BlockSpec(shape, lambda i: (0, 0))
    return pl.pallas_call(
        functools.partial(_hy_filter_kernel, L=L, mix_w=mix_w),
        grid=(L // tl,),
        in_specs=[
            pl.BlockSpec((tl, 128), lambda i: (i, 0)),
            pl.BlockSpec((tl, 128), lambda i: (i, 0)),
            full((128, order)), full((1, order)), full((order, order)), full((1, order)),
            full((order, 2 * mix_w)), full((1, order)), full((2, mix_w)),
        ],
        out_specs=[pl.BlockSpec((tl, mix_w), lambda i: (i, 0))] * 2,
        out_shape=[jax.ShapeDtypeStruct((L, mix_w), BF16)] * 2,
        compiler_params=_params("parallel"),
        name="hyena_filter_taps",
    )(zf, zb, w1p, b1.astype(F32)[None], w2.astype(F32), b2.astype(F32)[None], w3.astype(F32),
      freq.astype(F32)[None], decay.astype(F32))


def _dft_matrices(L):
    S = DFT_LANE_SPLIT
    unit = math.pi / (4 * L)
    odd = 2 * jnp.arange(L, dtype=jnp.int32) + 1

    def angle(cols):
        return ((odd[:, None] * cols[None, :]) % (8 * L)).astype(F32) * unit

    a = angle(2 * S * jnp.arange(L // S, dtype=jnp.int32))[:, :, None]
    b = angle(odd[:S])[:, None, :]
    ca, sa, cb, sb = jnp.cos(a), jnp.sin(a), jnp.cos(b), jnp.sin(b)
    mc = (ca * cb - sa * sb).reshape(L, L).astype(BF16)
    ms = (sa * cb + ca * sb).reshape(L, L).astype(BF16)
    half = odd.astype(F32) * unit
    return mc, ms, jnp.cos(half)[:, None], jnp.sin(half)[:, None]


def _filter_spectrum_kernel(mc_ref, ms_ref, hs_ref, hd_ref, cr_ref, sr_ref, gr_ref, gi_ref, *, scale):
    a = _dot(mc_ref[...], hs_ref[...])
    b = _dot(ms_ref[...], hd_ref[...])
    cr, sr = cr_ref[...] * scale, sr_ref[...] * scale
    gr_ref[...] = cr * a - sr * b
    gi_ref[...] = sr * a + cr * b


def _filter_spectrum(mc, ms, hs, hd, cr, sr, tk, cb):
    L, C = hs.shape
    return pl.pallas_call(
        functools.partial(_filter_spectrum_kernel, scale=1.0 / L),
        grid=(L // tk, C // cb),
        in_specs=[
            pl.BlockSpec((tk, L), lambda k, c: (k, 0)),
            pl.BlockSpec((tk, L), lambda k, c: (k, 0)),
            pl.BlockSpec((L, cb), lambda k, c: (0, c)),
            pl.BlockSpec((L, cb), lambda k, c: (0, c)),
            pl.BlockSpec((tk, 1), lambda k, c: (k, 0)),
            pl.BlockSpec((tk, 1), lambda k, c: (k, 0)),
        ],
        out_specs=[pl.BlockSpec((tk, cb), lambda k, c: (k, c))] * 2,
        out_shape=[jax.ShapeDtypeStruct((L, C), F32)] * 2,
        compiler_params=_params("parallel", "arbitrary"),
        name="hyena_filter_spectrum",
    )(mc, ms, hs, hd, cr, sr)


def _short_conv(ref, w_ref, b_ref, start, size, L):
    halo = 16
    cur = ref[pl.ds(start, size), :].astype(F32)
    prev_start = pl.multiple_of(jnp.maximum(start - halo, 0), halo)
    next_start = pl.multiple_of(jnp.minimum(start + size, L - halo), halo)
    prev = ref[pl.ds(prev_start, halo), :][halo - 1:halo, :].astype(F32)
    nxt = ref[pl.ds(next_start, halo), :][0:1, :].astype(F32)
    prev = jnp.where(start == 0, 0.0, prev)
    nxt = jnp.where(start + size == L, 0.0, nxt)
    row = lax.broadcasted_iota(jnp.int32, (size, 1), 0)
    up = jnp.where(row == 0, prev, pltpu.roll(cur, 1, axis=0))
    dn = jnp.where(row == size - 1, nxt, pltpu.roll(cur, size - 1, axis=0))
    return up * w_ref[0:1, :] + cur * w_ref[1:2, :] + dn * w_ref[2:3, :] + b_ref[...]


def _hy_gate_kernel(x0_ref, x1_ref, v_ref, w0_ref, w1_ref, wv_ref, b0_ref, b1_ref, bv_ref,
                    x0c_ref, z_ref, *, chunk):
    L = x0_ref.shape[0]

    def body(c, carry):
        start = pl.multiple_of(c * chunk, chunk)
        rows = pl.ds(start, chunk)
        x0c_ref[rows, :] = _short_conv(x0_ref, w0_ref, b0_ref, start, chunk, L).astype(x0c_ref.dtype)
        x1c = _short_conv(x1_ref, w1_ref, b1_ref, start, chunk, L)
        vc = _short_conv(v_ref, wv_ref, bv_ref, start, chunk, L)
        z_ref[rows, :] = (x1c * vc).astype(z_ref.dtype)
        return carry

    lax.fori_loop(0, L // chunk, body, 0)


def _hyena_gate(proj, short_w, short_b, B, L, mix_w, cb, chunk):
    nc = mix_w // cb
    col = lambda part: pl.BlockSpec((L, cb), lambda b, c: (b, part * nc + c))
    wspec = lambda part: pl.BlockSpec((3, cb), lambda b, c: (0, part * nc + c))
    bspec = lambda part: pl.BlockSpec((1, cb), lambda b, c: (0, part * nc + c))
    return pl.pallas_call(
        functools.partial(_hy_gate_kernel, chunk=chunk),
        grid=(B, nc),
        in_specs=[col(0), col(1), col(2), wspec(0), wspec(1), wspec(2), bspec(0), bspec(1), bspec(2)],
        out_specs=[pl.BlockSpec((L, cb), lambda b, c: (b, c))] * 2,
        out_shape=[jax.ShapeDtypeStruct((B * L, mix_w), BF16)] * 2,
        compiler_params=_params("parallel", "parallel"),
        name="hyena_gate",
    )(proj, proj, proj, short_w, short_w, short_w, short_b, short_b, short_b)


def _dft_fwd_kernel(mc_ref, ms_ref, z_ref, gr_ref, gi_ref, y_ref):
    z = z_ref[...]
    zr = _dot(mc_ref[...], z)
    zs = _dot(ms_ref[...], z)
    gr, gi = gr_ref[...], gi_ref[...]
    y_ref[0] = (zr * gr + zs * gi).astype(y_ref.dtype)
    y_ref[1] = (zr * gi - zs * gr).astype(y_ref.dtype)


def _dft_forward(mc, ms, z, gr, gi, B, L, tk, cb):
    C = z.shape[1]
    out = pl.pallas_call(
        _dft_fwd_kernel,
        grid=(L // tk, C // cb, B),
        in_specs=[
            pl.BlockSpec((tk, L), lambda k, c, b: (k, 0)),
            pl.BlockSpec((tk, L), lambda k, c, b: (k, 0)),
            pl.BlockSpec((L, cb), lambda k, c, b: (b, c)),
            pl.BlockSpec((tk, cb), lambda k, c, b: (k, c)),
            pl.BlockSpec((tk, cb), lambda k, c, b: (k, c)),
        ],
        out_specs=pl.BlockSpec((None, 2, tk, cb), lambda k, c, b: (b, 0, k, c)),
        out_shape=jax.ShapeDtypeStruct((B, 2, L, C), BF16),
        compiler_params=_params("parallel", "parallel", "arbitrary"),
        name="hyena_dft_forward",
    )(mc, ms, z, gr, gi)
    return out


def _dft_inv_kernel(mc_ref, ms_ref, y_ref, z_ref, x0_ref, d_ref, o_ref):
    y = _dot(mc_ref[...], y_ref[0]) - _dot(ms_ref[...], y_ref[1])
    y = y + d_ref[...] * z_ref[...].astype(F32)
    o_ref[...] = (x0_ref[...].astype(F32) * y).astype(o_ref.dtype)


def _dft_inverse(mc, ms, y, z, x0c, d, B, L, tt, cb):
    C = z.shape[1]
    nt = L // tt
    return pl.pallas_call(
        _dft_inv_kernel,
        grid=(nt, C // cb, B),
        in_specs=[
            pl.BlockSpec((tt, L), lambda t, c, b: (t, 0)),
            pl.BlockSpec((tt, L), lambda t, c, b: (t, 0)),
            pl.BlockSpec((None, 2, L, cb), lambda t, c, b: (b, 0, 0, c)),
            pl.BlockSpec((tt, cb), lambda t, c, b: (b * nt + t, c)),
            pl.BlockSpec((tt, cb), lambda t, c, b: (b * nt + t, c)),
            pl.BlockSpec((1, cb), lambda t, c, b: (0, c)),
        ],
        out_specs=pl.BlockSpec((tt, cb), lambda t, c, b: (b * nt + t, c)),
        out_shape=jax.ShapeDtypeStruct((B * L, C), BF16),
        compiler_params=_params("parallel", "parallel", "arbitrary"),
        name="hyena_dft_inverse",
    )(mc, ms, y, z, x0c, d)


def _hyena(proj, B, L, mix_w, short_w, short_b, w1, b1, w2, b2, w3, freq, decay, d_bias):
    tk = min(512, L)
    cb = 512
    hs, hd = _hyena_filter_taps(L, w1, b1, w2, b2, w3, freq, decay, tl=min(512, L))
    mc, ms, cr, sr = _dft_matrices(L)
    gr, gi = _filter_spectrum(mc, ms, hs, hd, cr, sr, tk, cb)
    x0c, z = _hyena_gate(proj, short_w.astype(F32), short_b.astype(F32)[None], B, L, mix_w,
                         cb=256, chunk=min(256, L))
    y = _dft_forward(mc, ms, z, gr, gi, B, L, tk, cb)
    return _dft_inverse(mc, ms, y, z, x0c, d_bias.astype(F32)[None], B, L, tk, cb)


def _mem_kv_kernel(m_ref, g_ref, w_ref, o_ref):
    h = _rmsnorm(m_ref[...], g_ref[...]).astype(BF16)
    o_ref[...] = _dot(h, w_ref[...]).astype(o_ref.dtype)


def _mem_kv(mem, g, w):
    BM, D = mem.shape
    N = w.shape[1]
    M = 256
    return pl.pallas_call(
        _mem_kv_kernel,
        grid=(BM // M,),
        in_specs=[
            pl.BlockSpec((M, D), lambda b: (b, 0)),
            pl.BlockSpec((1, D), lambda b: (0, 0)),
            pl.BlockSpec((D, N), lambda b: (0, 0)),
        ],
        out_specs=pl.BlockSpec((M, N), lambda b: (b, 0)),
        out_shape=jax.ShapeDtypeStruct((BM, N), BF16),
        compiler_params=_params("parallel"),
        name="memory_kv",
    )(mem, g, w)


def _mix_out_kernel(tok_ref, q_ref, kv_ref, w_ref, x_ref, g_ref, o_ref, mo_ref):
    mem_w = q_ref.shape[1]
    mix_w = tok_ref.shape[1]
    for h in range(mem_w // HEAD_DIM):
        cols = slice(h * HEAD_DIM, (h + 1) * HEAD_DIM)
        vcols = slice(mem_w + h * HEAD_DIM, mem_w + (h + 1) * HEAD_DIM)
        s = _dot_nt(q_ref[:, cols], kv_ref[:, cols])
        mo_ref[:, cols] = _softmax_pv(s, kv_ref[:, vcols]).astype(mo_ref.dtype)
    o = _dot(tok_ref[...], w_ref[:mix_w, :]) + _dot(mo_ref[...], w_ref[mix_w:, :])
    o_ref[...] = x_ref[...] + _rmsnorm(o, g_ref[...])


def _mix_out(tok, proj, kv, w_out, x, g, L, tm):
    T, D = x.shape
    mix_w = tok.shape[1]
    mem_w = w_out.shape[0] - mix_w
    M = kv.shape[0] // (T // L)
    qblk = (proj.shape[1] - mem_w) // mem_w
    per_b = L // tm
    return pl.pallas_call(
        _mix_out_kernel,
        grid=(T // tm,),
        in_specs=[
            pl.BlockSpec((tm, mix_w), lambda i: (i, 0)),
            pl.BlockSpec((tm, mem_w), lambda i: (i, qblk)),
            pl.BlockSpec((M, 2 * mem_w), lambda i: (i // per_b, 0)),
            pl.BlockSpec((mix_w + mem_w, D), lambda i: (0, 0)),
            pl.BlockSpec((tm, D), lambda i: (i, 0)),
            pl.BlockSpec((1, D), lambda i: (0, 0)),
        ],
        out_specs=pl.BlockSpec((tm, D), lambda i: (i, 0)),
        out_shape=jax.ShapeDtypeStruct((T, D), F32),
        scratch_shapes=[pltpu.VMEM((tm, mem_w), BF16)],
        compiler_params=_params("parallel"),
        name="mix_out_proj",
    )(tok, proj, kv, w_out, x, g)


def _mlp_kernel(x_ref, gpre_ref, wu_ref, wd_ref, gpost_ref, o_ref, h_ref, acc_ref):
    f = pl.program_id(1)
    nf = pl.num_programs(1)

    @pl.when(f == 0)
    def _():
        h_ref[...] = _rmsnorm(x_ref[...], gpre_ref[...]).astype(h_ref.dtype)
        acc_ref[...] = jnp.zeros_like(acc_ref)

    h = h_ref[...]
    tf = wu_ref.shape[1]
    cw = tf // MLP_CHUNKS
    d = None
    for c in range(MLP_CHUNKS):
        a = jnp.square(jnp.maximum(_dot(h, wu_ref[:, c * cw:(c + 1) * cw]), 0.0)).astype(BF16)
        dc = _dot(a, wd_ref[c * cw:(c + 1) * cw, :])
        d = dc if d is None else d + dc
    acc_ref[...] += d

    @pl.when(f == nf - 1)
    def _():
        o_ref[...] = x_ref[...] + _rmsnorm(acc_ref[...], gpost_ref[...])


def _mlp(x, gpre, w_up, w_down, gpost, tm, tf):
    T, D = x.shape
    F = w_up.shape[1]
    return pl.pallas_call(
        _mlp_kernel,
        grid=(T // tm, F // tf),
        in_specs=[
            pl.BlockSpec((tm, D), lambda i, f: (i, 0)),
            pl.BlockSpec((1, D), lambda i, f: (0, 0)),
            pl.BlockSpec((D, tf), lambda i, f: (0, f)),
            pl.BlockSpec((tf, D), lambda i, f: (f, 0)),
            pl.BlockSpec((1, D), lambda i, f: (0, 0)),
        ],
        out_specs=pl.BlockSpec((tm, D), lambda i, f: (i, 0)),
        out_shape=jax.ShapeDtypeStruct((T, D), F32),
        scratch_shapes=[pltpu.VMEM((tm, D), BF16), pltpu.VMEM((tm, D), F32)],
        compiler_params=_params("parallel", "arbitrary"),
        name="relu2_mlp",
    )(x, gpre, w_up, w_down, gpost)


def _trunk(x, mem, wts):
    (g_mix_pre, w_in, na_rpb, hy_short_w, hy_short_b, hy_w1, hy_b1, hy_w2, hy_b2, hy_w3, hy_freq,
     hy_decay, hy_d, g_mem, w_mem_kv, w_out, g_mix_post, g_mlp_pre, w_up, w_down, g_mlp_post) = wts
    B, L, D = x.shape
    depth = w_in.shape[0]
    in_w = w_in.shape[2]
    mem_w = w_mem_kv.shape[2] // 2
    mix_w = (in_w - mem_w) // 3
    x = x.reshape(B * L, D)
    mem = mem.reshape(-1, D)
    qk_scale = HEAD_DIM ** -0.5
    row = lambda v: v.astype(F32)[None]
    for i in range(depth):
        j = i // 2
        na_layer = i % 2 == 0
        colscale = jnp.concatenate([
            jnp.full((mix_w,), qk_scale if na_layer else 1.0, F32),
            jnp.ones((2 * mix_w,), F32),
            jnp.full((mem_w,), qk_scale, F32)])[None]
        proj = _norm_matmul(x, row(g_mix_pre[i]), w_in[i].astype(BF16), colscale, tm=min(1024, L), tn=1280)
        if na_layer:
            bias = _na_bias_table(na_rpb[j], L // GRID_W)
            tok = _neighbourhood_attention(proj, bias, B, L, mix_w)
        else:
            tok = _hyena(proj, B, L, mix_w, hy_short_w[j], hy_short_b[j], hy_w1[j], hy_b1[j], hy_w2[j],
                         hy_b2[j], hy_w3[j], hy_freq[j], hy_decay[j], hy_d[j])
        kv = _mem_kv(mem, row(g_mem[i]), w_mem_kv[i].astype(BF16))
        x = _mix_out(tok, proj, kv, w_out[i].astype(BF16), x, row(g_mix_post[i]), L, tm=min(512, L))
        x = _mlp(x, row(g_mlp_pre[i]), w_up[i].astype(BF16), w_down[i].astype(BF16), row(g_mlp_post[i]),
                 tm=min(512, L), tf=1024)
    return x.reshape(B, L, D)


def kernel(x_prompt, x_sample, mem_prompt, mem_sample, g_mix_pre, w_in, na_rpb, hy_short_w, hy_short_b,
           hy_w1, hy_b1, hy_w2, hy_b2, hy_w3, hy_freq, hy_decay, hy_d, g_mem, w_mem_kv, w_out,
           g_mix_post, g_mlp_pre, w_up, w_down, g_mlp_post):
    wts = (g_mix_pre, w_in, na_rpb, hy_short_w, hy_short_b, hy_w1, hy_b1, hy_w2, hy_b2, hy_w3, hy_freq,
           hy_decay, hy_d, g_mem, w_mem_kv, w_out, g_mix_post, g_mlp_pre, w_up, w_down, g_mlp_post)
    return (_trunk(x_prompt, mem_prompt, wts), _trunk(x_sample, mem_sample, wts))
```

```python
import functools
import math

import numpy as np
import jax
import jax.numpy as jnp
from jax import lax
from jax.experimental import pallas as pl
from jax.experimental.pallas import tpu as pltpu

F32 = jnp.float32
BF16 = jnp.bfloat16

GRID_W = 64
HEAD_DIM = 128
NA_MAX_KH = 8
NA_KW = 16
FILTER_EMB = 33
FILTER_BANDS = (FILTER_EMB - 1) // 2
RMS_EPS = 1e-6
NEG_BIG = -1e30

VMEM_LIMIT_BYTES = 56 * 1024 * 1024
NA_ROWS_PER_STEP = 4
NA_HEADS_PER_STEP = 6
NA_KEY_ROWS = NA_ROWS_PER_STEP + NA_MAX_KH
MLP_CHUNKS = 2
DFT_P = 32
DFT_MG = 8


def _params(*sem):
    return pltpu.CompilerParams(dimension_semantics=sem, vmem_limit_bytes=VMEM_LIMIT_BYTES)


def _rmsnorm(x, g):
    ms = jnp.mean(x * x, axis=-1, keepdims=True)
    return x * lax.rsqrt(ms + RMS_EPS) * g


def _dot(a, b):
    return jnp.dot(a, b, preferred_element_type=F32)


def _dot_nt(a, b):
    return lax.dot_general(a, b, (((1,), (1,)), ((), ())), preferred_element_type=F32)


def _softmax_pv(s, v):
    m = jnp.max(s, axis=-1, keepdims=True)
    p = jnp.exp(s - m)
    l = jnp.sum(p, axis=-1, keepdims=True)
    return _dot(p.astype(BF16), v) / l


def _norm_matmul_kernel(x_ref, g_ref, w_ref, s_ref, o_ref, h_ref):
    @pl.when(pl.program_id(1) == 0)
    def _():
        h_ref[...] = _rmsnorm(x_ref[...], g_ref[...]).astype(h_ref.dtype)

    o_ref[...] = (_dot(h_ref[...], w_ref[...]) * s_ref[...]).astype(o_ref.dtype)


def _norm_matmul(x, g, w, colscale, tm, tn):
    T, D = x.shape
    N = w.shape[1]
    return pl.pallas_call(
        _norm_matmul_kernel,
        grid=(T // tm, N // tn),
        in_specs=[
            pl.BlockSpec((tm, D), lambda i, j: (i, 0)),
            pl.BlockSpec((1, D), lambda i, j: (0, 0)),
            pl.BlockSpec((D, tn), lambda i, j: (0, j)),
            pl.BlockSpec((1, tn), lambda i, j: (0, j)),
        ],
        out_specs=pl.BlockSpec((tm, tn), lambda i, j: (i, j)),
        out_shape=jax.ShapeDtypeStruct((T, N), BF16),
        scratch_shapes=[pltpu.VMEM((tm, D), BF16)],
        compiler_params=_params("parallel", "arbitrary"),
        name="norm_in_proj",
    )(x, g, w, colscale)


def _na_bias_table(rpb, rows):
    R, KR, W = NA_ROWS_PER_STEP, NA_KEY_ROWS, GRID_W
    kh = min(NA_MAX_KH, rows)
    nblk = rows // R
    qc = np.arange(W)
    cs = np.clip(qc - NA_KW // 2, 0, W - NA_KW)
    kc = np.arange(W)
    col_ok = (kc[None, :] >= cs[:, None]) & (kc[None, :] < cs[:, None] + NA_KW)
    dc = np.clip(kc[None, :] - qc[:, None] + NA_KW - 1, 0, 2 * NA_KW - 2)
    dr_all, ok_all = [], []
    for blk in (0, min(1, nblk - 1), nblk - 1):
        ks = int(np.clip(blk * R - kh // 2, 0, rows - KR))
        r = blk * R + np.arange(R)
        rs = np.clip(r - kh // 2, 0, rows - kh)
        key = ks + np.arange(KR)
        row_ok = (key[None, :] >= rs[:, None]) & (key[None, :] < rs[:, None] + kh)
        dr = np.clip(key[None, :] - r[:, None] + NA_MAX_KH - 1, 0, 2 * NA_MAX_KH - 2)
        dr_all.append(dr)
        ok_all.append(row_ok)
    dr = np.stack(dr_all)
    row_ok = np.stack(ok_all)
    per_dr = jnp.where(col_ok, rpb.astype(F32)[:, :, dc], NEG_BIG)
    blk = jnp.where(row_ok[None, :, :, :, None, None], per_dr[:, dr], NEG_BIG)
    return blk.transpose(1, 0, 2, 4, 3, 5).reshape(3, rpb.shape[0], R * W, KR * W)


def _na_kernel(q_ref, k_ref, v_ref, b_ref, o_ref, *, rows):
    R, KR, W = NA_ROWS_PER_STEP, NA_KEY_ROWS, GRID_W
    i = pl.program_id(2)
    ks = jnp.clip(i * R - min(NA_MAX_KH, rows) // 2, 0, rows - KR) * W
    ks = pl.multiple_of(ks, W)
    for h in range(NA_HEADS_PER_STEP):
        cols = slice(h * HEAD_DIM, (h + 1) * HEAD_DIM)
        q = q_ref[:, cols]
        k = k_ref[pl.ds(ks, KR * W), cols]
        v = v_ref[pl.ds(ks, KR * W), cols]
        s = _dot_nt(q, k) + b_ref[0, h]
        o_ref[:, cols] = _softmax_pv(s, v).astype(o_ref.dtype)


def _neighbourhood_attention(proj, bias, B, L, mix_w):
    R, KR, W = NA_ROWS_PER_STEP, NA_KEY_ROWS, GRID_W
    rows = L // W
    nblk = rows // R
    hw = NA_HEADS_PER_STEP * HEAD_DIM
    ngrp = mix_w // hw
    assert rows % R == 0 and rows >= KR and nblk >= 2 and R >= NA_MAX_KH // 2

    def variant(i):
        return jnp.where(i == 0, 0, jnp.where(i == nblk - 1, 2, 1))

    return pl.pallas_call(
        functools.partial(_na_kernel, rows=rows),
        grid=(B, ngrp, nblk),
        in_specs=[
            pl.BlockSpec((R * W, hw), lambda b, g, i: (b * nblk + i, g)),
            pl.BlockSpec((L, hw), lambda b, g, i: (b, ngrp + g)),
            pl.BlockSpec((L, hw), lambda b, g, i: (b, 2 * ngrp + g)),
            pl.BlockSpec((1, NA_HEADS_PER_STEP, R * W, KR * W), lambda b, g, i: (variant(i), g, 0, 0)),
        ],
        out_specs=pl.BlockSpec((R * W, hw), lambda b, g, i: (b * nblk + i, g)),
        out_shape=jax.ShapeDtypeStruct((B * L, mix_w), BF16),
        compiler_params=_params("parallel", "parallel", "arbitrary"),
        name="neighbourhood_attention",
    )(proj, proj, proj, bias)


def _position_features(L):
    P = DFT_P
    pos = (jnp.arange(L // P, dtype=F32)[None, :] * P + jnp.arange(P, dtype=F32)[:, None]).reshape(L)
    t = pos / (L - 1)
    wpos = 2.0 * math.pi * pos / L
    fb = jnp.linspace(1e-4, FILTER_BANDS - 1, FILTER_BANDS, dtype=F32)
    ang = wpos[:, None] * fb[None, :]
    z = jnp.concatenate([t[:, None], jnp.cos(ang), -jnp.sin(ang)], axis=-1)
    return jnp.pad(z, ((0, 0), (0, 128 - FILTER_EMB)))


def _hy_filter_kernel(z_ref, w1_ref, b1_ref, w2_ref, b2_ref, w3_ref, fr_ref, dec_ref, h_ref, *, mix_w):
    hp = lax.Precision.HIGHEST
    fr = fr_ref[...]
    z = z_ref[...]
    h = jnp.sin(fr * (jnp.dot(z, w1_ref[...], precision=hp, preferred_element_type=F32) + b1_ref[...]))
    h = jnp.sin(fr * (jnp.dot(h, w2_ref[...], precision=hp, preferred_element_type=F32) + b2_ref[...]))

    def taps(d):
        w3 = w3_ref[:, d * mix_w:(d + 1) * mix_w]
        win = jnp.exp(-z[:, 0:1] * jnp.abs(dec_ref[d:d + 1, :]))
        return jnp.dot(h, w3, precision=hp, preferred_element_type=F32) * win

    tl = z_ref.shape[0]
    lag0 = pl.program_id(0) * tl + lax.broadcasted_iota(jnp.int32, (tl, 1), 0) == 0
    h_fwd, h_bwd = taps(0), taps(1)
    h_ref[0] = h_fwd + jnp.where(lag0, h_bwd, 0.0)
    h_ref[1] = jnp.where(lag0, 0.0, h_bwd)


def _hyena_filter_taps(L, w1, b1, w2, b2, w3, freq, decay, tl):
    mix_w = decay.shape[1]
    order = w2.shape[0]
    w1p = jnp.pad(w1.astype(F32), ((0, 128 - FILTER_EMB), (0, 0)))
    full = lambda shape: pl.BlockSpec(shape, lambda i: (0, 0))
    return pl.pallas_call(
        functools.partial(_hy_filter_kernel, mix_w=mix_w),
        grid=(L // tl,),
        in_specs=[
            pl.BlockSpec((tl, 128), lambda i: (i, 0)),
            full((128, order)), full((1, order)), full((order, order)), full((1, order)),
            full((order, 2 * mix_w)), full((1, order)), full((2, mix_w)),
        ],
        out_specs=pl.BlockSpec((2, tl, mix_w), lambda i: (0, i, 0)),
        out_shape=jax.ShapeDtypeStruct((2, L, mix_w), F32),
        compiler_params=_params("parallel"),
        name="hyena_filter_taps",
    )(_position_features(L), w1p, b1.astype(F32)[None], w2.astype(F32), b2.astype(F32)[None],
      w3.astype(F32), freq.astype(F32)[None], decay.astype(F32))


def _dft_tables(L):
    P, MG = DFT_P, DFT_MG
    A, n = L // P, 2 * L
    Mm, J = 2 * A, P // 2
    NG = Mm // MG
    ar = lambda size: jnp.arange(size, dtype=jnp.int32)
    ang1 = (((2 * ar(Mm) + 1)[:, None] * ar(A)[None, :]) % (2 * Mm)).astype(F32) * (math.pi / Mm)
    w1 = jnp.concatenate([jnp.cos(ang1), -jnp.sin(ang1)], axis=0)
    k = (Mm * ar(J)[None, None, :, None] + MG * ar(NG)[:, None, None, None] + ar(MG)[None, :, None, None])
    ph = ((ar(P)[None, None, None, :] * (2 * k + 1)) % (2 * n)).astype(F32) * (math.pi / n)
    cr, ci = jnp.cos(ph), -jnp.sin(ph)
    eye = jnp.eye(MG, dtype=F32)
    blk = lambda c: c[..., None] * eye[None, :, None, None, :]
    top = jnp.stack([blk(cr), blk(-ci)], axis=3)
    bot = jnp.stack([blk(ci), blk(cr)], axis=3)
    a3 = jnp.stack([top, bot], axis=1).reshape(NG, 2 * MG * J, 2 * P * MG)
    return w1.astype(BF16), w1.T.astype(BF16), a3.astype(BF16), a3.transpose(0, 2, 1).astype(BF16)


def _short_conv(ref, w_ref, b_ref, start, size, L):
    halo = 16
    cur = ref[pl.ds(start, size), :].astype(F32)
    prev_start = pl.multiple_of(jnp.maximum(start - halo, 0), halo)
    next_start = pl.multiple_of(jnp.minimum(start + size, L - halo), halo)
    prev = ref[pl.ds(prev_start, halo), :][halo - 1:halo, :].astype(F32)
    nxt = ref[pl.ds(next_start, halo), :][0:1, :].astype(F32)
    prev = jnp.where(start == 0, 0.0, prev)
    nxt = jnp.where(start + size == L, 0.0, nxt)
    row = lax.broadcasted_iota(jnp.int32, (size, 1), 0)
    up = jnp.where(row == 0, prev, pltpu.roll(cur, 1, axis=0))
    dn = jnp.where(row == size - 1, nxt, pltpu.roll(cur, size - 1, axis=0))
    return up * w_ref[0:1, :] + cur * w_ref[1:2, :] + dn * w_ref[2:3, :] + b_ref[...]


def _hy_gate_kernel(x0_ref, x1_ref, v_ref, w0_ref, w1_ref, wv_ref, b0_ref, b1_ref, bv_ref,
                    perm_ref, x0c_ref, zb_ref, zp_ref):
    P, MG = DFT_P, DFT_MG
    chunk = P * MG
    L, cb = x0_ref.shape

    def body(c, carry):
        start = pl.multiple_of(c * chunk, chunk)
        rows = pl.ds(start, chunk)
        x0c_ref[rows, :] = _short_conv(x0_ref, w0_ref, b0_ref, start, chunk, L).astype(x0c_ref.dtype)
        x1c = _short_conv(x1_ref, w1_ref, b1_ref, start, chunk, L)
        vc = _short_conv(v_ref, wv_ref, bv_ref, start, chunk, L)
        z = (x1c * vc).astype(BF16)
        zb_ref[rows, :] = z
        zp = _dot(perm_ref[...], z).reshape(P, MG, cb)
        zp_ref[:, pl.ds(pl.multiple_of(c * MG, MG), MG), :] = zp
        return carry

    lax.fori_loop(0, L // chunk, body, 0, unroll=2)


def _time_split_permutation():
    P, MG = DFT_P, DFT_MG
    r = np.arange(P * MG)
    perm = np.zeros((P * MG, P * MG), np.float32)
    perm[r, (r % MG) * P + r // MG] = 1.0
    return perm


def _hyena_gate(proj, short_w, short_b, B, L, mix_w, cb):
    nc = mix_w // cb
    P = DFT_P
    col = lambda part: pl.BlockSpec((L, cb), lambda b, c: (b, part * nc + c))
    wspec = lambda part: pl.BlockSpec((3, cb), lambda b, c: (0, part * nc + c))
    bspec = lambda part: pl.BlockSpec((1, cb), lambda b, c: (0, part * nc + c))
    perm = jnp.asarray(_time_split_permutation(), BF16)
    seq = pl.BlockSpec((L, cb), lambda b, c: (b, c))
    return pl.pallas_call(
        _hy_gate_kernel,
        grid=(B, nc),
        in_specs=[col(0), col(1), col(2), wspec(0), wspec(1), wspec(2), bspec(0), bspec(1), bspec(2),
                  pl.BlockSpec(perm.shape, lambda b, c: (0, 0))],
        out_specs=[seq, seq, pl.BlockSpec((None, P, L // P, cb), lambda b, c: (b, 0, 0, c))],
        out_shape=[jax.ShapeDtypeStruct((B * L, mix_w), BF16), jax.ShapeDtypeStruct((B * L, mix_w), BF16),
                   jax.ShapeDtypeStruct((B, P, L // P, mix_w), F32)],
        compiler_params=_params("parallel", "parallel"),
        name="hyena_gate",
    )(proj, proj, proj, short_w, short_w, short_w, short_b, short_b, short_b, perm)


def _dft_fwd_kernel(*refs, filtered):
    if filtered:
        z_ref, w1_ref, a3_ref, g_ref, y_ref, v_ref = refs
    else:
        z_ref, w1_ref, a3_ref, y_ref, v_ref = refs
    P, MG = DFT_P, DFT_MG
    cb = z_ref.shape[2]
    ng, rows_g = a3_ref.shape[0], a3_ref.shape[1]
    half = rows_g // 2
    w1 = w1_ref[...]
    for b in range(P):
        v = _dot(w1, z_ref[b].astype(BF16))
        v_ref[:, :, b * MG:(b + 1) * MG, :] = v.reshape(2, ng, MG, cb)

    def group(g, carry):
        vg = jnp.concatenate([v_ref[0, g], v_ref[1, g]], axis=0).astype(BF16)
        x = _dot(a3_ref[g], vg)
        if filtered:
            xr, xi = x[:half], x[half:]
            gr, gi = g_ref[g, :half], g_ref[g, half:]
            x = jnp.concatenate([xr * gr - xi * gi, xr * gi + xi * gr], axis=0)
        y_ref[pl.ds(pl.multiple_of(g * rows_g, rows_g), rows_g), :] = x.astype(y_ref.dtype)
        return carry

    lax.fori_loop(0, ng, group, 0, unroll=8)


def _dft_forward(z, w1, a3, g, Bz, L, cb, out_dtype):
    C = z.shape[3]
    ng, rows_g, cols_g = a3.shape
    const = lambda shape: pl.BlockSpec(shape, lambda c, b: (0,) * len(shape), pipeline_mode=pl.Buffered(1))
    in_specs = [pl.BlockSpec((None, DFT_P, L // DFT_P, cb), lambda c, b: (b, 0, 0, c)),
                const(w1.shape), const(a3.shape)]
    args = [z, w1, a3]
    if g is not None:
        in_specs.append(pl.BlockSpec((ng, rows_g, cb), lambda c, b: (0, 0, c), pipeline_mode=pl.Buffered(1)))
        args.append(g)
    return pl.pallas_call(
        functools.partial(_dft_fwd_kernel, filtered=g is not None),
        grid=(C // cb, Bz),
        in_specs=in_specs,
        out_specs=pl.BlockSpec((None, 2 * L, cb), lambda c, b: (b, 0, c)),
        out_shape=jax.ShapeDtypeStruct((Bz, 2 * L, C), out_dtype),
        scratch_shapes=[pltpu.VMEM((2, ng, cols_g // 2, cb), F32)],
        compiler_params=_params("parallel", "arbitrary"),
        name="hyena_dft_forward",
    )(*args)


def _dft_inv_kernel(y_ref, b3_ref, w1t_ref, zb_ref, x0_ref, d_ref, unperm_ref, o_ref, q_ref, t_ref):
    P, MG = DFT_P, DFT_MG
    chunk = P * MG
    cb = y_ref.shape[1]
    ng, rows_g, cols_g = b3_ref.shape
    half = rows_g // 2

    def group(g, carry):
        yg = y_ref[pl.ds(pl.multiple_of(g * cols_g, cols_g), cols_g), :]
        q = _dot(b3_ref[g], yg)
        q_ref[0, g] = q[:half]
        q_ref[1, g] = q[half:]
        return carry

    lax.fori_loop(0, ng, group, 0, unroll=8)
    w1t = w1t_ref[...]
    for b in range(P):
        qb = q_ref[:, :, b * MG:(b + 1) * MG, :].reshape(2 * ng * MG, cb)
        t_ref[b] = _dot(w1t, qb.astype(BF16))

    unperm = unperm_ref[...]

    def gate(c, carry):
        rows = pl.ds(pl.multiple_of(c * chunk, chunk), chunk)
        yp = t_ref[:, pl.ds(pl.multiple_of(c * MG, MG), MG), :].reshape(chunk, cb)
        hi = yp.astype(BF16)
        lo = (yp - hi.astype(F32)).astype(BF16)
        y = _dot(unperm, hi) + _dot(unperm, lo)
        y = y + d_ref[...] * zb_ref[rows, :].astype(F32)
        o_ref[rows, :] = (x0_ref[rows, :].astype(F32) * y).astype(o_ref.dtype)
        return carry

    lax.fori_loop(0, zb_ref.shape[0] // chunk, gate, 0, unroll=4)


def _dft_inverse(y, b3, w1t, zb, x0c, d, B, L, cb):
    C = zb.shape[1]
    ng, rows_g, _ = b3.shape
    const = lambda shape: pl.BlockSpec(shape, lambda c, b: (0,) * len(shape), pipeline_mode=pl.Buffered(1))
    seq = pl.BlockSpec((L, cb), lambda c, b: (b, c))
    unperm = jnp.asarray(_time_split_permutation().T, BF16)
    return pl.pallas_call(
        _dft_inv_kernel,
        grid=(C // cb, B),
        in_specs=[pl.BlockSpec((None, 2 * L, cb), lambda c, b: (b, 0, c)), const(b3.shape), const(w1t.shape),
                  seq, seq, pl.BlockSpec((1, cb), lambda c, b: (0, c)), const(unperm.shape)],
        out_specs=seq,
        out_shape=jax.ShapeDtypeStruct((B * L, C), BF16),
        scratch_shapes=[pltpu.VMEM((2, ng, rows_g // 2, cb), F32),
                        pltpu.VMEM((DFT_P, L // DFT_P, cb), F32)],
        compiler_params=_params("parallel", "arbitrary"),
        name="hyena_dft_inverse",
    )(y, b3, w1t, zb, x0c, d, unperm)


def _hyena(proj, B, L, mix_w, short_w, short_b, w1, b1, w2, b2, w3, freq, decay, d_bias):
    cb = 256
    dw1, dw1t, a3, b3 = _dft_tables(L)
    ng, rows_g, _ = a3.shape
    half = rows_g // 2
    taps = _hyena_filter_taps(L, w1, b1, w2, b2, w3, freq, decay, tl=min(512, L))
    spec = _dft_forward(taps.reshape(2, DFT_P, L // DFT_P, mix_w), dw1, a3, None, 2, L, cb, F32)
    spec = spec.reshape(2, ng, rows_g, mix_w)
    g = (spec[0] + jnp.concatenate([spec[1, :, :half], -spec[1, :, half:]], axis=1)) * (1.0 / L)
    x0c, zb, zp = _hyena_gate(proj, short_w.astype(F32), short_b.astype(F32)[None], B, L, mix_w, cb)
    y = _dft_forward(zp, dw1, a3, g, B, L, cb, BF16)
    return _dft_inverse(y, b3, dw1t, zb, x0c, d_bias.astype(F32)[None], B, L, cb)


def _mem_kv_kernel(m_ref, g_ref, w_ref, o_ref):
    h = _rmsnorm(m_ref[...], g_ref[...]).astype(BF16)
    o_ref[...] = _dot(h, w_ref[...]).astype(o_ref.dtype)


def _mem_kv(mem, g, w):
    BM, D = mem.shape
    N = w.shape[1]
    M = 256
    return pl.pallas_call(
        _mem_kv_kernel,
        grid=(BM // M,),
        in_specs=[
            pl.BlockSpec((M, D), lambda b: (b, 0)),
            pl.BlockSpec((1, D), lambda b: (0, 0)),
            pl.BlockSpec((D, N), lambda b: (0, 0)),
        ],
        out_specs=pl.BlockSpec((M, N), lambda b: (b, 0)),
        out_shape=jax.ShapeDtypeStruct((BM, N), BF16),
        compiler_params=_params("parallel"),
        name="memory_kv",
    )(mem, g, w)


def _mix_out_kernel(tok_ref, q_ref, kv_ref, w_ref, x_ref, g_ref, o_ref, mo_ref):
    mem_w = q_ref.shape[1]
    mix_w = tok_ref.shape[1]
    for h in range(mem_w // HEAD_DIM):
        cols = slice(h * HEAD_DIM, (h + 1) * HEAD_DIM)
        vcols = slice(mem_w + h * HEAD_DIM, mem_w + (h + 1) * HEAD_DIM)
        s = _dot_nt(q_ref[:, cols], kv_ref[:, cols])
        mo_ref[:, cols] = _softmax_pv(s, kv_ref[:, vcols]).astype(mo_ref.dtype)
    o = _dot(tok_ref[...], w_ref[:mix_w, :]) + _dot(mo_ref[...], w_ref[mix_w:, :])
    o_ref[...] = x_ref[...] + _rmsnorm(o, g_ref[...])


def _mix_out(tok, proj, kv, w_out, x, g, L, tm):
    T, D = x.shape
    mix_w = tok.shape[1]
    mem_w = w_out.shape[0] - mix_w
    M = kv.shape[0] // (T // L)
    qblk = (proj.shape[1] - mem_w) // mem_w
    per_b = L // tm
    return pl.pallas_call(
        _mix_out_kernel,
        grid=(T // tm,),
        in_specs=[
            pl.BlockSpec((tm, mix_w), lambda i: (i, 0)),
            pl.BlockSpec((tm, mem_w), lambda i: (i, qblk)),
            pl.BlockSpec((M, 2 * mem_w), lambda i: (i // per_b, 0)),
            pl.BlockSpec((mix_w + mem_w, D), lambda i: (0, 0)),
            pl.BlockSpec((tm, D), lambda i: (i, 0)),
            pl.BlockSpec((1, D), lambda i: (0, 0)),
        ],
        out_specs=pl.BlockSpec((tm, D), lambda i: (i, 0)),
        out_shape=jax.ShapeDtypeStruct((T, D), F32),
        scratch_shapes=[pltpu.VMEM((tm, mem_w), BF16)],
        compiler_params=_params("parallel"),
        name="mix_out_proj",
    )(tok, proj, kv, w_out, x, g)


def _mlp_kernel(x_ref, gpre_ref, wu_ref, wd_ref, gpost_ref, o_ref, h_ref, acc_ref):
    f = pl.program_id(1)
    nf = pl.num_programs(1)

    @pl.when(f == 0)
    def _():
        h_ref[...] = _rmsnorm(x_ref[...], gpre_ref[...]).astype(h_ref.dtype)
        acc_ref[...] = jnp.zeros_like(acc_ref)

    h = h_ref[...]
    tf = wu_ref.shape[1]
    cw = tf // MLP_CHUNKS
    d = None
    for c in range(MLP_CHUNKS):
        a = jnp.square(jnp.maximum(_dot(h, wu_ref[:, c * cw:(c + 1) * cw]), 0.0)).astype(BF16)
        dc = _dot(a, wd_ref[c * cw:(c + 1) * cw, :])
        d = dc if d is None else d + dc
    acc_ref[...] += d

    @pl.when(f == nf - 1)
    def _():
        o_ref[...] = x_ref[...] + _rmsnorm(acc_ref[...], gpost_ref[...])


def _mlp(x, gpre, w_up, w_down, gpost, tm, tf):
    T, D = x.shape
    F = w_up.shape[1]
    return pl.pallas_call(
        _mlp_kernel,
        grid=(T // tm, F // tf),
        in_specs=[
            pl.BlockSpec((tm, D), lambda i, f: (i, 0)),
            pl.BlockSpec((1, D), lambda i, f: (0, 0)),
            pl.BlockSpec((D, tf), lambda i, f: (0, f)),
            pl.BlockSpec((tf, D), lambda i, f: (f, 0)),
            pl.BlockSpec((1, D), lambda i, f: (0, 0)),
        ],
        out_specs=pl.BlockSpec((tm, D), lambda i, f: (i, 0)),
        out_shape=jax.ShapeDtypeStruct((T, D), F32),
        scratch_shapes=[pltpu.VMEM((tm, D), BF16), pltpu.VMEM((tm, D), F32)],
        compiler_params=_params("parallel", "arbitrary"),
        name="relu2_mlp",
    )(x, gpre, w_up, w_down, gpost)


def _trunk(x, mem, wts):
    (g_mix_pre, w_in, na_rpb, hy_short_w, hy_short_b, hy_w1, hy_b1, hy_w2, hy_b2, hy_w3, hy_freq,
     hy_decay, hy_d, g_mem, w_mem_kv, w_out, g_mix_post, g_mlp_pre, w_up, w_down, g_mlp_post) = wts
    B, L, D = x.shape
    depth = w_in.shape[0]
    in_w = w_in.shape[2]
    mem_w = w_mem_kv.shape[2] // 2
    mix_w = (in_w - mem_w) // 3
    x = x.reshape(B * L, D)
    mem = mem.reshape(-1, D)
    qk_scale = HEAD_DIM ** -0.5
    row = lambda v: v.astype(F32)[None]
    for i in range(depth):
        j = i // 2
        na_layer = i % 2 == 0
        colscale = jnp.concatenate([
            jnp.full((mix_w,), qk_scale if na_layer else 1.0, F32),
            jnp.ones((2 * mix_w,), F32),
            jnp.full((mem_w,), qk_scale, F32)])[None]
        proj = _norm_matmul(x, row(g_mix_pre[i]), w_in[i].astype(BF16), colscale, tm=min(1024, L), tn=1280)
        if na_layer:
            bias = _na_bias_table(na_rpb[j], L // GRID_W)
            tok = _neighbourhood_attention(proj, bias, B, L, mix_w)
        else:
            tok = _hyena(proj, B, L, mix_w, hy_short_w[j], hy_short_b[j], hy_w1[j], hy_b1[j], hy_w2[j],
                         hy_b2[j], hy_w3[j], hy_freq[j], hy_decay[j], hy_d[j])
        kv = _mem_kv(mem, row(g_mem[i]), w_mem_kv[i].astype(BF16))
        x = _mix_out(tok, proj, kv, w_out[i].astype(BF16), x, row(g_mix_post[i]), L, tm=min(512, L))
        x = _mlp(x, row(g_mlp_pre[i]), w_up[i].astype(BF16), w_down[i].astype(BF16), row(g_mlp_post[i]),
                 tm=min(512, L), tf=1024)
    return x.reshape(B, L, D)


def kernel(x_prompt, x_sample, mem_prompt, mem_sample, g_mix_pre, w_in, na_rpb, hy_short_w, hy_short_b,
           hy_w1, hy_b1, hy_w2, hy_b2, hy_w3, hy_freq, hy_decay, hy_d, g_mem, w_mem_kv, w_out,
           g_mix_post, g_mlp_pre, w_up, w_down, g_mlp_post):
    wts = (g_mix_pre, w_in, na_rpb, hy_short_w, hy_short_b, hy_w1, hy_b1, hy_w2, hy_b2, hy_w3, hy_freq,
           hy_decay, hy_d, g_mem, w_mem_kv, w_out, g_mix_post, g_mlp_pre, w_up, w_down, g_mlp_post)
    return (_trunk(x_prompt, mem_prompt, wts), _trunk(x_sample, mem_sample, wts))
```

```python
import functools
import math

import numpy as np
import jax
import jax.numpy as jnp
from jax import lax
from jax.experimental import pallas as pl
from jax.experimental.pallas import tpu as pltpu

F32 = jnp.float32
BF16 = jnp.bfloat16

GRID_W = 64
HEAD_DIM = 128
NA_MAX_KH = 8
NA_KW = 16
FILTER_EMB = 33
FILTER_BANDS = (FILTER_EMB - 1) // 2
RMS_EPS = 1e-6
NEG_BIG = -1e30
LOG2E = math.log2(math.e)

VMEM_LIMIT_BYTES = 56 * 1024 * 1024
NA_ROWS_PER_STEP = 4
NA_HEADS_PER_STEP = 6
NA_KEY_ROWS = NA_ROWS_PER_STEP + NA_MAX_KH
MLP_CHUNKS = 2
NORM_ROWS = 64
DFT_P = 32
DFT_MG = 8


def _params(*sem):
    return pltpu.CompilerParams(dimension_semantics=sem, vmem_limit_bytes=VMEM_LIMIT_BYTES)


def _rmsnorm(x, g):
    ms = jnp.mean(x * x, axis=-1, keepdims=True)
    return x * lax.rsqrt(ms + RMS_EPS) * g


def _dot(a, b):
    return jnp.dot(a, b, preferred_element_type=F32)


def _dot_nt(a, b):
    return lax.dot_general(a, b, (((1,), (1,)), ((), ())), preferred_element_type=F32)


def _norm_matmul_kernel(x_ref, g_ref, w_ref, s_ref, o_ref, h_ref):
    @pl.when(pl.program_id(1) == 0)
    def _():
        g = g_ref[...]
        for r in range(0, h_ref.shape[0], NORM_ROWS):
            rows = slice(r, r + NORM_ROWS)
            h_ref[rows, :] = _rmsnorm(x_ref[rows, :], g).astype(h_ref.dtype)

    o_ref[...] = (_dot(h_ref[...], w_ref[...]) * s_ref[...]).astype(o_ref.dtype)


def _norm_matmul(x, g, w, layer, colscale, tm, tn):
    T, D = x.shape
    N = w.shape[2]
    return pl.pallas_call(
        _norm_matmul_kernel,
        grid=(T // tm, N // tn),
        in_specs=[
            pl.BlockSpec((tm, D), lambda i, j: (i, 0)),
            pl.BlockSpec((1, D), lambda i, j: (0, 0)),
            pl.BlockSpec((None, D, tn), lambda i, j: (layer, 0, j)),
            pl.BlockSpec((1, tn), lambda i, j: (0, j)),
        ],
        out_specs=pl.BlockSpec((tm, tn), lambda i, j: (i, j)),
        out_shape=jax.ShapeDtypeStruct((T, N), BF16),
        scratch_shapes=[pltpu.VMEM((tm, D), BF16)],
        compiler_params=_params("parallel", "arbitrary"),
        name="norm_in_proj",
    )(x, g, w, colscale)


def _na_bias_table(rpb, rows):
    R, KR, W = NA_ROWS_PER_STEP, NA_KEY_ROWS, GRID_W
    kh = min(NA_MAX_KH, rows)
    nblk = rows // R
    qc = np.arange(W)
    cs = np.clip(qc - NA_KW // 2, 0, W - NA_KW)
    kc = np.arange(W)
    col_ok = (kc[None, :] >= cs[:, None]) & (kc[None, :] < cs[:, None] + NA_KW)
    dc = np.clip(kc[None, :] - qc[:, None] + NA_KW - 1, 0, 2 * NA_KW - 2)
    dr_all, ok_all = [], []
    for blk in (0, min(1, nblk - 1), nblk - 1):
        ks = int(np.clip(blk * R - kh // 2, 0, rows - KR))
        r = blk * R + np.arange(R)
        rs = np.clip(r - kh // 2, 0, rows - kh)
        key = ks + np.arange(KR)
        row_ok = (key[None, :] >= rs[:, None]) & (key[None, :] < rs[:, None] + kh)
        dr = np.clip(key[None, :] - r[:, None] + NA_MAX_KH - 1, 0, 2 * NA_MAX_KH - 2)
        dr_all.append(dr)
        ok_all.append(row_ok)
    dr = np.stack(dr_all)
    row_ok = np.stack(ok_all)
    per_dr = jnp.where(col_ok, rpb.astype(F32)[:, :, dc] * LOG2E, NEG_BIG)
    blk = jnp.where(row_ok[None, :, :, :, None, None], per_dr[:, dr], NEG_BIG)
    return blk.transpose(1, 0, 2, 4, 3, 5).reshape(3, rpb.shape[0], R * W, KR * W)


def _na_kernel(q_ref, k_ref, v_ref, b_ref, o_ref, *, rows):
    R, KR, W = NA_ROWS_PER_STEP, NA_KEY_ROWS, GRID_W
    i = pl.program_id(2)
    ks = jnp.clip(i * R - min(NA_MAX_KH, rows) // 2, 0, rows - KR) * W
    ks = pl.multiple_of(ks, W)
    heads = [slice(h * HEAD_DIM, (h + 1) * HEAD_DIM) for h in range(NA_HEADS_PER_STEP)]
    keys = pl.ds(ks, KR * W)
    scores = [_dot_nt(q_ref[:, c], k_ref[keys, c]) + b_ref[0, h] for h, c in enumerate(heads)]
    probs = []
    for s in scores:
        p = jnp.exp2(s - jnp.max(s, axis=-1, keepdims=True))
        probs.append((p.astype(BF16), jnp.sum(p, axis=-1, keepdims=True)))
    for c, (p, l) in zip(heads, probs):
        o_ref[:, c] = (_dot(p, v_ref[keys, c]) / l).astype(o_ref.dtype)


def _neighbourhood_attention(proj, bias, B, L, mix_w):
    R, KR, W = NA_ROWS_PER_STEP, NA_KEY_ROWS, GRID_W
    rows = L // W
    nblk = rows // R
    hw = NA_HEADS_PER_STEP * HEAD_DIM
    ngrp = mix_w // hw
    assert rows % R == 0 and rows >= KR and nblk >= 2 and R >= NA_MAX_KH // 2

    def variant(i):
        return jnp.where(i == 0, 0, jnp.where(i == nblk - 1, 2, 1))

    return pl.pallas_call(
        functools.partial(_na_kernel, rows=rows),
        grid=(B, ngrp, nblk),
        in_specs=[
            pl.BlockSpec((R * W, hw), lambda b, g, i: (b * nblk + i, g)),
            pl.BlockSpec((L, hw), lambda b, g, i: (b, ngrp + g)),
            pl.BlockSpec((L, hw), lambda b, g, i: (b, 2 * ngrp + g)),
            pl.BlockSpec((1, NA_HEADS_PER_STEP, R * W, KR * W), lambda b, g, i: (variant(i), g, 0, 0)),
        ],
        out_specs=pl.BlockSpec((R * W, hw), lambda b, g, i: (b * nblk + i, g)),
        out_shape=jax.ShapeDtypeStruct((B * L, mix_w), BF16),
        compiler_params=_params("parallel", "parallel", "arbitrary"),
        name="neighbourhood_attention",
    )(proj, proj, proj, bias)


def _position_features(L):
    P = DFT_P
    pos = (jnp.arange(L // P, dtype=F32)[None, :] * P + jnp.arange(P, dtype=F32)[:, None]).reshape(L)
    t = pos / (L - 1)
    wpos = 2.0 * math.pi * pos / L
    fb = jnp.linspace(1e-4, FILTER_BANDS - 1, FILTER_BANDS, dtype=F32)
    ang = wpos[:, None] * fb[None, :]
    z = jnp.concatenate([t[:, None], jnp.cos(ang), -jnp.sin(ang)], axis=-1)
    return jnp.pad(z, ((0, 0), (0, 128 - FILTER_EMB)))


def _hy_filter_kernel(z_ref, w1_ref, b1_ref, w2_ref, b2_ref, w3_ref, fr_ref, dec_ref, h_ref, *, mix_w):
    hp = lax.Precision.HIGHEST
    fr = fr_ref[...]
    z = z_ref[...]
    h = jnp.sin(fr * (jnp.dot(z, w1_ref[...], precision=hp, preferred_element_type=F32) + b1_ref[...]))
    h = jnp.sin(fr * (jnp.dot(h, w2_ref[...], precision=hp, preferred_element_type=F32) + b2_ref[...]))

    def taps(d):
        w3 = w3_ref[:, d * mix_w:(d + 1) * mix_w]
        win = jnp.exp(-z[:, 0:1] * jnp.abs(dec_ref[d:d + 1, :]))
        return jnp.dot(h, w3, precision=hp, preferred_element_type=F32) * win

    tl = z_ref.shape[0]
    lag0 = pl.program_id(0) * tl + lax.broadcasted_iota(jnp.int32, (tl, 1), 0) == 0
    h_fwd, h_bwd = taps(0), taps(1)
    h_ref[0] = h_fwd + jnp.where(lag0, h_bwd, 0.0)
    h_ref[1] = jnp.where(lag0, 0.0, h_bwd)


def _hyena_filter_taps(L, w1, b1, w2, b2, w3, freq, decay, tl):
    mix_w = decay.shape[1]
    order = w2.shape[0]
    w1p = jnp.pad(w1.astype(F32), ((0, 128 - FILTER_EMB), (0, 0)))
    full = lambda shape: pl.BlockSpec(shape, lambda i: (0, 0))
    return pl.pallas_call(
        functools.partial(_hy_filter_kernel, mix_w=mix_w),
        grid=(L // tl,),
        in_specs=[
            pl.BlockSpec((tl, 128), lambda i: (i, 0)),
            full((128, order)), full((1, order)), full((order, order)), full((1, order)),
            full((order, 2 * mix_w)), full((1, order)), full((2, mix_w)),
        ],
        out_specs=pl.BlockSpec((2, tl, mix_w), lambda i: (0, i, 0)),
        out_shape=jax.ShapeDtypeStruct((2, L, mix_w), F32),
        compiler_params=_params("parallel"),
        name="hyena_filter_taps",
    )(_position_features(L), w1p, b1.astype(F32)[None], w2.astype(F32), b2.astype(F32)[None],
      w3.astype(F32), freq.astype(F32)[None], decay.astype(F32))


def _dft_tables(L):
    P, MG = DFT_P, DFT_MG
    A, n = L // P, 2 * L
    Mm, J = 2 * A, P // 2
    NG = Mm // MG
    ar = lambda size: jnp.arange(size, dtype=jnp.int32)
    ang1 = (((2 * ar(Mm) + 1)[:, None] * ar(A)[None, :]) % (2 * Mm)).astype(F32) * (math.pi / Mm)
    w1 = jnp.concatenate([jnp.cos(ang1), -jnp.sin(ang1)], axis=0)
    k = (Mm * ar(J)[None, None, :, None] + MG * ar(NG)[:, None, None, None] + ar(MG)[None, :, None, None])
    ph = ((ar(P)[None, None, None, :] * (2 * k + 1)) % (2 * n)).astype(F32) * (math.pi / n)
    cr, ci = jnp.cos(ph), -jnp.sin(ph)
    eye = jnp.eye(MG, dtype=F32)
    blk = lambda c: c[..., None] * eye[None, :, None, None, :]
    top = jnp.stack([blk(cr), blk(-ci)], axis=3)
    bot = jnp.stack([blk(ci), blk(cr)], axis=3)
    a3 = jnp.stack([top, bot], axis=1).reshape(NG, 2 * MG * J, 2 * P * MG)
    return w1.astype(BF16), w1.T.astype(BF16), a3.astype(BF16), a3.transpose(0, 2, 1).astype(BF16)


def _short_conv(ref, w_ref, b_ref, start, size, L):
    halo, tile = 16, 8
    cur = ref[pl.ds(start, size), :].astype(F32)
    up, dn = pltpu.roll(cur, 1, axis=0), pltpu.roll(cur, size - 1, axis=0)
    prev_start = pl.multiple_of(jnp.maximum(start - halo, 0), halo)
    next_start = pl.multiple_of(jnp.minimum(start + size, L - halo), halo)
    prev = ref[pl.ds(prev_start, halo), :][halo - 1:halo, :].astype(F32)
    nxt = ref[pl.ds(next_start, halo), :][0:1, :].astype(F32)
    prev = jnp.where(start == 0, 0.0, prev)
    nxt = jnp.where(start + size == L, 0.0, nxt)
    row = lax.broadcasted_iota(jnp.int32, (tile, 1), 0)
    up = jnp.concatenate([jnp.where(row == 0, prev, up[:tile]), up[tile:]], axis=0)
    dn = jnp.concatenate([dn[:size - tile], jnp.where(row == tile - 1, nxt, dn[size - tile:])], axis=0)
    return up * w_ref[0:1, :] + cur * w_ref[1:2, :] + dn * w_ref[2:3, :] + b_ref[...]


def _hy_gate_kernel(x0_ref, x1_ref, v_ref, w0_ref, w1_ref, wv_ref, b0_ref, b1_ref, bv_ref,
                    perm_ref, x0c_ref, zb_ref, zp_ref):
    P, MG = DFT_P, DFT_MG
    chunk = P * MG
    L, cb = x0_ref.shape

    def body(c, carry):
        start = pl.multiple_of(c * chunk, chunk)
        rows = pl.ds(start, chunk)
        conv = functools.partial(_short_conv, start=start, size=chunk, L=L)
        x0c_ref[rows, :] = conv(x0_ref, w0_ref, b0_ref).astype(x0c_ref.dtype)
        x1c = conv(x1_ref, w1_ref, b1_ref)
        vc = conv(v_ref, wv_ref, bv_ref)
        z = (x1c * vc).astype(BF16)
        zb_ref[rows, :] = z
        zp = _dot(perm_ref[...], z).reshape(P, MG, cb)
        zp_ref[:, pl.ds(pl.multiple_of(c * MG, MG), MG), :] = zp
        return carry

    lax.fori_loop(0, L // chunk, body, 0, unroll=2)


def _time_split_permutation():
    P, MG = DFT_P, DFT_MG
    r = np.arange(P * MG)
    perm = np.zeros((P * MG, P * MG), np.float32)
    perm[r, (r % MG) * P + r // MG] = 1.0
    return perm


def _hyena_gate(proj, short_w, short_b, B, L, mix_w, cb):
    nc = mix_w // cb
    P = DFT_P
    col = lambda part: pl.BlockSpec((L, cb), lambda b, c: (b, part * nc + c))
    wspec = lambda part: pl.BlockSpec((3, cb), lambda b, c: (0, part * nc + c))
    bspec = lambda part: pl.BlockSpec((1, cb), lambda b, c: (0, part * nc + c))
    perm = jnp.asarray(_time_split_permutation(), BF16)
    seq = pl.BlockSpec((L, cb), lambda b, c: (b, c))
    return pl.pallas_call(
        _hy_gate_kernel,
        grid=(B, nc),
        in_specs=[col(0), col(1), col(2), wspec(0), wspec(1), wspec(2), bspec(0), bspec(1), bspec(2),
                  pl.BlockSpec(perm.shape, lambda b, c: (0, 0))],
        out_specs=[seq, seq, pl.BlockSpec((None, P, L // P, cb), lambda b, c: (b, 0, 0, c))],
        out_shape=[jax.ShapeDtypeStruct((B * L, mix_w), BF16), jax.ShapeDtypeStruct((B * L, mix_w), BF16),
                   jax.ShapeDtypeStruct((B, P, L // P, mix_w), F32)],
        compiler_params=_params("parallel", "parallel"),
        name="hyena_gate",
    )(proj, proj, proj, short_w, short_w, short_w, short_b, short_b, short_b, perm)


def _dft_fwd_kernel(*refs, filtered):
    if filtered:
        z_ref, w1_ref, a3_ref, g_ref, y_ref, v_ref = refs
    else:
        z_ref, w1_ref, a3_ref, y_ref, v_ref = refs
    P, MG = DFT_P, DFT_MG
    cb = z_ref.shape[2]
    ng, rows_g = a3_ref.shape[0], a3_ref.shape[1]
    half = rows_g // 2
    w1 = w1_ref[...]
    for b in range(P):
        v = _dot(w1, z_ref[b].astype(BF16))
        v_ref[:, :, b * MG:(b + 1) * MG, :] = v.reshape(2, ng, MG, cb)

    def group(g, carry):
        vg = jnp.concatenate([v_ref[0, g], v_ref[1, g]], axis=0).astype(BF16)
        x = _dot(a3_ref[g], vg)
        if filtered:
            xr, xi = x[:half], x[half:]
            gr, gi = g_ref[g, :half], g_ref[g, half:]
            x = jnp.concatenate([xr * gr - xi * gi, xr * gi + xi * gr], axis=0)
        y_ref[pl.ds(pl.multiple_of(g * rows_g, rows_g), rows_g), :] = x.astype(y_ref.dtype)
        return carry

    lax.fori_loop(0, ng, group, 0, unroll=8)


def _dft_forward(z, w1, a3, g, Bz, L, cb, out_dtype):
    C = z.shape[3]
    ng, rows_g, cols_g = a3.shape
    const = lambda shape: pl.BlockSpec(shape, lambda c, b: (0,) * len(shape), pipeline_mode=pl.Buffered(1))
    in_specs = [pl.BlockSpec((None, DFT_P, L // DFT_P, cb), lambda c, b: (b, 0, 0, c)),
                const(w1.shape), const(a3.shape)]
    args = [z, w1, a3]
    if g is not None:
        in_specs.append(pl.BlockSpec((ng, rows_g, cb), lambda c, b: (0, 0, c), pipeline_mode=pl.Buffered(1)))
        args.append(g)
    return pl.pallas_call(
        functools.partial(_dft_fwd_kernel, filtered=g is not None),
        grid=(C // cb, Bz),
        in_specs=in_specs,
        out_specs=pl.BlockSpec((None, 2 * L, cb), lambda c, b: (b, 0, c)),
        out_shape=jax.ShapeDtypeStruct((Bz, 2 * L, C), out_dtype),
        scratch_shapes=[pltpu.VMEM((2, ng, cols_g // 2, cb), F32)],
        compiler_params=_params("parallel", "arbitrary"),
        name="hyena_dft_forward",
    )(*args)


def _dft_inv_kernel(y_ref, b3_ref, w1t_ref, zb_ref, x0_ref, d_ref, unperm_ref, o_ref, q_ref, t_ref):
    P, MG = DFT_P, DFT_MG
    chunk = P * MG
    cb = y_ref.shape[1]
    ng, rows_g, cols_g = b3_ref.shape
    half = rows_g // 2

    def group(g, carry):
        yg = y_ref[pl.ds(pl.multiple_of(g * cols_g, cols_g), cols_g), :]
        q = _dot(b3_ref[g], yg)
        q_ref[0, g] = q[:half]
        q_ref[1, g] = q[half:]
        return carry

    lax.fori_loop(0, ng, group, 0, unroll=8)
    w1t = w1t_ref[...]
    for b in range(P):
        qb = q_ref[:, :, b * MG:(b + 1) * MG, :].reshape(2 * ng * MG, cb)
        t_ref[b] = _dot(w1t, qb.astype(BF16))

    unperm = unperm_ref[...]

    def gate(c, carry):
        rows = pl.ds(pl.multiple_of(c * chunk, chunk), chunk)
        yp = t_ref[:, pl.ds(pl.multiple_of(c * MG, MG), MG), :].reshape(chunk, cb)
        hi = yp.astype(BF16)
        lo = (yp - hi.astype(F32)).astype(BF16)
        y = _dot(unperm, hi) + _dot(unperm, lo)
        y = y + d_ref[...] * zb_ref[rows, :].astype(F32)
        o_ref[rows, :] = (x0_ref[rows, :].astype(F32) * y).astype(o_ref.dtype)
        return carry

    lax.fori_loop(0, zb_ref.shape[0] // chunk, gate, 0, unroll=4)


def _dft_inverse(y, b3, w1t, zb, x0c, d, B, L, cb):
    C = zb.shape[1]
    ng, rows_g, _ = b3.shape
    const = lambda shape: pl.BlockSpec(shape, lambda c, b: (0,) * len(shape), pipeline_mode=pl.Buffered(1))
    seq = pl.BlockSpec((L, cb), lambda c, b: (b, c))
    unperm = jnp.asarray(_time_split_permutation().T, BF16)
    return pl.pallas_call(
        _dft_inv_kernel,
        grid=(C // cb, B),
        in_specs=[pl.BlockSpec((None, 2 * L, cb), lambda c, b: (b, 0, c)), const(b3.shape), const(w1t.shape),
                  seq, seq, pl.BlockSpec((1, cb), lambda c, b: (0, c)), const(unperm.shape)],
        out_specs=seq,
        out_shape=jax.ShapeDtypeStruct((B * L, C), BF16),
        scratch_shapes=[pltpu.VMEM((2, ng, rows_g // 2, cb), F32),
                        pltpu.VMEM((DFT_P, L // DFT_P, cb), F32)],
        compiler_params=_params("parallel", "arbitrary"),
        name="hyena_dft_inverse",
    )(y, b3, w1t, zb, x0c, d, unperm)


def _hyena(proj, B, L, mix_w, short_w, short_b, w1, b1, w2, b2, w3, freq, decay, d_bias):
    cb = 256
    dw1, dw1t, a3, b3 = _dft_tables(L)
    ng, rows_g, _ = a3.shape
    half = rows_g // 2
    taps = _hyena_filter_taps(L, w1, b1, w2, b2, w3, freq, decay, tl=min(512, L))
    spec = _dft_forward(taps.reshape(2, DFT_P, L // DFT_P, mix_w), dw1, a3, None, 2, L, cb, F32)
    spec = spec.reshape(2, ng, rows_g, mix_w)
    g = (spec[0] + jnp.concatenate([spec[1, :, :half], -spec[1, :, half:]], axis=1)) * (1.0 / L)
    x0c, zb, zp = _hyena_gate(proj, short_w.astype(F32), short_b.astype(F32)[None], B, L, mix_w, cb)
    y = _dft_forward(zp, dw1, a3, g, B, L, cb, BF16)
    return _dft_inverse(y, b3, dw1t, zb, x0c, d_bias.astype(F32)[None], B, L, cb)


def _mem_kv_kernel(m_ref, g_ref, w_ref, o_ref):
    h = _rmsnorm(m_ref[...], g_ref[...]).astype(BF16)
    o_ref[...] = _dot(h, w_ref[...]).astype(o_ref.dtype)


def _mem_kv(mem, g, w, layer):
    BM, D = mem.shape
    N = w.shape[2]
    M = 256
    return pl.pallas_call(
        _mem_kv_kernel,
        grid=(BM // M,),
        in_specs=[
            pl.BlockSpec((M, D), lambda b: (b, 0)),
            pl.BlockSpec((1, D), lambda b: (0, 0)),
            pl.BlockSpec((None, D, N), lambda b: (layer, 0, 0)),
        ],
        out_specs=pl.BlockSpec((M, N), lambda b: (b, 0)),
        out_shape=jax.ShapeDtypeStruct((BM, N), BF16),
        compiler_params=_params("parallel"),
        name="memory_kv",
    )(mem, g, w)


def _mix_out_kernel(tok_ref, q_ref, kv_ref, w_ref, x_ref, g_ref, o_ref, mo_ref):
    mem_w = q_ref.shape[1]
    mix_w = tok_ref.shape[1]
    heads = [slice(h * HEAD_DIM, (h + 1) * HEAD_DIM) for h in range(mem_w // HEAD_DIM)]
    scores = [_dot_nt(q_ref[:, c], kv_ref[:, c]) for c in heads]
    o = _dot(tok_ref[...], w_ref[:mix_w, :])
    probs = []
    for s in scores:
        p = jnp.exp2(s - jnp.max(s, axis=-1, keepdims=True))
        probs.append((p.astype(BF16), jnp.sum(p, axis=-1, keepdims=True)))
    for c, (p, l) in zip(heads, probs):
        vcols = slice(mem_w + c.start, mem_w + c.stop)
        mo_ref[:, c] = (_dot(p, kv_ref[:, vcols]) / l).astype(mo_ref.dtype)
    o = o + _dot(mo_ref[...], w_ref[mix_w:, :])
    o_ref[...] = x_ref[...] + _rmsnorm(o, g_ref[...])


def _mix_out(tok, proj, kv, w_out, layer, x, g, L, tm):
    T, D = x.shape
    mix_w = tok.shape[1]
    mem_w = w_out.shape[1] - mix_w
    M = kv.shape[0] // (T // L)
    qblk = (proj.shape[1] - mem_w) // mem_w
    per_b = L // tm
    return pl.pallas_call(
        _mix_out_kernel,
        grid=(T // tm,),
        in_specs=[
            pl.BlockSpec((tm, mix_w), lambda i: (i, 0)),
            pl.BlockSpec((tm, mem_w), lambda i: (i, qblk)),
            pl.BlockSpec((M, 2 * mem_w), lambda i: (i // per_b, 0)),
            pl.BlockSpec((None, mix_w + mem_w, D), lambda i: (layer, 0, 0)),
            pl.BlockSpec((tm, D), lambda i: (i, 0)),
            pl.BlockSpec((1, D), lambda i: (0, 0)),
        ],
        out_specs=pl.BlockSpec((tm, D), lambda i: (i, 0)),
        out_shape=jax.ShapeDtypeStruct((T, D), F32),
        scratch_shapes=[pltpu.VMEM((tm, mem_w), BF16)],
        compiler_params=_params("parallel"),
        name="mix_out_proj",
    )(tok, proj, kv, w_out, x, g)


def _mlp_kernel(x_ref, gpre_ref, wu_ref, wd_ref, gpost_ref, o_ref, h_ref, acc_ref):
    f = pl.program_id(1)
    nf = pl.num_programs(1)

    @pl.when(f == 0)
    def _():
        g = gpre_ref[...]
        for r in range(0, h_ref.shape[0], NORM_ROWS):
            rows = slice(r, r + NORM_ROWS)
            h_ref[rows, :] = _rmsnorm(x_ref[rows, :], g).astype(h_ref.dtype)
        acc_ref[...] = jnp.zeros_like(acc_ref)

    h = h_ref[...]
    tf = wu_ref.shape[1]
    cw = tf // MLP_CHUNKS
    chunks = [slice(c * cw, (c + 1) * cw) for c in range(MLP_CHUNKS)]
    ups = [_dot(h, wu_ref[:, c]) for c in chunks]
    acts = [jnp.square(jnp.maximum(u, 0.0)).astype(BF16) for u in ups]
    d = acc_ref[...]
    for a, c in zip(acts, chunks):
        d = d + _dot(a, wd_ref[c, :])
    acc_ref[...] = d

    @pl.when(f == nf - 1)
    def _():
        g = gpost_ref[...]
        for r in range(0, o_ref.shape[0], NORM_ROWS):
            rows = slice(r, r + NORM_ROWS)
            o_ref[rows, :] = x_ref[rows, :] + _rmsnorm(acc_ref[rows, :], g)


def _mlp(x, gpre, w_up, w_down, layer, gpost, tm, tf):
    T, D = x.shape
    F = w_up.shape[2]
    return pl.pallas_call(
        _mlp_kernel,
        grid=(T // tm, F // tf),
        in_specs=[
            pl.BlockSpec((tm, D), lambda i, f: (i, 0)),
            pl.BlockSpec((1, D), lambda i, f: (0, 0)),
            pl.BlockSpec((None, D, tf), lambda i, f: (layer, 0, f)),
            pl.BlockSpec((None, tf, D), lambda i, f: (layer, f, 0)),
            pl.BlockSpec((1, D), lambda i, f: (0, 0)),
        ],
        out_specs=pl.BlockSpec((tm, D), lambda i, f: (i, 0)),
        out_shape=jax.ShapeDtypeStruct((T, D), F32),
        scratch_shapes=[pltpu.VMEM((tm, D), BF16), pltpu.VMEM((tm, D), F32)],
        compiler_params=_params("parallel", "arbitrary"),
        name="relu2_mlp",
    )(x, gpre, w_up, w_down, gpost)


def _trunk(x, mem, wts):
    (g_mix_pre, w_in, na_rpb, hy_short_w, hy_short_b, hy_w1, hy_b1, hy_w2, hy_b2, hy_w3, hy_freq,
     hy_decay, hy_d, g_mem, w_mem_kv, w_out, g_mix_post, g_mlp_pre, w_up, w_down, g_mlp_post) = wts
    B, L, D = x.shape
    depth = w_in.shape[0]
    in_w = w_in.shape[2]
    mem_w = w_mem_kv.shape[2] // 2
    mix_w = (in_w - mem_w) // 3
    x = x.reshape(B * L, D)
    mem = mem.reshape(-1, D)
    qk_scale = HEAD_DIM ** -0.5 * LOG2E
    row = lambda v: v.astype(F32)[None]
    for i in range(depth):
        j = i // 2
        na_layer = i % 2 == 0
        colscale = jnp.concatenate([
            jnp.full((mix_w,), qk_scale if na_layer else 1.0, F32),
            jnp.ones((2 * mix_w,), F32),
            jnp.full((mem_w,), qk_scale, F32)])[None]
        proj = _norm_matmul(x, row(g_mix_pre[i]), w_in, i, colscale, tm=min(1024, L), tn=1280)
        if na_layer:
            bias = _na_bias_table(na_rpb[j], L // GRID_W)
            tok = _neighbourhood_attention(proj, bias, B, L, mix_w)
        else:
            tok = _hyena(proj, B, L, mix_w, hy_short_w[j], hy_short_b[j], hy_w1[j], hy_b1[j], hy_w2[j],
                         hy_b2[j], hy_w3[j], hy_freq[j], hy_decay[j], hy_d[j])
        kv = _mem_kv(mem, row(g_mem[i]), w_mem_kv, i)
        x = _mix_out(tok, proj, kv, w_out, i, x, row(g_mix_post[i]), L, tm=min(512, L))
        x = _mlp(x, row(g_mlp_pre[i]), w_up, w_down, i, row(g_mlp_post[i]), tm=min(512, L), tf=1024)
    return x.reshape(B, L, D)


def kernel(x_prompt, x_sample, mem_prompt, mem_sample, g_mix_pre, w_in, na_rpb, hy_short_w, hy_short_b,
           hy_w1, hy_b1, hy_w2, hy_b2, hy_w3, hy_freq, hy_decay, hy_d, g_mem, w_mem_kv, w_out,
           g_mix_post, g_mlp_pre, w_up, w_down, g_mlp_post):
    bf = lambda w: w.astype(BF16)
    wts = (g_mix_pre, bf(w_in), na_rpb, hy_short_w, hy_short_b, hy_w1, hy_b1, hy_w2, hy_b2, hy_w3, hy_freq,
           hy_decay, hy_d, g_mem, bf(w_mem_kv), bf(w_out), g_mix_post, g_mlp_pre, bf(w_up), bf(w_down), g_mlp_post)
    return (_trunk(x_prompt, mem_prompt, wts), _trunk(x_sample, mem_sample, wts))
```

```python
import functools
import math

import numpy as np
import jax
import jax.numpy as jnp
from jax import lax
from jax.experimental import pallas as pl
from jax.experimental.pallas import tpu as pltpu

F32 = jnp.float32
BF16 = jnp.bfloat16

GRID_W = 64
HEAD_DIM = 128
NA_MAX_KH = 8
NA_KW = 16
FILTER_EMB = 33
FILTER_BANDS = (FILTER_EMB - 1) // 2
RMS_EPS = 1e-6
NEG_BIG = -1e30
LOG2E = math.log2(math.e)

VMEM_LIMIT_BYTES = 56 * 1024 * 1024
NA_ROWS_PER_STEP = 4
NA_HEADS_PER_STEP = 6
NA_KEY_ROWS = NA_ROWS_PER_STEP + NA_MAX_KH
MLP_CHUNKS = 2
NORM_ROWS = 64
DFT_P = 32
DFT_MG = 8
HYENA_LANE_ELEMS = 1 << 20


def _params(*sem):
    return pltpu.CompilerParams(dimension_semantics=sem, vmem_limit_bytes=VMEM_LIMIT_BYTES)


def _rmsnorm(x, g):
    ms = jnp.mean(x * x, axis=-1, keepdims=True)
    return x * lax.rsqrt(ms + RMS_EPS) * g


def _dot(a, b):
    return jnp.dot(a, b, preferred_element_type=F32)


def _dot_nt(a, b):
    return lax.dot_general(a, b, (((1,), (1,)), ((), ())), preferred_element_type=F32)


def _norm_matmul_kernel(x_ref, g_ref, w_ref, s_ref, o_ref, h_ref):
    @pl.when(pl.program_id(1) == 0)
    def _():
        g = g_ref[...]
        for r in range(0, h_ref.shape[0], NORM_ROWS):
            rows = slice(r, r + NORM_ROWS)
            h_ref[rows, :] = _rmsnorm(x_ref[rows, :], g).astype(h_ref.dtype)

    o_ref[...] = (_dot(h_ref[...], w_ref[...]) * s_ref[...]).astype(o_ref.dtype)


def _norm_matmul(x, g, w, layer, colscale, tm, tn):
    T, D = x.shape
    N = w.shape[2]
    return pl.pallas_call(
        _norm_matmul_kernel,
        grid=(T // tm, N // tn),
        in_specs=[
            pl.BlockSpec((tm, D), lambda i, j: (i, 0)),
            pl.BlockSpec((1, D), lambda i, j: (0, 0)),
            pl.BlockSpec((None, D, tn), lambda i, j: (layer, 0, j)),
            pl.BlockSpec((1, tn), lambda i, j: (0, j)),
        ],
        out_specs=pl.BlockSpec((tm, tn), lambda i, j: (i, j)),
        out_shape=jax.ShapeDtypeStruct((T, N), BF16),
        scratch_shapes=[pltpu.VMEM((tm, D), BF16)],
        compiler_params=_params("parallel", "arbitrary"),
        name="norm_in_proj",
    )(x, g, w, colscale)


def _na_bias_table(rpb, rows):
    R, KR, W = NA_ROWS_PER_STEP, NA_KEY_ROWS, GRID_W
    kh = min(NA_MAX_KH, rows)
    nblk = rows // R
    qc = np.arange(W)
    cs = np.clip(qc - NA_KW // 2, 0, W - NA_KW)
    kc = np.arange(W)
    col_ok = (kc[None, :] >= cs[:, None]) & (kc[None, :] < cs[:, None] + NA_KW)
    dr_all, ok_all = [], []
    for blk in (0, min(1, nblk - 1), nblk - 1):
        ks = int(np.clip(blk * R - kh // 2, 0, rows - KR))
        r = blk * R + np.arange(R)
        rs = np.clip(r - kh // 2, 0, rows - kh)
        key = ks + np.arange(KR)
        row_ok = (key[None, :] >= rs[:, None]) & (key[None, :] < rs[:, None] + kh)
        dr = np.clip(key[None, :] - r[:, None] + NA_MAX_KH - 1, 0, 2 * NA_MAX_KH - 2)
        dr_all.append(dr)
        ok_all.append(row_ok)
    pad = W - NA_KW
    padded = jnp.pad(rpb.astype(F32) * LOG2E, ((0, 0), (0, 0), (pad, pad)))
    per_dr = jnp.stack([padded[:, :, W - 1 - q:2 * W - 1 - q] for q in range(W)], axis=2)
    per_dr = jnp.where(col_ok, per_dr, NEG_BIG)
    masked = jnp.full(per_dr.shape[:1] + per_dr.shape[2:], NEG_BIG, F32)
    variants = []
    for dr, row_ok in zip(dr_all, ok_all):
        slabs = [jnp.concatenate([per_dr[:, dr[r, k]] if row_ok[r, k] else masked for k in range(KR)], axis=-1)
                 for r in range(R)]
        variants.append(jnp.concatenate(slabs, axis=1))
    return jnp.stack(variants)


def _na_kernel(q_ref, k_ref, v_ref, b_ref, o_ref, *, rows):
    R, KR, W = NA_ROWS_PER_STEP, NA_KEY_ROWS, GRID_W
    i = pl.program_id(2)
    ks = jnp.clip(i * R - min(NA_MAX_KH, rows) // 2, 0, rows - KR) * W
    ks = pl.multiple_of(ks, W)
    heads = [slice(h * HEAD_DIM, (h + 1) * HEAD_DIM) for h in range(NA_HEADS_PER_STEP)]
    keys = pl.ds(ks, KR * W)
    scores = [_dot_nt(q_ref[:, c], k_ref[keys, c]) + b_ref[0, h] for h, c in enumerate(heads)]
    probs = []
    for s in scores:
        p = jnp.exp2(s - jnp.max(s, axis=-1, keepdims=True))
        probs.append((p.astype(BF16), jnp.sum(p, axis=-1, keepdims=True)))
    for c, (p, l) in zip(heads, probs):
        o_ref[:, c] = (_dot(p, v_ref[keys, c]) / l).astype(o_ref.dtype)


def _neighbourhood_attention(proj, bias, B, L, mix_w):
    R, KR, W = NA_ROWS_PER_STEP, NA_KEY_ROWS, GRID_W
    rows = L // W
    nblk = rows // R
    hw = NA_HEADS_PER_STEP * HEAD_DIM
    ngrp = mix_w // hw
    assert rows % R == 0 and rows >= KR and nblk >= 2 and R >= NA_MAX_KH // 2

    def variant(i):
        return jnp.where(i == 0, 0, jnp.where(i == nblk - 1, 2, 1))

    return pl.pallas_call(
        functools.partial(_na_kernel, rows=rows),
        grid=(B, ngrp, nblk),
        in_specs=[
            pl.BlockSpec((R * W, hw), lambda b, g, i: (b * nblk + i, g)),
            pl.BlockSpec((L, hw), lambda b, g, i: (b, ngrp + g)),
            pl.BlockSpec((L, hw), lambda b, g, i: (b, 2 * ngrp + g)),
            pl.BlockSpec((1, NA_HEADS_PER_STEP, R * W, KR * W), lambda b, g, i: (variant(i), g, 0, 0)),
        ],
        out_specs=pl.BlockSpec((R * W, hw), lambda b, g, i: (b * nblk + i, g)),
        out_shape=jax.ShapeDtypeStruct((B * L, mix_w), BF16),
        compiler_params=_params("parallel", "parallel", "arbitrary"),
        name="neighbourhood_attention",
    )(proj, proj, proj, bias)


def _position_features(L):
    P = DFT_P
    pos = (jnp.arange(L // P, dtype=F32)[None, :] * P + jnp.arange(P, dtype=F32)[:, None]).reshape(L)
    t = pos / (L - 1)
    wpos = 2.0 * math.pi * pos / L
    fb = jnp.linspace(1e-4, FILTER_BANDS - 1, FILTER_BANDS, dtype=F32)
    ang = wpos[:, None] * fb[None, :]
    z = jnp.concatenate([t[:, None], jnp.cos(ang), -jnp.sin(ang)], axis=-1)
    return jnp.pad(z, ((0, 0), (0, 128 - FILTER_EMB)))


def _hy_filter_kernel(z_ref, w1_ref, b1_ref, w2_ref, b2_ref, w3_ref, fr_ref, dec_ref, h_ref, *, mix_w):
    hp = lax.Precision.HIGHEST
    fr = fr_ref[...]
    z = z_ref[...]
    h = jnp.sin(fr * (jnp.dot(z, w1_ref[...], precision=hp, preferred_element_type=F32) + b1_ref[...]))
    h = jnp.sin(fr * (jnp.dot(h, w2_ref[...], precision=hp, preferred_element_type=F32) + b2_ref[...]))

    def taps(d):
        w3 = w3_ref[:, d * mix_w:(d + 1) * mix_w]
        win = jnp.exp(-z[:, 0:1] * jnp.abs(dec_ref[d:d + 1, :]))
        return jnp.dot(h, w3, precision=hp, preferred_element_type=F32) * win

    tl = z_ref.shape[0]
    lag0 = pl.program_id(0) * tl + lax.broadcasted_iota(jnp.int32, (tl, 1), 0) == 0
    h_fwd, h_bwd = taps(0), taps(1)
    h_ref[0] = h_fwd + jnp.where(lag0, h_bwd, 0.0)
    h_ref[1] = jnp.where(lag0, 0.0, h_bwd)


def _hyena_filter_taps(L, w1, b1, w2, b2, w3, freq, decay, tl):
    mix_w = decay.shape[1]
    order = w2.shape[0]
    w1p = jnp.pad(w1.astype(F32), ((0, 128 - FILTER_EMB), (0, 0)))
    full = lambda shape: pl.BlockSpec(shape, lambda i: (0, 0))
    return pl.pallas_call(
        functools.partial(_hy_filter_kernel, mix_w=mix_w),
        grid=(L // tl,),
        in_specs=[
            pl.BlockSpec((tl, 128), lambda i: (i, 0)),
            full((128, order)), full((1, order)), full((order, order)), full((1, order)),
            full((order, 2 * mix_w)), full((1, order)), full((2, mix_w)),
        ],
        out_specs=pl.BlockSpec((2, tl, mix_w), lambda i: (0, i, 0)),
        out_shape=jax.ShapeDtypeStruct((2, L, mix_w), F32),
        compiler_params=_params("parallel"),
        name="hyena_filter_taps",
    )(_position_features(L), w1p, b1.astype(F32)[None], w2.astype(F32), b2.astype(F32)[None],
      w3.astype(F32), freq.astype(F32)[None], decay.astype(F32))


def _dft_tables(L):
    P, MG = DFT_P, DFT_MG
    A, n = L // P, 2 * L
    Mm, J = 2 * A, P // 2
    NG = Mm // MG
    ar = lambda size: jnp.arange(size, dtype=jnp.int32)
    ang1 = (((2 * ar(Mm) + 1)[:, None] * ar(A)[None, :]) % (2 * Mm)).astype(F32) * (math.pi / Mm)
    w1 = jnp.concatenate([jnp.cos(ang1), -jnp.sin(ang1)], axis=0)
    k = (Mm * ar(J)[None, None, :, None] + MG * ar(NG)[:, None, None, None] + ar(MG)[None, :, None, None])
    ph = ((ar(P)[None, None, None, :] * (2 * k + 1)) % (2 * n)).astype(F32) * (math.pi / n)
    cr, ci = jnp.cos(ph), -jnp.sin(ph)
    eye = jnp.eye(MG, dtype=F32)
    blk = lambda c: c[..., None] * eye[None, :, None, None, :]
    top = jnp.stack([blk(cr), blk(-ci)], axis=3)
    bot = jnp.stack([blk(ci), blk(cr)], axis=3)
    a3 = jnp.stack([top, bot], axis=1).reshape(NG, 2 * MG * J, 2 * P * MG)
    return w1.astype(BF16), w1.T.astype(BF16), a3.astype(BF16), a3.transpose(0, 2, 1).astype(BF16)


def _short_conv(ref, w_ref, b_ref, start, size, L):
    halo, tile = 16, 8
    cur = ref[pl.ds(start, size), :].astype(F32)
    up, dn = pltpu.roll(cur, 1, axis=0), pltpu.roll(cur, size - 1, axis=0)
    prev_start = pl.multiple_of(jnp.maximum(start - halo, 0), halo)
    next_start = pl.multiple_of(jnp.minimum(start + size, L - halo), halo)
    prev = ref[pl.ds(prev_start, halo), :][halo - 1:halo, :].astype(F32)
    nxt = ref[pl.ds(next_start, halo), :][0:1, :].astype(F32)
    prev = jnp.where(start == 0, 0.0, prev)
    nxt = jnp.where(start + size == L, 0.0, nxt)
    row = lax.broadcasted_iota(jnp.int32, (tile, 1), 0)
    up = jnp.concatenate([jnp.where(row == 0, prev, up[:tile]), up[tile:]], axis=0)
    dn = jnp.concatenate([dn[:size - tile], jnp.where(row == tile - 1, nxt, dn[size - tile:])], axis=0)
    return up * w_ref[0:1, :] + cur * w_ref[1:2, :] + dn * w_ref[2:3, :] + b_ref[...]


def _hy_gate_kernel(x0_ref, x1_ref, v_ref, w0_ref, w1_ref, wv_ref, b0_ref, b1_ref, bv_ref,
                    perm_ref, x0c_ref, zb_ref, zp_ref):
    P, MG = DFT_P, DFT_MG
    chunk = P * MG
    L, cb = x0_ref.shape

    def body(c, carry):
        start = pl.multiple_of(c * chunk, chunk)
        rows = pl.ds(start, chunk)
        conv = functools.partial(_short_conv, start=start, size=chunk, L=L)
        x0c_ref[rows, :] = conv(x0_ref, w0_ref, b0_ref).astype(x0c_ref.dtype)
        x1c = conv(x1_ref, w1_ref, b1_ref)
        vc = conv(v_ref, wv_ref, bv_ref)
        z = (x1c * vc).astype(BF16)
        zb_ref[rows, :] = z
        zp = _dot(perm_ref[...], z).reshape(P, MG, cb)
        zp_ref[:, pl.ds(pl.multiple_of(c * MG, MG), MG), :] = zp
        return carry

    lax.fori_loop(0, L // chunk, body, 0, unroll=2)


def _time_split_permutation():
    P, MG = DFT_P, DFT_MG
    r = np.arange(P * MG)
    perm = np.zeros((P * MG, P * MG), np.float32)
    perm[r, (r % MG) * P + r // MG] = 1.0
    return perm


def _hyena_gate(proj, short_w, short_b, B, L, mix_w, cb):
    nc = mix_w // cb
    P = DFT_P
    col = lambda part: pl.BlockSpec((L, cb), lambda b, c: (b, part * nc + c))
    wspec = lambda part: pl.BlockSpec((3, cb), lambda b, c: (0, part * nc + c))
    bspec = lambda part: pl.BlockSpec((1, cb), lambda b, c: (0, part * nc + c))
    perm = jnp.asarray(_time_split_permutation(), BF16)
    seq = pl.BlockSpec((L, cb), lambda b, c: (b, c))
    return pl.pallas_call(
        _hy_gate_kernel,
        grid=(B, nc),
        in_specs=[col(0), col(1), col(2), wspec(0), wspec(1), wspec(2), bspec(0), bspec(1), bspec(2),
                  pl.BlockSpec(perm.shape, lambda b, c: (0, 0))],
        out_specs=[seq, seq, pl.BlockSpec((None, P, L // P, cb), lambda b, c: (b, 0, 0, c))],
        out_shape=[jax.ShapeDtypeStruct((B * L, mix_w), BF16), jax.ShapeDtypeStruct((B * L, mix_w), BF16),
                   jax.ShapeDtypeStruct((B, P, L // P, mix_w), F32)],
        compiler_params=_params("parallel", "parallel"),
        name="hyena_gate",
    )(proj, proj, proj, short_w, short_w, short_w, short_b, short_b, short_b, perm)


def _dft_fwd_kernel(*refs, filtered):
    if filtered:
        z_ref, w1_ref, a3_ref, g_ref, y_ref, v_ref = refs
    else:
        z_ref, w1_ref, a3_ref, y_ref, v_ref = refs
    P, MG = DFT_P, DFT_MG
    cb = z_ref.shape[2]
    ng, rows_g = a3_ref.shape[0], a3_ref.shape[1]
    half = rows_g // 2
    w1 = w1_ref[...]
    for b in range(P):
        v = _dot(w1, z_ref[b].astype(BF16))
        v_ref[:, :, b * MG:(b + 1) * MG, :] = v.reshape(2, ng, MG, cb)

    def group(g, carry):
        vg = jnp.concatenate([v_ref[0, g], v_ref[1, g]], axis=0).astype(BF16)
        x = _dot(a3_ref[g], vg)
        if filtered:
            xr, xi = x[:half], x[half:]
            gr, gi = g_ref[g, :half], g_ref[g, half:]
            x = jnp.concatenate([xr * gr - xi * gi, xr * gi + xi * gr], axis=0)
        y_ref[pl.ds(pl.multiple_of(g * rows_g, rows_g), rows_g), :] = x.astype(y_ref.dtype)
        return carry

    lax.fori_loop(0, ng, group, 0, unroll=8)


def _dft_forward(z, w1, a3, g, Bz, L, cb, out_dtype):
    C = z.shape[3]
    ng, rows_g, cols_g = a3.shape
    const = lambda shape: pl.BlockSpec(shape, lambda c, b: (0,) * len(shape), pipeline_mode=pl.Buffered(1))
    in_specs = [pl.BlockSpec((None, DFT_P, L // DFT_P, cb), lambda c, b: (b, 0, 0, c)),
                const(w1.shape), const(a3.shape)]
    args = [z, w1, a3]
    if g is not None:
        in_specs.append(pl.BlockSpec((ng, rows_g, cb), lambda c, b: (0, 0, c), pipeline_mode=pl.Buffered(1)))
        args.append(g)
    return pl.pallas_call(
        functools.partial(_dft_fwd_kernel, filtered=g is not None),
        grid=(C // cb, Bz),
        in_specs=in_specs,
        out_specs=pl.BlockSpec((None, 2 * L, cb), lambda c, b: (b, 0, c)),
        out_shape=jax.ShapeDtypeStruct((Bz, 2 * L, C), out_dtype),
        scratch_shapes=[pltpu.VMEM((2, ng, cols_g // 2, cb), F32)],
        compiler_params=_params("parallel", "arbitrary"),
        name="hyena_dft_forward",
    )(*args)


def _dft_inv_kernel(y_ref, b3_ref, w1t_ref, zb_ref, x0_ref, d_ref, unperm_ref, o_ref, q_ref, t_ref):
    P, MG = DFT_P, DFT_MG
    chunk = P * MG
    cb = y_ref.shape[1]
    ng, rows_g, cols_g = b3_ref.shape
    half = rows_g // 2

    def group(g, carry):
        yg = y_ref[pl.ds(pl.multiple_of(g * cols_g, cols_g), cols_g), :]
        q = _dot(b3_ref[g], yg)
        q_ref[0, g] = q[:half]
        q_ref[1, g] = q[half:]
        return carry

    lax.fori_loop(0, ng, group, 0, unroll=8)
    w1t = w1t_ref[...]
    for b in range(P):
        qb = q_ref[:, :, b * MG:(b + 1) * MG, :].reshape(2 * ng * MG, cb)
        t_ref[b] = _dot(w1t, qb.astype(BF16))

    unperm = unperm_ref[...]

    def gate(c, carry):
        rows = pl.ds(pl.multiple_of(c * chunk, chunk), chunk)
        yp = t_ref[:, pl.ds(pl.multiple_of(c * MG, MG), MG), :].reshape(chunk, cb)
        hi = yp.astype(BF16)
        lo = (yp - hi.astype(F32)).astype(BF16)
        y = _dot(unperm, hi) + _dot(unperm, lo)
        y = y + d_ref[...] * zb_ref[rows, :].astype(F32)
        o_ref[rows, :] = (x0_ref[rows, :].astype(F32) * y).astype(o_ref.dtype)
        return carry

    lax.fori_loop(0, zb_ref.shape[0] // chunk, gate, 0, unroll=4)


def _dft_inverse(y, b3, w1t, zb, x0c, d, B, L, cb):
    C = zb.shape[1]
    ng, rows_g, _ = b3.shape
    const = lambda shape: pl.BlockSpec(shape, lambda c, b: (0,) * len(shape), pipeline_mode=pl.Buffered(1))
    seq = pl.BlockSpec((L, cb), lambda c, b: (b, c))
    unperm = jnp.asarray(_time_split_permutation().T, BF16)
    return pl.pallas_call(
        _dft_inv_kernel,
        grid=(C // cb, B),
        in_specs=[pl.BlockSpec((None, 2 * L, cb), lambda c, b: (b, 0, c)), const(b3.shape), const(w1t.shape),
                  seq, seq, pl.BlockSpec((1, cb), lambda c, b: (0, c)), const(unperm.shape)],
        out_specs=seq,
        out_shape=jax.ShapeDtypeStruct((B * L, C), BF16),
        scratch_shapes=[pltpu.VMEM((2, ng, rows_g // 2, cb), F32),
                        pltpu.VMEM((DFT_P, L // DFT_P, cb), F32)],
        compiler_params=_params("parallel", "arbitrary"),
        name="hyena_dft_inverse",
    )(y, b3, w1t, zb, x0c, d, unperm)


def _hyena(proj, B, L, mix_w, short_w, short_b, w1, b1, w2, b2, w3, freq, decay, d_bias):
    cb = min(512, HYENA_LANE_ELEMS // L)
    dw1, dw1t, a3, b3 = _dft_tables(L)
    ng, rows_g, _ = a3.shape
    half = rows_g // 2
    taps = _hyena_filter_taps(L, w1, b1, w2, b2, w3, freq, decay, tl=min(512, L))
    spec = _dft_forward(taps.reshape(2, DFT_P, L // DFT_P, mix_w), dw1, a3, None, 2, L, cb, F32)
    spec = spec.reshape(2, ng, rows_g, mix_w)
    g = (spec[0] + jnp.concatenate([spec[1, :, :half], -spec[1, :, half:]], axis=1)) * (1.0 / L)
    x0c, zb, zp = _hyena_gate(proj, short_w.astype(F32), short_b.astype(F32)[None], B, L, mix_w, cb)
    y = _dft_forward(zp, dw1, a3, g, B, L, cb, BF16)
    return _dft_inverse(y, b3, dw1t, zb, x0c, d_bias.astype(F32)[None], B, L, cb)


def _mem_kv_kernel(m_ref, g_ref, w_ref, o_ref):
    h = _rmsnorm(m_ref[...], g_ref[...]).astype(BF16)
    o_ref[...] = _dot(h, w_ref[...]).astype(o_ref.dtype)


def _mem_kv(mem, g, w, layer):
    BM, D = mem.shape
    N = w.shape[2]
    M = 256
    return pl.pallas_call(
        _mem_kv_kernel,
        grid=(BM // M,),
        in_specs=[
            pl.BlockSpec((M, D), lambda b: (b, 0)),
            pl.BlockSpec((1, D), lambda b: (0, 0)),
            pl.BlockSpec((None, D, N), lambda b: (layer, 0, 0)),
        ],
        out_specs=pl.BlockSpec((M, N), lambda b: (b, 0)),
        out_shape=jax.ShapeDtypeStruct((BM, N), BF16),
        compiler_params=_params("parallel"),
        name="memory_kv",
    )(mem, g, w)


def _mix_out_kernel(tok_ref, q_ref, kv_ref, w_ref, x_ref, g_ref, o_ref, mo_ref):
    mem_w = q_ref.shape[1]
    mix_w = tok_ref.shape[1]
    heads = [slice(h * HEAD_DIM, (h + 1) * HEAD_DIM) for h in range(mem_w // HEAD_DIM)]
    scores = [_dot_nt(q_ref[:, c], kv_ref[:, c]) for c in heads]
    o = _dot(tok_ref[...], w_ref[:mix_w, :])
    probs = []
    for s in scores:
        p = jnp.exp2(s - jnp.max(s, axis=-1, keepdims=True))
        probs.append((p.astype(BF16), jnp.sum(p, axis=-1, keepdims=True)))
    for c, (p, l) in zip(heads, probs):
        vcols = slice(mem_w + c.start, mem_w + c.stop)
        mo_ref[:, c] = (_dot(p, kv_ref[:, vcols]) / l).astype(mo_ref.dtype)
    o = o + _dot(mo_ref[...], w_ref[mix_w:, :])
    o_ref[...] = x_ref[...] + _rmsnorm(o, g_ref[...])


def _mix_out(tok, proj, kv, w_out, layer, x, g, L, tm):
    T, D = x.shape
    mix_w = tok.shape[1]
    mem_w = w_out.shape[1] - mix_w
    M = kv.shape[0] // (T // L)
    qblk = (proj.shape[1] - mem_w) // mem_w
    per_b = L // tm
    return pl.pallas_call(
        _mix_out_kernel,
        grid=(T // tm,),
        in_specs=[
            pl.BlockSpec((tm, mix_w), lambda i: (i, 0)),
            pl.BlockSpec((tm, mem_w), lambda i: (i, qblk)),
            pl.BlockSpec((M, 2 * mem_w), lambda i: (i // per_b, 0)),
            pl.BlockSpec((None, mix_w + mem_w, D), lambda i: (layer, 0, 0)),
            pl.BlockSpec((tm, D), lambda i: (i, 0)),
            pl.BlockSpec((1, D), lambda i: (0, 0)),
        ],
        out_specs=pl.BlockSpec((tm, D), lambda i: (i, 0)),
        out_shape=jax.ShapeDtypeStruct((T, D), F32),
        scratch_shapes=[pltpu.VMEM((tm, mem_w), BF16)],
        compiler_params=_params("parallel"),
        name="mix_out_proj",
    )(tok, proj, kv, w_out, x, g)


def _mlp_kernel(x_ref, gpre_ref, wu_ref, wd_ref, gpost_ref, o_ref, h_ref, acc_ref):
    f = pl.program_id(1)
    nf = pl.num_programs(1)
    tm = x_ref.shape[0]

    @pl.when(f == 0)
    def _():
        g = gpre_ref[...]
        for r in range(0, tm, NORM_ROWS):
            rows = slice(r, r + NORM_ROWS)
            h_ref[rows, :] = _rmsnorm(x_ref[rows, :], g).astype(h_ref.dtype)
        acc_ref[...] = jnp.zeros_like(acc_ref)

    h = h_ref[...]
    tf = wu_ref.shape[1]
    cw = tf // MLP_CHUNKS
    chunks = [slice(c * cw, (c + 1) * cw) for c in range(MLP_CHUNKS)]
    ups = [_dot(h, wu_ref[:, c]) for c in chunks]
    acts = [jnp.square(jnp.maximum(u, 0.0)).astype(BF16) for u in ups]
    d = acc_ref[...]
    for a, c in zip(acts, chunks):
        d = d + _dot(a, wd_ref[c, :])
    acc_ref[...] = d

    @pl.when(f == nf - 1)
    def _():
        g = gpost_ref[...]
        for r in range(0, tm, NORM_ROWS):
            rows = slice(r, r + NORM_ROWS)
            o_ref[rows, :] = x_ref[rows, :] + _rmsnorm(acc_ref[rows, :], g)


def _mlp(x, gpre, w_up, w_down, layer, gpost, tm, tf):
    T, D = x.shape
    F = w_up.shape[2]
    return pl.pallas_call(
        _mlp_kernel,
        grid=(T // tm, F // tf),
        in_specs=[
            pl.BlockSpec((tm, D), lambda i, f: (i, 0)),
            pl.BlockSpec((1, D), lambda i, f: (0, 0)),
            pl.BlockSpec((None, D, tf), lambda i, f: (layer, 0, f)),
            pl.BlockSpec((None, tf, D), lambda i, f: (layer, f, 0)),
            pl.BlockSpec((1, D), lambda i, f: (0, 0)),
        ],
        out_specs=pl.BlockSpec((tm, D), lambda i, f: (i, 0)),
        out_shape=jax.ShapeDtypeStruct((T, D), F32),
        scratch_shapes=[pltpu.VMEM((tm, D), BF16), pltpu.VMEM((tm, D), F32)],
        compiler_params=_params("parallel", "arbitrary"),
        name="relu2_mlp",
    )(x, gpre, w_up, w_down, gpost)


def _trunk(x, mem, wts):
    (g_mix_pre, w_in, na_rpb, hy_short_w, hy_short_b, hy_w1, hy_b1, hy_w2, hy_b2, hy_w3, hy_freq,
     hy_decay, hy_d, g_mem, w_mem_kv, w_out, g_mix_post, g_mlp_pre, w_up, w_down, g_mlp_post) = wts
    B, L, D = x.shape
    depth = w_in.shape[0]
    in_w = w_in.shape[2]
    mem_w = w_mem_kv.shape[2] // 2
    mix_w = (in_w - mem_w) // 3
    x = x.reshape(B * L, D)
    mem = mem.reshape(-1, D)
    qk_scale = HEAD_DIM ** -0.5 * LOG2E
    row = lambda v: v.astype(F32)[None]
    for i in range(depth):
        j = i // 2
        na_layer = i % 2 == 0
        colscale = jnp.concatenate([
            jnp.full((mix_w,), qk_scale if na_layer else 1.0, F32),
            jnp.ones((2 * mix_w,), F32),
            jnp.full((mem_w,), qk_scale, F32)])[None]
        proj = _norm_matmul(x, row(g_mix_pre[i]), w_in, i, colscale, tm=min(1024, L), tn=1280)
        if na_layer:
            bias = _na_bias_table(na_rpb[j], L // GRID_W)
            tok = _neighbourhood_attention(proj, bias, B, L, mix_w)
        else:
            tok = _hyena(proj, B, L, mix_w, hy_short_w[j], hy_short_b[j], hy_w1[j], hy_b1[j], hy_w2[j],
                         hy_b2[j], hy_w3[j], hy_freq[j], hy_decay[j], hy_d[j])
        kv = _mem_kv(mem, row(g_mem[i]), w_mem_kv, i)
        x = _mix_out(tok, proj, kv, w_out, i, x, row(g_mix_post[i]), L, tm=min(512, L))
        x = _mlp(x, row(g_mlp_pre[i]), w_up, w_down, i, row(g_mlp_post[i]), tm=min(512, L), tf=1024)
    return x.reshape(B, L, D)


def kernel(x_prompt, x_sample, mem_prompt, mem_sample, g_mix_pre, w_in, na_rpb, hy_short_w, hy_short_b,
           hy_w1, hy_b1, hy_w2, hy_b2, hy_w3, hy_freq, hy_decay, hy_d, g_mem, w_mem_kv, w_out,
           g_mix_post, g_mlp_pre, w_up, w_down, g_mlp_post):
    bf = lambda w: w.astype(BF16)
    wts = (g_mix_pre, bf(w_in), na_rpb, hy_short_w, hy_short_b, hy_w1, hy_b1, hy_w2, hy_b2, hy_w3, hy_freq,
           hy_decay, hy_d, g_mem, bf(w_mem_kv), bf(w_out), g_mix_post, g_mlp_pre, bf(w_up), bf(w_down), g_mlp_post)
    return (_trunk(x_prompt, mem_prompt, wts), _trunk(x_sample, mem_sample, wts))
```

```python
import functools
import math

import numpy as np
import jax
import jax.numpy as jnp
from jax import lax
from jax.experimental import pallas as pl
from jax.experimental.pallas import tpu as pltpu

F32 = jnp.float32
BF16 = jnp.bfloat16

GRID_W = 64
HEAD_DIM = 128
NA_MAX_KH = 8
NA_KW = 16
FILTER_EMB = 33
FILTER_BANDS = (FILTER_EMB - 1) // 2
RMS_EPS = 1e-6
NEG_BIG = -1e30
LOG2E = math.log2(math.e)

VMEM_LIMIT_BYTES = 56 * 1024 * 1024
NA_ROWS_PER_STEP = 4
NA_HEADS_PER_STEP = 6
NA_KEY_ROWS = NA_ROWS_PER_STEP + NA_MAX_KH
MLP_CHUNKS = 2
NORM_ROWS = 64
DFT_P = 32
DFT_MG = 8
HYENA_LANE_ELEMS = 1 << 20


def _params(*sem):
    return pltpu.CompilerParams(dimension_semantics=sem, vmem_limit_bytes=VMEM_LIMIT_BYTES)


def _rmsnorm(x, g):
    ms = jnp.mean(x * x, axis=-1, keepdims=True)
    return x * lax.rsqrt(ms + RMS_EPS) * g


def _dot(a, b):
    return jnp.dot(a, b, preferred_element_type=F32)


def _dot_nt(a, b):
    return lax.dot_general(a, b, (((1,), (1,)), ((), ())), preferred_element_type=F32)


def _norm_matmul_kernel(x_ref, g_ref, w_ref, s_ref, o_ref, h_ref):
    @pl.when(pl.program_id(1) == 0)
    def _():
        g = g_ref[...]
        for r in range(0, h_ref.shape[0], NORM_ROWS):
            rows = slice(r, r + NORM_ROWS)
            h_ref[rows, :] = _rmsnorm(x_ref[rows, :], g).astype(h_ref.dtype)

    o_ref[...] = (_dot(h_ref[...], w_ref[...]) * s_ref[...]).astype(o_ref.dtype)


def _norm_matmul(x, g, w, layer, colscale, tm, tn):
    T, D = x.shape
    N = w.shape[2]
    return pl.pallas_call(
        _norm_matmul_kernel,
        grid=(T // tm, N // tn),
        in_specs=[
            pl.BlockSpec((tm, D), lambda i, j: (i, 0)),
            pl.BlockSpec((1, D), lambda i, j: (0, 0)),
            pl.BlockSpec((None, D, tn), lambda i, j: (layer, 0, j)),
            pl.BlockSpec((1, tn), lambda i, j: (0, j)),
        ],
        out_specs=pl.BlockSpec((tm, tn), lambda i, j: (i, j)),
        out_shape=jax.ShapeDtypeStruct((T, N), BF16),
        scratch_shapes=[pltpu.VMEM((tm, D), BF16)],
        compiler_params=_params("parallel", "arbitrary"),
        name="norm_in_proj",
    )(x, g, w, colscale)


def _na_bias_table(rpb, rows):
    R, KR, W = NA_ROWS_PER_STEP, NA_KEY_ROWS, GRID_W
    kh = min(NA_MAX_KH, rows)
    nblk = rows // R
    qc = np.arange(W)
    cs = np.clip(qc - NA_KW // 2, 0, W - NA_KW)
    kc = np.arange(W)
    col_ok = (kc[None, :] >= cs[:, None]) & (kc[None, :] < cs[:, None] + NA_KW)
    dr_all, ok_all = [], []
    for blk in (0, min(1, nblk - 1), nblk - 1):
        ks = int(np.clip(blk * R - kh // 2, 0, rows - KR))
        r = blk * R + np.arange(R)
        rs = np.clip(r - kh // 2, 0, rows - kh)
        key = ks + np.arange(KR)
        row_ok = (key[None, :] >= rs[:, None]) & (key[None, :] < rs[:, None] + kh)
        dr = np.clip(key[None, :] - r[:, None] + NA_MAX_KH - 1, 0, 2 * NA_MAX_KH - 2)
        dr_all.append(dr)
        ok_all.append(row_ok)
    pad = W - NA_KW
    padded = jnp.pad(rpb.astype(F32) * LOG2E, ((0, 0), (0, 0), (pad, pad)))
    per_dr = jnp.stack([padded[:, :, W - 1 - q:2 * W - 1 - q] for q in range(W)], axis=2)
    per_dr = jnp.where(col_ok, per_dr, NEG_BIG)
    H = rpb.shape[0]

    def assemble(p_ref, o_ref):
        masked = jnp.full((W, W), NEG_BIG, F32)
        block = lambda v, r, k: p_ref[0, int(dr_all[v][r, k])] if ok_all[v][r, k] else masked
        for v in range(3):
            for r in range(R):
                for k in range(0, KR, 2):
                    o_ref[v, 0, r * W:(r + 1) * W, k * W:(k + 2) * W] = jnp.concatenate(
                        [block(v, r, k), block(v, r, k + 1)], axis=1)

    return pl.pallas_call(
        assemble,
        grid=(H,),
        in_specs=[pl.BlockSpec((1,) + per_dr.shape[1:], lambda h: (h, 0, 0, 0))],
        out_specs=pl.BlockSpec((3, 1, R * W, KR * W), lambda h: (0, h, 0, 0)),
        out_shape=jax.ShapeDtypeStruct((3, H, R * W, KR * W), F32),
        compiler_params=_params("parallel"),
        name="na_bias_table",
    )(per_dr)


def _na_kernel(q_ref, k_ref, v_ref, b_ref, o_ref, *, rows):
    R, KR, W = NA_ROWS_PER_STEP, NA_KEY_ROWS, GRID_W
    i = pl.program_id(2)
    ks = jnp.clip(i * R - min(NA_MAX_KH, rows) // 2, 0, rows - KR) * W
    ks = pl.multiple_of(ks, W)
    heads = [slice(h * HEAD_DIM, (h + 1) * HEAD_DIM) for h in range(NA_HEADS_PER_STEP)]
    keys = pl.ds(ks, KR * W)
    scores = [_dot_nt(q_ref[:, c], k_ref[keys, c]) + b_ref[0, h] for h, c in enumerate(heads)]
    probs = []
    for s in scores:
        p = jnp.exp2(s - jnp.max(s, axis=-1, keepdims=True))
        probs.append((p.astype(BF16), jnp.sum(p, axis=-1, keepdims=True)))
    for c, (p, l) in zip(heads, probs):
        o_ref[:, c] = (_dot(p, v_ref[keys, c]) / l).astype(o_ref.dtype)


def _neighbourhood_attention(proj, bias, B, L, mix_w):
    R, KR, W = NA_ROWS_PER_STEP, NA_KEY_ROWS, GRID_W
    rows = L // W
    nblk = rows // R
    hw = NA_HEADS_PER_STEP * HEAD_DIM
    ngrp = mix_w // hw
    assert rows % R == 0 and rows >= KR and nblk >= 2 and R >= NA_MAX_KH // 2

    def variant(i):
        return jnp.where(i == 0, 0, jnp.where(i == nblk - 1, 2, 1))

    return pl.pallas_call(
        functools.partial(_na_kernel, rows=rows),
        grid=(B, ngrp, nblk),
        in_specs=[
            pl.BlockSpec((R * W, hw), lambda b, g, i: (b * nblk + i, g)),
            pl.BlockSpec((L, hw), lambda b, g, i: (b, ngrp + g)),
            pl.BlockSpec((L, hw), lambda b, g, i: (b, 2 * ngrp + g)),
            pl.BlockSpec((1, NA_HEADS_PER_STEP, R * W, KR * W), lambda b, g, i: (variant(i), g, 0, 0)),
        ],
        out_specs=pl.BlockSpec((R * W, hw), lambda b, g, i: (b * nblk + i, g)),
        out_shape=jax.ShapeDtypeStruct((B * L, mix_w), BF16),
        compiler_params=_params("parallel", "parallel", "arbitrary"),
        name="neighbourhood_attention",
    )(proj, proj, proj, bias)


def _position_features(L):
    P = DFT_P
    pos = (jnp.arange(L // P, dtype=F32)[None, :] * P + jnp.arange(P, dtype=F32)[:, None]).reshape(L)
    t = pos / (L - 1)
    wpos = 2.0 * math.pi * pos / L
    fb = jnp.linspace(1e-4, FILTER_BANDS - 1, FILTER_BANDS, dtype=F32)
    ang = wpos[:, None] * fb[None, :]
    z = jnp.concatenate([t[:, None], jnp.cos(ang), -jnp.sin(ang)], axis=-1)
    return jnp.pad(z, ((0, 0), (0, 128 - FILTER_EMB)))


def _hy_filter_kernel(z_ref, w1_ref, b1_ref, w2_ref, b2_ref, w3_ref, fr_ref, dec_ref, h_ref, *, mix_w):
    hp = lax.Precision.HIGHEST
    fr = fr_ref[...]
    z = z_ref[...]
    h = jnp.sin(fr * (jnp.dot(z, w1_ref[...], precision=hp, preferred_element_type=F32) + b1_ref[...]))
    h = jnp.sin(fr * (jnp.dot(h, w2_ref[...], precision=hp, preferred_element_type=F32) + b2_ref[...]))

    def taps(d):
        w3 = w3_ref[:, d * mix_w:(d + 1) * mix_w]
        win = jnp.exp(-z[:, 0:1] * jnp.abs(dec_ref[d:d + 1, :]))
        return jnp.dot(h, w3, precision=hp, preferred_element_type=F32) * win

    tl = z_ref.shape[0]
    lag0 = pl.program_id(0) * tl + lax.broadcasted_iota(jnp.int32, (tl, 1), 0) == 0
    h_fwd, h_bwd = taps(0), taps(1)
    h_ref[0] = h_fwd + jnp.where(lag0, h_bwd, 0.0)
    h_ref[1] = jnp.where(lag0, 0.0, h_bwd)


def _hyena_filter_taps(L, w1, b1, w2, b2, w3, freq, decay, tl):
    mix_w = decay.shape[1]
    order = w2.shape[0]
    w1p = jnp.pad(w1.astype(F32), ((0, 128 - FILTER_EMB), (0, 0)))
    full = lambda shape: pl.BlockSpec(shape, lambda i: (0, 0))
    return pl.pallas_call(
        functools.partial(_hy_filter_kernel, mix_w=mix_w),
        grid=(L // tl,),
        in_specs=[
            pl.BlockSpec((tl, 128), lambda i: (i, 0)),
            full((128, order)), full((1, order)), full((order, order)), full((1, order)),
            full((order, 2 * mix_w)), full((1, order)), full((2, mix_w)),
        ],
        out_specs=pl.BlockSpec((2, tl, mix_w), lambda i: (0, i, 0)),
        out_shape=jax.ShapeDtypeStruct((2, L, mix_w), F32),
        compiler_params=_params("parallel"),
        name="hyena_filter_taps",
    )(_position_features(L), w1p, b1.astype(F32)[None], w2.astype(F32), b2.astype(F32)[None],
      w3.astype(F32), freq.astype(F32)[None], decay.astype(F32))


def _dft_tables(L):
    P, MG = DFT_P, DFT_MG
    A, n = L // P, 2 * L
    Mm, J = 2 * A, P // 2
    NG = Mm // MG
    ar = lambda size: jnp.arange(size, dtype=jnp.int32)
    ang1 = (((2 * ar(Mm) + 1)[:, None] * ar(A)[None, :]) % (2 * Mm)).astype(F32) * (math.pi / Mm)
    w1 = jnp.concatenate([jnp.cos(ang1), -jnp.sin(ang1)], axis=0)
    k = (Mm * ar(J)[None, None, :, None] + MG * ar(NG)[:, None, None, None] + ar(MG)[None, :, None, None])
    ph = ((ar(P)[None, None, None, :] * (2 * k + 1)) % (2 * n)).astype(F32) * (math.pi / n)
    cr, ci = jnp.cos(ph), -jnp.sin(ph)
    eye = jnp.eye(MG, dtype=F32)
    blk = lambda c: c[..., None] * eye[None, :, None, None, :]
    top = jnp.stack([blk(cr), blk(-ci)], axis=3)
    bot = jnp.stack([blk(ci), blk(cr)], axis=3)
    a3 = jnp.stack([top, bot], axis=1).reshape(NG, 2 * MG * J, 2 * P * MG)
    return w1.astype(BF16), w1.T.astype(BF16), a3.astype(BF16), a3.transpose(0, 2, 1).astype(BF16)


def _short_conv(ref, w_ref, b_ref, start, size, L):
    halo, tile = 16, 8
    cur = ref[pl.ds(start, size), :].astype(F32)
    up, dn = pltpu.roll(cur, 1, axis=0), pltpu.roll(cur, size - 1, axis=0)
    prev_start = pl.multiple_of(jnp.maximum(start - halo, 0), halo)
    next_start = pl.multiple_of(jnp.minimum(start + size, L - halo), halo)
    prev = ref[pl.ds(prev_start, halo), :][halo - 1:halo, :].astype(F32)
    nxt = ref[pl.ds(next_start, halo), :][0:1, :].astype(F32)
    prev = jnp.where(start == 0, 0.0, prev)
    nxt = jnp.where(start + size == L, 0.0, nxt)
    row = lax.broadcasted_iota(jnp.int32, (tile, 1), 0)
    up = jnp.concatenate([jnp.where(row == 0, prev, up[:tile]), up[tile:]], axis=0)
    dn = jnp.concatenate([dn[:size - tile], jnp.where(row == tile - 1, nxt, dn[size - tile:])], axis=0)
    return up * w_ref[0:1, :] + cur * w_ref[1:2, :] + dn * w_ref[2:3, :] + b_ref[...]


def _hy_gate_kernel(x0_ref, x1_ref, v_ref, w0_ref, w1_ref, wv_ref, b0_ref, b1_ref, bv_ref,
                    perm_ref, x0c_ref, zb_ref, zp_ref):
    P, MG = DFT_P, DFT_MG
    chunk = P * MG
    L, cb = x0_ref.shape

    def body(c, carry):
        start = pl.multiple_of(c * chunk, chunk)
        rows = pl.ds(start, chunk)
        conv = functools.partial(_short_conv, start=start, size=chunk, L=L)
        x0c_ref[rows, :] = conv(x0_ref, w0_ref, b0_ref).astype(x0c_ref.dtype)
        x1c = conv(x1_ref, w1_ref, b1_ref)
        vc = conv(v_ref, wv_ref, bv_ref)
        z = (x1c * vc).astype(BF16)
        zb_ref[rows, :] = z
        zp = _dot(perm_ref[...], z).reshape(P, MG, cb)
        zp_ref[:, pl.ds(pl.multiple_of(c * MG, MG), MG), :] = zp
        return carry

    lax.fori_loop(0, L // chunk, body, 0, unroll=2)


def _time_split_permutation():
    P, MG = DFT_P, DFT_MG
    r = np.arange(P * MG)
    perm = np.zeros((P * MG, P * MG), np.float32)
    perm[r, (r % MG) * P + r // MG] = 1.0
    return perm


def _hyena_gate(proj, short_w, short_b, B, L, mix_w, cb):
    nc = mix_w // cb
    P = DFT_P
    col = lambda part: pl.BlockSpec((L, cb), lambda b, c: (b, part * nc + c))
    wspec = lambda part: pl.BlockSpec((3, cb), lambda b, c: (0, part * nc + c))
    bspec = lambda part: pl.BlockSpec((1, cb), lambda b, c: (0, part * nc + c))
    perm = jnp.asarray(_time_split_permutation(), BF16)
    seq = pl.BlockSpec((None, None, L, cb), lambda b, c: (b, c, 0, 0))
    return pl.pallas_call(
        _hy_gate_kernel,
        grid=(B, nc),
        in_specs=[col(0), col(1), col(2), wspec(0), wspec(1), wspec(2), bspec(0), bspec(1), bspec(2),
                  pl.BlockSpec(perm.shape, lambda b, c: (0, 0))],
        out_specs=[seq, seq, pl.BlockSpec((None, None, P, L // P, cb), lambda b, c: (b, c, 0, 0, 0))],
        out_shape=[jax.ShapeDtypeStruct((B, nc, L, cb), BF16), jax.ShapeDtypeStruct((B, nc, L, cb), BF16),
                   jax.ShapeDtypeStruct((B, nc, P, L // P, cb), F32)],
        compiler_params=_params("parallel", "parallel"),
        name="hyena_gate",
    )(proj, proj, proj, short_w, short_w, short_w, short_b, short_b, short_b, perm)


def _dft_fwd_kernel(*refs, filtered):
    if filtered:
        z_ref, w1_ref, a3_ref, g_ref, y_ref, v_ref = refs
    else:
        z_ref, w1_ref, a3_ref, y_ref, v_ref = refs
    P, MG = DFT_P, DFT_MG
    cb = z_ref.shape[2]
    ng, rows_g = a3_ref.shape[0], a3_ref.shape[1]
    half = rows_g // 2
    w1 = w1_ref[...]
    for b in range(P):
        v = _dot(w1, z_ref[b].astype(BF16))
        v_ref[:, :, b * MG:(b + 1) * MG, :] = v.reshape(2, ng, MG, cb)

    if not filtered:
        backward = pl.program_id(1) == 1
        scale = 2.0 / y_ref.shape[0]
        im_scale = jnp.where(backward, -scale, scale)

        @pl.when(jnp.logical_not(backward))
        def _():
            y_ref[...] = jnp.zeros_like(y_ref)

    def group(g, carry):
        vg = jnp.concatenate([v_ref[0, g], v_ref[1, g]], axis=0).astype(BF16)
        x = _dot(a3_ref[g], vg)
        rows = pl.ds(pl.multiple_of(g * rows_g, rows_g), rows_g)
        xr, xi = x[:half], x[half:]
        if filtered:
            gr, gi = g_ref[g, :half], g_ref[g, half:]
            y_ref[rows, :] = jnp.concatenate([xr * gr - xi * gi, xr * gi + xi * gr], axis=0).astype(y_ref.dtype)
        else:
            y_ref[rows, :] += jnp.concatenate([xr * scale, xi * im_scale], axis=0)
        return carry

    lax.fori_loop(0, ng, group, 0, unroll=8)


def _dft_forward(z, w1, a3, g, Bz, L, cb, out_dtype):
    ng, rows_g, cols_g = a3.shape
    const = lambda shape: pl.BlockSpec(shape, lambda c, b: (0,) * len(shape), pipeline_mode=pl.Buffered(1))
    P, A = DFT_P, L // DFT_P
    if g is not None:
        nc = z.shape[1]
        in_specs = [pl.BlockSpec((None, None, P, A, cb), lambda c, b: (b, c, 0, 0, 0)), const(w1.shape),
                    const(a3.shape),
                    pl.BlockSpec((ng, rows_g, cb), lambda c, b: (0, 0, c), pipeline_mode=pl.Buffered(1))]
        args = [z, w1, a3, g]
        out_spec = pl.BlockSpec((None, None, 2 * L, cb), lambda c, b: (b, c, 0, 0))
        out_shape = (Bz, nc, 2 * L, cb)
    else:
        assert Bz == 2
        nc = z.shape[3] // cb
        in_specs = [pl.BlockSpec((None, P, A, cb), lambda c, b: (b, 0, 0, c)), const(w1.shape), const(a3.shape)]
        args = [z, w1, a3]
        out_spec = pl.BlockSpec((None, 2 * L, cb), lambda c, b: (0, 0, c))
        out_shape = (1, 2 * L, z.shape[3])
    return pl.pallas_call(
        functools.partial(_dft_fwd_kernel, filtered=g is not None),
        grid=(nc, Bz),
        in_specs=in_specs,
        out_specs=out_spec,
        out_shape=jax.ShapeDtypeStruct(out_shape, out_dtype),
        scratch_shapes=[pltpu.VMEM((2, ng, cols_g // 2, cb), F32)],
        compiler_params=_params("parallel", "arbitrary"),
        name="hyena_dft_forward",
    )(*args)


def _dft_inv_kernel(y_ref, b3_ref, w1t_ref, zb_ref, x0_ref, d_ref, unperm_ref, o_ref, q_ref, t_ref):
    P, MG = DFT_P, DFT_MG
    chunk = P * MG
    cb = y_ref.shape[1]
    ng, rows_g, cols_g = b3_ref.shape
    half = rows_g // 2

    def group(g, carry):
        yg = y_ref[pl.ds(pl.multiple_of(g * cols_g, cols_g), cols_g), :]
        q = _dot(b3_ref[g], yg)
        q_ref[0, g] = q[:half]
        q_ref[1, g] = q[half:]
        return carry

    lax.fori_loop(0, ng, group, 0, unroll=8)
    w1t = w1t_ref[...]
    for b in range(P):
        qb = q_ref[:, :, b * MG:(b + 1) * MG, :].reshape(2 * ng * MG, cb)
        t_ref[b] = _dot(w1t, qb.astype(BF16))

    unperm = unperm_ref[...]

    def gate(c, carry):
        rows = pl.ds(pl.multiple_of(c * chunk, chunk), chunk)
        yp = t_ref[:, pl.ds(pl.multiple_of(c * MG, MG), MG), :].reshape(chunk, cb)
        hi = yp.astype(BF16)
        lo = (yp - hi.astype(F32)).astype(BF16)
        y = _dot(unperm, hi) + _dot(unperm, lo)
        y = y + d_ref[...] * zb_ref[rows, :].astype(F32)
        o_ref[rows, :] = (x0_ref[rows, :].astype(F32) * y).astype(o_ref.dtype)
        return carry

    lax.fori_loop(0, zb_ref.shape[0] // chunk, gate, 0, unroll=4)


def _dft_inverse(y, b3, w1t, zb, x0c, d, B, L, cb):
    nc = zb.shape[1]
    ng, rows_g, _ = b3.shape
    const = lambda shape: pl.BlockSpec(shape, lambda c, b: (0,) * len(shape), pipeline_mode=pl.Buffered(1))
    blocked = lambda rows: pl.BlockSpec((None, None, rows, cb), lambda c, b: (b, c, 0, 0))
    unperm = jnp.asarray(_time_split_permutation().T, BF16)
    return pl.pallas_call(
        _dft_inv_kernel,
        grid=(nc, B),
        in_specs=[blocked(2 * L), const(b3.shape), const(w1t.shape), blocked(L), blocked(L),
                  pl.BlockSpec((1, cb), lambda c, b: (0, c)), const(unperm.shape)],
        out_specs=pl.BlockSpec((L, cb), lambda c, b: (b, c)),
        out_shape=jax.ShapeDtypeStruct((B * L, nc * cb), BF16),
        scratch_shapes=[pltpu.VMEM((2, ng, rows_g // 2, cb), F32),
                        pltpu.VMEM((DFT_P, L // DFT_P, cb), F32)],
        compiler_params=_params("parallel", "arbitrary"),
        name="hyena_dft_inverse",
    )(y, b3, w1t, zb, x0c, d, unperm)


def _hyena(proj, B, L, mix_w, short_w, short_b, w1, b1, w2, b2, w3, freq, decay, d_bias):
    cb = min(512, HYENA_LANE_ELEMS // L)
    dw1, dw1t, a3, b3 = _dft_tables(L)
    ng, rows_g, _ = a3.shape
    taps = _hyena_filter_taps(L, w1, b1, w2, b2, w3, freq, decay, tl=min(512, L))
    g = _dft_forward(taps.reshape(2, DFT_P, L // DFT_P, mix_w), dw1, a3, None, 2, L, cb, F32)
    g = g.reshape(ng, rows_g, mix_w)
    x0c, zb, zp = _hyena_gate(proj, short_w.astype(F32), short_b.astype(F32)[None], B, L, mix_w, cb)
    y = _dft_forward(zp, dw1, a3, g, B, L, cb, BF16)
    return _dft_inverse(y, b3, dw1t, zb, x0c, d_bias.astype(F32)[None], B, L, cb)


def _mem_kv_kernel(m_ref, g_ref, w_ref, o_ref):
    h = _rmsnorm(m_ref[...], g_ref[...]).astype(BF16)
    o_ref[...] = _dot(h, w_ref[...]).astype(o_ref.dtype)


def _mem_kv(mem, g, w, layer):
    BM, D = mem.shape
    N = w.shape[2]
    M = 256
    return pl.pallas_call(
        _mem_kv_kernel,
        grid=(BM // M,),
        in_specs=[
            pl.BlockSpec((M, D), lambda b: (b, 0)),
            pl.BlockSpec((1, D), lambda b: (0, 0)),
            pl.BlockSpec((None, D, N), lambda b: (layer, 0, 0)),
        ],
        out_specs=pl.BlockSpec((M, N), lambda b: (b, 0)),
        out_shape=jax.ShapeDtypeStruct((BM, N), BF16),
        compiler_params=_params("parallel"),
        name="memory_kv",
    )(mem, g, w)


def _mix_out_kernel(tok_ref, q_ref, kv_ref, w_ref, x_ref, g_ref, o_ref, mo_ref):
    mem_w = q_ref.shape[1]
    mix_w = tok_ref.shape[1]
    heads = [slice(h * HEAD_DIM, (h + 1) * HEAD_DIM) for h in range(mem_w // HEAD_DIM)]
    scores = [_dot_nt(q_ref[:, c], kv_ref[:, c]) for c in heads]
    o = _dot(tok_ref[...], w_ref[:mix_w, :])
    probs = []
    for s in scores:
        p = jnp.exp2(s - jnp.max(s, axis=-1, keepdims=True))
        probs.append((p.astype(BF16), jnp.sum(p, axis=-1, keepdims=True)))
    for c, (p, l) in zip(heads, probs):
        vcols = slice(mem_w + c.start, mem_w + c.stop)
        mo_ref[:, c] = (_dot(p, kv_ref[:, vcols]) / l).astype(mo_ref.dtype)
    o = o + _dot(mo_ref[...], w_ref[mix_w:, :])
    o_ref[...] = x_ref[...] + _rmsnorm(o, g_ref[...])


def _mix_out(tok, proj, kv, w_out, layer, x, g, L, tm):
    T, D = x.shape
    mix_w = tok.shape[1]
    mem_w = w_out.shape[1] - mix_w
    M = kv.shape[0] // (T // L)
    qblk = (proj.shape[1] - mem_w) // mem_w
    per_b = L // tm
    return pl.pallas_call(
        _mix_out_kernel,
        grid=(T // tm,),
        in_specs=[
            pl.BlockSpec((tm, mix_w), lambda i: (i, 0)),
            pl.BlockSpec((tm, mem_w), lambda i: (i, qblk)),
            pl.BlockSpec((M, 2 * mem_w), lambda i: (i // per_b, 0)),
            pl.BlockSpec((None, mix_w + mem_w, D), lambda i: (layer, 0, 0)),
            pl.BlockSpec((tm, D), lambda i: (i, 0)),
            pl.BlockSpec((1, D), lambda i: (0, 0)),
        ],
        out_specs=pl.BlockSpec((tm, D), lambda i: (i, 0)),
        out_shape=jax.ShapeDtypeStruct((T, D), F32),
        scratch_shapes=[pltpu.VMEM((tm, mem_w), BF16)],
        compiler_params=_params("parallel"),
        name="mix_out_proj",
    )(tok, proj, kv, w_out, x, g)


def _mlp_kernel(x_ref, gpre_ref, wu_ref, wd_ref, gpost_ref, o_ref, h_ref, acc_ref):
    f = pl.program_id(1)
    nf = pl.num_programs(1)
    tm = x_ref.shape[0]

    @pl.when(f == 0)
    def _():
        g = gpre_ref[...]
        for r in range(0, tm, NORM_ROWS):
            rows = slice(r, r + NORM_ROWS)
            h_ref[rows, :] = _rmsnorm(x_ref[rows, :], g).astype(h_ref.dtype)
        acc_ref[...] = jnp.zeros_like(acc_ref)

    h = h_ref[...]
    tf = wu_ref.shape[1]
    cw = tf // MLP_CHUNKS
    chunks = [slice(c * cw, (c + 1) * cw) for c in range(MLP_CHUNKS)]
    ups = [_dot(h, wu_ref[:, c]) for c in chunks]
    acts = [jnp.square(jnp.maximum(u, 0.0)).astype(BF16) for u in ups]
    d = acc_ref[...]
    for a, c in zip(acts, chunks):
        d = d + _dot(a, wd_ref[c, :])
    acc_ref[...] = d

    @pl.when(f == nf - 1)
    def _():
        g = gpost_ref[...]
        for r in range(0, tm, NORM_ROWS):
            rows = slice(r, r + NORM_ROWS)
            o_ref[rows, :] = x_ref[rows, :] + _rmsnorm(acc_ref[rows, :], g)


def _mlp(x, gpre, w_up, w_down, layer, gpost, tm, tf):
    T, D = x.shape
    F = w_up.shape[2]
    return pl.pallas_call(
        _mlp_kernel,
        grid=(T // tm, F // tf),
        in_specs=[
            pl.BlockSpec((tm, D), lambda i, f: (i, 0)),
            pl.BlockSpec((1, D), lambda i, f: (0, 0)),
            pl.BlockSpec((None, D, tf), lambda i, f: (layer, 0, f)),
            pl.BlockSpec((None, tf, D), lambda i, f: (layer, f, 0)),
            pl.BlockSpec((1, D), lambda i, f: (0, 0)),
        ],
        out_specs=pl.BlockSpec((tm, D), lambda i, f: (i, 0)),
        out_shape=jax.ShapeDtypeStruct((T, D), F32),
        scratch_shapes=[pltpu.VMEM((tm, D), BF16), pltpu.VMEM((tm, D), F32)],
        compiler_params=_params("parallel", "arbitrary"),
        name="relu2_mlp",
    )(x, gpre, w_up, w_down, gpost)


def _trunk(x, mem, wts):
    (g_mix_pre, w_in, na_rpb, hy_short_w, hy_short_b, hy_w1, hy_b1, hy_w2, hy_b2, hy_w3, hy_freq,
     hy_decay, hy_d, g_mem, w_mem_kv, w_out, g_mix_post, g_mlp_pre, w_up, w_down, g_mlp_post) = wts
    B, L, D = x.shape
    depth = w_in.shape[0]
    in_w = w_in.shape[2]
    mem_w = w_mem_kv.shape[2] // 2
    mix_w = (in_w - mem_w) // 3
    x = x.reshape(B * L, D)
    mem = mem.reshape(-1, D)
    qk_scale = HEAD_DIM ** -0.5 * LOG2E
    row = lambda v: v.astype(F32)[None]
    for i in range(depth):
        j = i // 2
        na_layer = i % 2 == 0
        colscale = jnp.concatenate([
            jnp.full((mix_w,), qk_scale if na_layer else 1.0, F32),
            jnp.ones((2 * mix_w,), F32),
            jnp.full((mem_w,), qk_scale, F32)])[None]
        proj = _norm_matmul(x, row(g_mix_pre[i]), w_in, i, colscale, tm=min(1024, L), tn=1280)
        if na_layer:
            bias = _na_bias_table(na_rpb[j], L // GRID_W)
            tok = _neighbourhood_attention(proj, bias, B, L, mix_w)
        else:
            tok = _hyena(proj, B, L, mix_w, hy_short_w[j], hy_short_b[j], hy_w1[j], hy_b1[j], hy_w2[j],
                         hy_b2[j], hy_w3[j], hy_freq[j], hy_decay[j], hy_d[j])
        kv = _mem_kv(mem, row(g_mem[i]), w_mem_kv, i)
        x = _mix_out(tok, proj, kv, w_out, i, x, row(g_mix_post[i]), L, tm=min(512, L))
        x = _mlp(x, row(g_mlp_pre[i]), w_up, w_down, i, row(g_mlp_post[i]), tm=min(512, L), tf=1024)
    return x.reshape(B, L, D)


def kernel(x_prompt, x_sample, mem_prompt, mem_sample, g_mix_pre, w_in, na_rpb, hy_short_w, hy_short_b,
           hy_w1, hy_b1, hy_w2, hy_b2, hy_w3, hy_freq, hy_decay, hy_d, g_mem, w_mem_kv, w_out,
           g_mix_post, g_mlp_pre, w_up, w_down, g_mlp_post):
    bf = lambda w: w.astype(BF16)
    wts = (g_mix_pre, bf(w_in), na_rpb, hy_short_w, hy_short_b, hy_w1, hy_b1, hy_w2, hy_b2, hy_w3, hy_freq,
           hy_decay, hy_d, g_mem, bf(w_mem_kv), bf(w_out), g_mix_post, g_mlp_pre, bf(w_up), bf(w_down), g_mlp_post)
    return (_trunk(x_prompt, mem_prompt, wts), _trunk(x_sample, mem_sample, wts))
```

```python
import functools
import math

import numpy as np
import jax
import jax.numpy as jnp
from jax import lax
from jax.experimental import pallas as pl
from jax.experimental.pallas import tpu as pltpu

F32 = jnp.float32
BF16 = jnp.bfloat16

GRID_W = 64
HEAD_DIM = 128
NA_MAX_KH = 8
NA_KW = 16
FILTER_EMB = 33
FILTER_BANDS = (FILTER_EMB - 1) // 2
RMS_EPS = 1e-6
NEG_BIG = -1e30
LOG2E = math.log2(math.e)

VMEM_LIMIT_BYTES = 56 * 1024 * 1024
MLP_VMEM_LIMIT_BYTES = 60 * 1024 * 1024
NA_ROWS_PER_STEP = 4
NA_HEADS_PER_STEP = 6
NA_KEY_ROWS = NA_ROWS_PER_STEP + NA_MAX_KH
MLP_CHUNKS = 2
NORM_ROWS = 64
DFT_P = 32
DFT_MG = 8
HYENA_LANE_ELEMS = 1 << 20


def _params(*sem):
    return pltpu.CompilerParams(dimension_semantics=sem, vmem_limit_bytes=VMEM_LIMIT_BYTES)


def _rmsnorm(x, g):
    ms = jnp.mean(x * x, axis=-1, keepdims=True)
    return x * lax.rsqrt(ms + RMS_EPS) * g


def _dot(a, b):
    return jnp.dot(a, b, preferred_element_type=F32)


def _dot_nt(a, b):
    return lax.dot_general(a, b, (((1,), (1,)), ((), ())), preferred_element_type=F32)


def _norm_matmul_kernel(x_ref, g_ref, w_ref, s_ref, o_ref, h_ref):
    @pl.when(pl.program_id(1) == 0)
    def _():
        g = g_ref[...]
        for r in range(0, h_ref.shape[0], NORM_ROWS):
            rows = slice(r, r + NORM_ROWS)
            h_ref[rows, :] = _rmsnorm(x_ref[rows, :], g).astype(h_ref.dtype)

    o_ref[...] = (_dot(h_ref[...], w_ref[...]) * s_ref[...]).astype(o_ref.dtype)


def _norm_matmul(x, g, w, layer, colscale, tm, tn):
    T, D = x.shape
    N = w.shape[2]
    return pl.pallas_call(
        _norm_matmul_kernel,
        grid=(T // tm, N // tn),
        in_specs=[
            pl.BlockSpec((tm, D), lambda i, j: (i, 0)),
            pl.BlockSpec((1, D), lambda i, j: (0, 0)),
            pl.BlockSpec((None, D, tn), lambda i, j: (layer, 0, j)),
            pl.BlockSpec((1, tn), lambda i, j: (0, j)),
        ],
        out_specs=pl.BlockSpec((tm, tn), lambda i, j: (i, j)),
        out_shape=jax.ShapeDtypeStruct((T, N), BF16),
        scratch_shapes=[pltpu.VMEM((tm, D), BF16)],
        compiler_params=_params("parallel", "arbitrary"),
        name="norm_in_proj",
    )(x, g, w, colscale)


def _na_bias_table(rpb, rows):
    R, KR, W = NA_ROWS_PER_STEP, NA_KEY_ROWS, GRID_W
    kh = min(NA_MAX_KH, rows)
    nblk = rows // R
    qc = np.arange(W)
    cs = np.clip(qc - NA_KW // 2, 0, W - NA_KW)
    kc = np.arange(W)
    col_ok = (kc[None, :] >= cs[:, None]) & (kc[None, :] < cs[:, None] + NA_KW)
    dr_all, ok_all = [], []
    for blk in (0, min(1, nblk - 1), nblk - 1):
        ks = int(np.clip(blk * R - kh // 2, 0, rows - KR))
        r = blk * R + np.arange(R)
        rs = np.clip(r - kh // 2, 0, rows - kh)
        key = ks + np.arange(KR)
        row_ok = (key[None, :] >= rs[:, None]) & (key[None, :] < rs[:, None] + kh)
        dr = np.clip(key[None, :] - r[:, None] + NA_MAX_KH - 1, 0, 2 * NA_MAX_KH - 2)
        dr_all.append(dr)
        ok_all.append(row_ok)
    pad = W - NA_KW
    padded = jnp.pad(rpb.astype(F32) * LOG2E, ((0, 0), (0, 0), (pad, pad)))
    per_dr = jnp.stack([padded[:, :, W - 1 - q:2 * W - 1 - q] for q in range(W)], axis=2)
    per_dr = jnp.where(col_ok, per_dr, NEG_BIG)
    H = rpb.shape[0]

    def assemble(p_ref, o_ref):
        masked = jnp.full((W, W), NEG_BIG, F32)
        block = lambda v, r, k: p_ref[0, int(dr_all[v][r, k])] if ok_all[v][r, k] else masked
        for v in range(3):
            for r in range(R):
                for k in range(0, KR, 2):
                    o_ref[v, 0, r * W:(r + 1) * W, k * W:(k + 2) * W] = jnp.concatenate(
                        [block(v, r, k), block(v, r, k + 1)], axis=1)

    return pl.pallas_call(
        assemble,
        grid=(H,),
        in_specs=[pl.BlockSpec((1,) + per_dr.shape[1:], lambda h: (h, 0, 0, 0))],
        out_specs=pl.BlockSpec((3, 1, R * W, KR * W), lambda h: (0, h, 0, 0)),
        out_shape=jax.ShapeDtypeStruct((3, H, R * W, KR * W), F32),
        compiler_params=_params("parallel"),
        name="na_bias_table",
    )(per_dr)


def _na_kernel(q_ref, k_ref, v_ref, b_ref, o_ref, *, rows):
    R, KR, W = NA_ROWS_PER_STEP, NA_KEY_ROWS, GRID_W
    i = pl.program_id(2)
    ks = jnp.clip(i * R - min(NA_MAX_KH, rows) // 2, 0, rows - KR) * W
    ks = pl.multiple_of(ks, W)
    heads = [slice(h * HEAD_DIM, (h + 1) * HEAD_DIM) for h in range(NA_HEADS_PER_STEP)]
    keys = pl.ds(ks, KR * W)
    scores = [_dot_nt(q_ref[:, c], k_ref[keys, c]) + b_ref[0, h] for h, c in enumerate(heads)]
    probs = []
    for s in scores:
        p = jnp.exp2(s - jnp.max(s, axis=-1, keepdims=True))
        probs.append((p.astype(BF16), jnp.sum(p, axis=-1, keepdims=True)))
    for c, (p, l) in zip(heads, probs):
        o_ref[:, c] = (_dot(p, v_ref[keys, c]) / l).astype(o_ref.dtype)


def _neighbourhood_attention(proj, bias, B, L, mix_w):
    R, KR, W = NA_ROWS_PER_STEP, NA_KEY_ROWS, GRID_W
    rows = L // W
    nblk = rows // R
    hw = NA_HEADS_PER_STEP * HEAD_DIM
    ngrp = mix_w // hw
    assert rows % R == 0 and rows >= KR and nblk >= 2 and R >= NA_MAX_KH // 2

    def variant(i):
        return jnp.where(i == 0, 0, jnp.where(i == nblk - 1, 2, 1))

    return pl.pallas_call(
        functools.partial(_na_kernel, rows=rows),
        grid=(B, ngrp, nblk),
        in_specs=[
            pl.BlockSpec((R * W, hw), lambda b, g, i: (b * nblk + i, g)),
            pl.BlockSpec((L, hw), lambda b, g, i: (b, ngrp + g)),
            pl.BlockSpec((L, hw), lambda b, g, i: (b, 2 * ngrp + g)),
            pl.BlockSpec((1, NA_HEADS_PER_STEP, R * W, KR * W), lambda b, g, i: (variant(i), g, 0, 0)),
        ],
        out_specs=pl.BlockSpec((R * W, hw), lambda b, g, i: (b * nblk + i, g)),
        out_shape=jax.ShapeDtypeStruct((B * L, mix_w), BF16),
        compiler_params=_params("parallel", "parallel", "arbitrary"),
        name="neighbourhood_attention",
    )(proj, proj, proj, bias)


def _position_features(L):
    P = DFT_P
    pos = (jnp.arange(L // P, dtype=F32)[None, :] * P + jnp.arange(P, dtype=F32)[:, None]).reshape(L)
    t = pos / (L - 1)
    wpos = 2.0 * math.pi * pos / L
    fb = jnp.linspace(1e-4, FILTER_BANDS - 1, FILTER_BANDS, dtype=F32)
    ang = wpos[:, None] * fb[None, :]
    z = jnp.concatenate([t[:, None], jnp.cos(ang), -jnp.sin(ang)], axis=-1)
    return jnp.pad(z, ((0, 0), (0, 128 - FILTER_EMB)))


def _hy_filter_kernel(z_ref, w1_ref, b1_ref, w2_ref, b2_ref, w3_ref, fr_ref, dec_ref, h_ref, *, mix_w):
    hp = lax.Precision.HIGHEST
    fr = fr_ref[...]
    z = z_ref[...]
    h = jnp.sin(fr * (jnp.dot(z, w1_ref[...], precision=hp, preferred_element_type=F32) + b1_ref[...]))
    h = jnp.sin(fr * (jnp.dot(h, w2_ref[...], precision=hp, preferred_element_type=F32) + b2_ref[...]))

    def taps(d):
        w3 = w3_ref[:, d * mix_w:(d + 1) * mix_w]
        win = jnp.exp(-z[:, 0:1] * jnp.abs(dec_ref[d:d + 1, :]))
        return jnp.dot(h, w3, precision=hp, preferred_element_type=F32) * win

    tl = z_ref.shape[0]
    lag0 = pl.program_id(0) * tl + lax.broadcasted_iota(jnp.int32, (tl, 1), 0) == 0
    h_fwd, h_bwd = taps(0), taps(1)
    h_ref[0] = h_fwd + jnp.where(lag0, h_bwd, 0.0)
    h_ref[1] = jnp.where(lag0, 0.0, h_bwd)


def _hyena_filter_taps(L, w1, b1, w2, b2, w3, freq, decay, tl):
    mix_w = decay.shape[1]
    order = w2.shape[0]
    w1p = jnp.pad(w1.astype(F32), ((0, 128 - FILTER_EMB), (0, 0)))
    full = lambda shape: pl.BlockSpec(shape, lambda i: (0, 0))
    return pl.pallas_call(
        functools.partial(_hy_filter_kernel, mix_w=mix_w),
        grid=(L // tl,),
        in_specs=[
            pl.BlockSpec((tl, 128), lambda i: (i, 0)),
            full((128, order)), full((1, order)), full((order, order)), full((1, order)),
            full((order, 2 * mix_w)), full((1, order)), full((2, mix_w)),
        ],
        out_specs=pl.BlockSpec((2, tl, mix_w), lambda i: (0, i, 0)),
        out_shape=jax.ShapeDtypeStruct((2, L, mix_w), F32),
        compiler_params=_params("parallel"),
        name="hyena_filter_taps",
    )(_position_features(L), w1p, b1.astype(F32)[None], w2.astype(F32), b2.astype(F32)[None],
      w3.astype(F32), freq.astype(F32)[None], decay.astype(F32))


def _dft_tables(L):
    P, MG = DFT_P, DFT_MG
    A, n = L // P, 2 * L
    Mm, J = 2 * A, P // 2
    NG = Mm // MG
    ar = lambda size: jnp.arange(size, dtype=jnp.int32)
    ang1 = (((2 * ar(Mm) + 1)[:, None] * ar(A)[None, :]) % (2 * Mm)).astype(F32) * (math.pi / Mm)
    w1 = jnp.concatenate([jnp.cos(ang1), -jnp.sin(ang1)], axis=0)
    k = (Mm * ar(J)[None, None, :, None] + MG * ar(NG)[:, None, None, None] + ar(MG)[None, :, None, None])
    ph = ((ar(P)[None, None, None, :] * (2 * k + 1)) % (2 * n)).astype(F32) * (math.pi / n)
    cr, ci = jnp.cos(ph), -jnp.sin(ph)
    eye = jnp.eye(MG, dtype=F32)
    blk = lambda c: c[..., None] * eye[None, :, None, None, :]
    top = jnp.stack([blk(cr), blk(-ci)], axis=3)
    bot = jnp.stack([blk(ci), blk(cr)], axis=3)
    a3 = jnp.stack([top, bot], axis=1).reshape(NG, 2 * MG * J, 2 * P * MG)
    return w1.astype(BF16), w1.T.astype(BF16), a3.astype(BF16), a3.transpose(0, 2, 1).astype(BF16)


def _short_conv(ref, w_ref, b_ref, start, size, L):
    halo, tile = 16, 8
    cur = ref[pl.ds(start, size), :].astype(F32)
    up, dn = pltpu.roll(cur, 1, axis=0), pltpu.roll(cur, size - 1, axis=0)
    prev_start = pl.multiple_of(jnp.maximum(start - halo, 0), halo)
    next_start = pl.multiple_of(jnp.minimum(start + size, L - halo), halo)
    prev = ref[pl.ds(prev_start, halo), :][halo - 1:halo, :].astype(F32)
    nxt = ref[pl.ds(next_start, halo), :][0:1, :].astype(F32)
    prev = jnp.where(start == 0, 0.0, prev)
    nxt = jnp.where(start + size == L, 0.0, nxt)
    row = lax.broadcasted_iota(jnp.int32, (tile, 1), 0)
    up = jnp.concatenate([jnp.where(row == 0, prev, up[:tile]), up[tile:]], axis=0)
    dn = jnp.concatenate([dn[:size - tile], jnp.where(row == tile - 1, nxt, dn[size - tile:])], axis=0)
    return up * w_ref[0:1, :] + cur * w_ref[1:2, :] + dn * w_ref[2:3, :] + b_ref[...]


def _hy_gate_kernel(x0_ref, x1_ref, v_ref, w0_ref, w1_ref, wv_ref, b0_ref, b1_ref, bv_ref,
                    perm_ref, x0c_ref, zb_ref, zp_ref):
    P, MG = DFT_P, DFT_MG
    chunk = P * MG
    L, cb = x0_ref.shape

    def body(c, carry):
        start = pl.multiple_of(c * chunk, chunk)
        rows = pl.ds(start, chunk)
        conv = functools.partial(_short_conv, start=start, size=chunk, L=L)
        x0c_ref[rows, :] = conv(x0_ref, w0_ref, b0_ref).astype(x0c_ref.dtype)
        x1c = conv(x1_ref, w1_ref, b1_ref)
        vc = conv(v_ref, wv_ref, bv_ref)
        z = (x1c * vc).astype(BF16)
        zb_ref[rows, :] = z
        zp = _dot(perm_ref[...], z).reshape(P, MG, cb)
        zp_ref[:, pl.ds(pl.multiple_of(c * MG, MG), MG), :] = zp
        return carry

    lax.fori_loop(0, L // chunk, body, 0, unroll=2)


def _time_split_permutation():
    P, MG = DFT_P, DFT_MG
    r = np.arange(P * MG)
    perm = np.zeros((P * MG, P * MG), np.float32)
    perm[r, (r % MG) * P + r // MG] = 1.0
    return perm


def _hyena_gate(proj, short_w, short_b, B, L, mix_w, cb):
    nc = mix_w // cb
    P = DFT_P
    col = lambda part: pl.BlockSpec((L, cb), lambda b, c: (b, part * nc + c))
    wspec = lambda part: pl.BlockSpec((3, cb), lambda b, c: (0, part * nc + c))
    bspec = lambda part: pl.BlockSpec((1, cb), lambda b, c: (0, part * nc + c))
    perm = jnp.asarray(_time_split_permutation(), BF16)
    seq = pl.BlockSpec((None, None, L, cb), lambda b, c: (b, c, 0, 0))
    return pl.pallas_call(
        _hy_gate_kernel,
        grid=(B, nc),
        in_specs=[col(0), col(1), col(2), wspec(0), wspec(1), wspec(2), bspec(0), bspec(1), bspec(2),
                  pl.BlockSpec(perm.shape, lambda b, c: (0, 0))],
        out_specs=[seq, seq, pl.BlockSpec((None, None, P, L // P, cb), lambda b, c: (b, c, 0, 0, 0))],
        out_shape=[jax.ShapeDtypeStruct((B, nc, L, cb), BF16), jax.ShapeDtypeStruct((B, nc, L, cb), BF16),
                   jax.ShapeDtypeStruct((B, nc, P, L // P, cb), F32)],
        compiler_params=_params("parallel", "parallel"),
        name="hyena_gate",
    )(proj, proj, proj, short_w, short_w, short_w, short_b, short_b, short_b, perm)


def _dft_fwd_kernel(*refs, filtered):
    if filtered:
        z_ref, w1_ref, a3_ref, g_ref, y_ref, v_ref = refs
    else:
        z_ref, w1_ref, a3_ref, y_ref, v_ref = refs
    P, MG = DFT_P, DFT_MG
    cb = z_ref.shape[2]
    ng, rows_g = a3_ref.shape[0], a3_ref.shape[1]
    half = rows_g // 2
    w1 = w1_ref[...]
    for b in range(P):
        v = _dot(w1, z_ref[b].astype(BF16))
        v_ref[:, :, b * MG:(b + 1) * MG, :] = v.reshape(2, ng, MG, cb)

    if not filtered:
        backward = pl.program_id(1) == 1
        scale = 2.0 / y_ref.shape[0]
        im_scale = jnp.where(backward, -scale, scale)

        @pl.when(jnp.logical_not(backward))
        def _():
            y_ref[...] = jnp.zeros_like(y_ref)

    def group(g, carry):
        vg = jnp.concatenate([v_ref[0, g], v_ref[1, g]], axis=0).astype(BF16)
        x = _dot(a3_ref[g], vg)
        rows = pl.ds(pl.multiple_of(g * rows_g, rows_g), rows_g)
        xr, xi = x[:half], x[half:]
        if filtered:
            gr, gi = g_ref[g, :half], g_ref[g, half:]
            y_ref[rows, :] = jnp.concatenate([xr * gr - xi * gi, xr * gi + xi * gr], axis=0).astype(y_ref.dtype)
        else:
            y_ref[rows, :] += jnp.concatenate([xr * scale, xi * im_scale], axis=0)
        return carry

    lax.fori_loop(0, ng, group, 0, unroll=8)


def _dft_forward(z, w1, a3, g, Bz, L, cb, out_dtype):
    ng, rows_g, cols_g = a3.shape
    const = lambda shape: pl.BlockSpec(shape, lambda c, b: (0,) * len(shape), pipeline_mode=pl.Buffered(1))
    P, A = DFT_P, L // DFT_P
    if g is not None:
        nc = z.shape[1]
        in_specs = [pl.BlockSpec((None, None, P, A, cb), lambda c, b: (b, c, 0, 0, 0)), const(w1.shape),
                    const(a3.shape),
                    pl.BlockSpec((ng, rows_g, cb), lambda c, b: (0, 0, c), pipeline_mode=pl.Buffered(1))]
        args = [z, w1, a3, g]
        out_spec = pl.BlockSpec((None, None, 2 * L, cb), lambda c, b: (b, c, 0, 0))
        out_shape = (Bz, nc, 2 * L, cb)
    else:
        assert Bz == 2
        nc = z.shape[3] // cb
        in_specs = [pl.BlockSpec((None, P, A, cb), lambda c, b: (b, 0, 0, c)), const(w1.shape), const(a3.shape)]
        args = [z, w1, a3]
        out_spec = pl.BlockSpec((None, 2 * L, cb), lambda c, b: (0, 0, c))
        out_shape = (1, 2 * L, z.shape[3])
    return pl.pallas_call(
        functools.partial(_dft_fwd_kernel, filtered=g is not None),
        grid=(nc, Bz),
        in_specs=in_specs,
        out_specs=out_spec,
        out_shape=jax.ShapeDtypeStruct(out_shape, out_dtype),
        scratch_shapes=[pltpu.VMEM((2, ng, cols_g // 2, cb), F32)],
        compiler_params=_params("parallel", "arbitrary"),
        name="hyena_dft_forward",
    )(*args)


def _dft_inv_kernel(y_ref, b3_ref, w1t_ref, zb_ref, x0_ref, d_ref, unperm_ref, o_ref, q_ref, t_ref):
    P, MG = DFT_P, DFT_MG
    chunk = P * MG
    cb = y_ref.shape[1]
    ng, rows_g, cols_g = b3_ref.shape
    half = rows_g // 2

    def group(g, carry):
        yg = y_ref[pl.ds(pl.multiple_of(g * cols_g, cols_g), cols_g), :]
        q = _dot(b3_ref[g], yg)
        q_ref[0, g] = q[:half]
        q_ref[1, g] = q[half:]
        return carry

    lax.fori_loop(0, ng, group, 0, unroll=8)
    w1t = w1t_ref[...]
    for b in range(P):
        qb = q_ref[:, :, b * MG:(b + 1) * MG, :].reshape(2 * ng * MG, cb)
        t_ref[b] = _dot(w1t, qb.astype(BF16))

    unperm = unperm_ref[...]

    def gate(c, carry):
        rows = pl.ds(pl.multiple_of(c * chunk, chunk), chunk)
        yp = t_ref[:, pl.ds(pl.multiple_of(c * MG, MG), MG), :].reshape(chunk, cb)
        hi = yp.astype(BF16)
        lo = (yp - hi.astype(F32)).astype(BF16)
        y = _dot(unperm, hi) + _dot(unperm, lo)
        y = y + d_ref[...] * zb_ref[rows, :].astype(F32)
        o_ref[rows, :] = (x0_ref[rows, :].astype(F32) * y).astype(o_ref.dtype)
        return carry

    lax.fori_loop(0, zb_ref.shape[0] // chunk, gate, 0, unroll=4)


def _dft_inverse(y, b3, w1t, zb, x0c, d, B, L, cb):
    nc = zb.shape[1]
    ng, rows_g, _ = b3.shape
    const = lambda shape: pl.BlockSpec(shape, lambda c, b: (0,) * len(shape), pipeline_mode=pl.Buffered(1))
    blocked = lambda rows: pl.BlockSpec((None, None, rows, cb), lambda c, b: (b, c, 0, 0))
    unperm = jnp.asarray(_time_split_permutation().T, BF16)
    return pl.pallas_call(
        _dft_inv_kernel,
        grid=(nc, B),
        in_specs=[blocked(2 * L), const(b3.shape), const(w1t.shape), blocked(L), blocked(L),
                  pl.BlockSpec((1, cb), lambda c, b: (0, c)), const(unperm.shape)],
        out_specs=pl.BlockSpec((L, cb), lambda c, b: (b, c)),
        out_shape=jax.ShapeDtypeStruct((B * L, nc * cb), BF16),
        scratch_shapes=[pltpu.VMEM((2, ng, rows_g // 2, cb), F32),
                        pltpu.VMEM((DFT_P, L // DFT_P, cb), F32)],
        compiler_params=_params("parallel", "arbitrary"),
        name="hyena_dft_inverse",
    )(y, b3, w1t, zb, x0c, d, unperm)


def _hyena(proj, B, L, mix_w, short_w, short_b, w1, b1, w2, b2, w3, freq, decay, d_bias):
    cb = min(512, HYENA_LANE_ELEMS // L)
    dw1, dw1t, a3, b3 = _dft_tables(L)
    ng, rows_g, _ = a3.shape
    taps = _hyena_filter_taps(L, w1, b1, w2, b2, w3, freq, decay, tl=min(512, L))
    g = _dft_forward(taps.reshape(2, DFT_P, L // DFT_P, mix_w), dw1, a3, None, 2, L, cb, F32)
    g = g.reshape(ng, rows_g, mix_w)
    x0c, zb, zp = _hyena_gate(proj, short_w.astype(F32), short_b.astype(F32)[None], B, L, mix_w, cb)
    y = _dft_forward(zp, dw1, a3, g, B, L, cb, BF16)
    return _dft_inverse(y, b3, dw1t, zb, x0c, d_bias.astype(F32)[None], B, L, cb)


def _mem_kv_kernel(m_ref, g_ref, w_ref, o_ref):
    h = _rmsnorm(m_ref[...], g_ref[...]).astype(BF16)
    o_ref[...] = _dot(h, w_ref[...]).astype(o_ref.dtype)


def _mem_kv(mem, g, w, layer):
    BM, D = mem.shape
    N = w.shape[2]
    M = 256
    return pl.pallas_call(
        _mem_kv_kernel,
        grid=(BM // M,),
        in_specs=[
            pl.BlockSpec((M, D), lambda b: (b, 0)),
            pl.BlockSpec((1, D), lambda b: (0, 0)),
            pl.BlockSpec((None, D, N), lambda b: (layer, 0, 0)),
        ],
        out_specs=pl.BlockSpec((M, N), lambda b: (b, 0)),
        out_shape=jax.ShapeDtypeStruct((BM, N), BF16),
        compiler_params=_params("parallel"),
        name="memory_kv",
    )(mem, g, w)


def _mix_out_kernel(tok_ref, q_ref, kv_ref, w_ref, x_ref, g_ref, o_ref, mo_ref):
    mem_w = q_ref.shape[1]
    mix_w = tok_ref.shape[1]
    heads = [slice(h * HEAD_DIM, (h + 1) * HEAD_DIM) for h in range(mem_w // HEAD_DIM)]
    scores = [_dot_nt(q_ref[:, c], kv_ref[:, c]) for c in heads]
    o = _dot(tok_ref[...], w_ref[:mix_w, :])
    probs = []
    for s in scores:
        p = jnp.exp2(s - jnp.max(s, axis=-1, keepdims=True))
        probs.append((p.astype(BF16), jnp.sum(p, axis=-1, keepdims=True)))
    for c, (p, l) in zip(heads, probs):
        vcols = slice(mem_w + c.start, mem_w + c.stop)
        mo_ref[:, c] = (_dot(p, kv_ref[:, vcols]) / l).astype(mo_ref.dtype)
    o = o + _dot(mo_ref[...], w_ref[mix_w:, :])
    o_ref[...] = x_ref[...] + _rmsnorm(o, g_ref[...])


def _mix_out(tok, proj, kv, w_out, layer, x, g, L, tm):
    T, D = x.shape
    mix_w = tok.shape[1]
    mem_w = w_out.shape[1] - mix_w
    M = kv.shape[0] // (T // L)
    qblk = (proj.shape[1] - mem_w) // mem_w
    per_b = L // tm
    return pl.pallas_call(
        _mix_out_kernel,
        grid=(T // tm,),
        in_specs=[
            pl.BlockSpec((tm, mix_w), lambda i: (i, 0)),
            pl.BlockSpec((tm, mem_w), lambda i: (i, qblk)),
            pl.BlockSpec((M, 2 * mem_w), lambda i: (i // per_b, 0)),
            pl.BlockSpec((None, mix_w + mem_w, D), lambda i: (layer, 0, 0)),
            pl.BlockSpec((tm, D), lambda i: (i, 0)),
            pl.BlockSpec((1, D), lambda i: (0, 0)),
        ],
        out_specs=pl.BlockSpec((tm, D), lambda i: (i, 0)),
        out_shape=jax.ShapeDtypeStruct((T, D), F32),
        scratch_shapes=[pltpu.VMEM((tm, mem_w), BF16)],
        compiler_params=_params("parallel"),
        name="mix_out_proj",
    )(tok, proj, kv, w_out, x, g)


def _mlp_kernel(xp_ref, xn_ref, gpre_ref, wu_ref, wd_ref, gpost_ref, o_ref, h0_ref, h1_ref, acc0_ref, acc1_ref,
                *, nt, nf):
    s, f = pl.program_id(0), pl.program_id(1)
    tm = xp_ref.shape[0]
    part = tm // nf
    rows = pl.ds(pl.multiple_of(f * part, part), part)
    gpre, gpost = gpre_ref[...], gpost_ref[...]

    @pl.when((s == 0) & (f == 0))
    def _():
        for r in range(0, tm, NORM_ROWS):
            blk = slice(r, r + NORM_ROWS)
            h0_ref[blk, :] = _rmsnorm(xp_ref[blk, :], gpre).astype(h0_ref.dtype)
        acc0_ref[...] = jnp.zeros_like(acc0_ref)
        acc1_ref[...] = jnp.zeros_like(acc1_ref)

    def finish_previous(acc_prev):
        o_ref[rows, :] = xp_ref[rows, :] + _rmsnorm(acc_prev[rows, :], gpost)

    def step(h_cur, h_nxt, acc_cur, acc_prev):
        h = h_cur[...]
        cw = wu_ref.shape[1] // MLP_CHUNKS
        chunks = [slice(c * cw, (c + 1) * cw) for c in range(MLP_CHUNKS)]
        ups = [_dot(h, wu_ref[:, c]) for c in chunks]
        acts = [jnp.square(jnp.maximum(u, 0.0)).astype(BF16) for u in ups]
        d = jnp.where(f == 0, 0.0, acc_cur[...])
        for a, c in zip(acts, chunks):
            d = d + _dot(a, wd_ref[c, :])
        acc_cur[...] = d
        h_nxt[rows, :] = _rmsnorm(xn_ref[rows, :], gpre).astype(h_nxt.dtype)
        if acc_prev is not None:
            finish_previous(acc_prev)

    even = s % 2 == 0

    @pl.when(s == 0)
    def _():
        step(h0_ref, h1_ref, acc0_ref, None)

    @pl.when((s > 0) & (s < nt) & even)
    def _():
        step(h0_ref, h1_ref, acc0_ref, acc1_ref)

    @pl.when((s < nt) & jnp.logical_not(even))
    def _():
        step(h1_ref, h0_ref, acc1_ref, acc0_ref)

    @pl.when(s == nt)
    def _():
        finish_previous(acc1_ref if nt % 2 == 0 else acc0_ref)


def _mlp(x, gpre, w_up, w_down, layer, gpost, tm, tf):
    T, D = x.shape
    F = w_up.shape[2]
    nt, nf = T // tm, F // tf
    prev = lambda s, f: (jnp.maximum(s - 1, 0), 0)
    last_f = lambda s, f: jnp.where(s == nt, nf - 1, f)
    return pl.pallas_call(
        functools.partial(_mlp_kernel, nt=nt, nf=nf),
        grid=(nt + 1, nf),
        in_specs=[
            pl.BlockSpec((tm, D), prev),
            pl.BlockSpec((tm, D), lambda s, f: (jnp.minimum(s + 1, nt - 1), 0)),
            pl.BlockSpec((1, D), lambda s, f: (0, 0)),
            pl.BlockSpec((None, D, tf), lambda s, f: (layer, 0, last_f(s, f))),
            pl.BlockSpec((None, tf, D), lambda s, f: (layer, last_f(s, f), 0)),
            pl.BlockSpec((1, D), lambda s, f: (0, 0)),
        ],
        out_specs=pl.BlockSpec((tm, D), prev),
        out_shape=jax.ShapeDtypeStruct((T, D), F32),
        scratch_shapes=[pltpu.VMEM((tm, D), BF16), pltpu.VMEM((tm, D), BF16),
                        pltpu.VMEM((tm, D), F32), pltpu.VMEM((tm, D), F32)],
        compiler_params=pltpu.CompilerParams(dimension_semantics=("arbitrary", "arbitrary"),
                                             vmem_limit_bytes=MLP_VMEM_LIMIT_BYTES),
        name="relu2_mlp",
    )(x, x, gpre, w_up, w_down, gpost)


def _trunk(x, mem, wts):
    (g_mix_pre, w_in, na_rpb, hy_short_w, hy_short_b, hy_w1, hy_b1, hy_w2, hy_b2, hy_w3, hy_freq,
     hy_decay, hy_d, g_mem, w_mem_kv, w_out, g_mix_post, g_mlp_pre, w_up, w_down, g_mlp_post) = wts
    B, L, D = x.shape
    depth = w_in.shape[0]
    in_w = w_in.shape[2]
    mem_w = w_mem_kv.shape[2] // 2
    mix_w = (in_w - mem_w) // 3
    x = x.reshape(B * L, D)
    mem = mem.reshape(-1, D)
    qk_scale = HEAD_DIM ** -0.5 * LOG2E
    row = lambda v: v.astype(F32)[None]
    for i in range(depth):
        j = i // 2
        na_layer = i % 2 == 0
        colscale = jnp.concatenate([
            jnp.full((mix_w,), qk_scale if na_layer else 1.0, F32),
            jnp.ones((2 * mix_w,), F32),
            jnp.full((mem_w,), qk_scale, F32)])[None]
        proj = _norm_matmul(x, row(g_mix_pre[i]), w_in, i, colscale, tm=min(1024, L), tn=1280)
        if na_layer:
            bias = _na_bias_table(na_rpb[j], L // GRID_W)
            tok = _neighbourhood_attention(proj, bias, B, L, mix_w)
        else:
            tok = _hyena(proj, B, L, mix_w, hy_short_w[j], hy_short_b[j], hy_w1[j], hy_b1[j], hy_w2[j],
                         hy_b2[j], hy_w3[j], hy_freq[j], hy_decay[j], hy_d[j])
        kv = _mem_kv(mem, row(g_mem[i]), w_mem_kv, i)
        x = _mix_out(tok, proj, kv, w_out, i, x, row(g_mix_post[i]), L, tm=min(512, L))
        x = _mlp(x, row(g_mlp_pre[i]), w_up, w_down, i, row(g_mlp_post[i]), tm=min(512, L), tf=1024)
    return x.reshape(B, L, D)


def kernel(x_prompt, x_sample, mem_prompt, mem_sample, g_mix_pre, w_in, na_rpb, hy_short_w, hy_short_b,
           hy_w1, hy_b1, hy_w2, hy_b2, hy_w3, hy_freq, hy_decay, hy_d, g_mem, w_mem_kv, w_out,
           g_mix_post, g_mlp_pre, w_up, w_down, g_mlp_post):
    bf = lambda w: w.astype(BF16)
    wts = (g_mix_pre, bf(w_in), na_rpb, hy_short_w, hy_short_b, hy_w1, hy_b1, hy_w2, hy_b2, hy_w3, hy_freq,
           hy_decay, hy_d, g_mem, bf(w_mem_kv), bf(w_out), g_mix_post, g_mlp_pre, bf(w_up), bf(w_down), g_mlp_post)
    return (_trunk(x_prompt, mem_prompt, wts), _trunk(x_sample, mem_sample, wts))
```

```python
import functools
import math
from typing import NamedTuple

import numpy as np
import jax
import jax.numpy as jnp
from jax import lax
from jax.experimental import pallas as pl
from jax.experimental.pallas import tpu as pltpu

F32 = jnp.float32
BF16 = jnp.bfloat16

GRID_W = 64
HEAD_DIM = 128
NA_MAX_KH = 8
NA_KW = 16
FILTER_EMB = 33
FILTER_BANDS = (FILTER_EMB - 1) // 2
RMS_EPS = 1e-6
NEG_BIG = -1e30
LOG2E = math.log2(math.e)

VMEM_LIMIT_BYTES = 56 * 1024 * 1024
LANES = 128
SUBLANES_F32 = 8
SUBLANES_BF16 = 16

NA_ROWS_PER_STEP = 4
NA_HEADS_PER_STEP = 6
NA_KEY_ROWS = NA_ROWS_PER_STEP + NA_MAX_KH
MLP_CHUNKS = 2
NORM_ROWS = 64
DFT_P = 32
DFT_MG = SUBLANES_F32
HYENA_LANE_ELEMS = 1 << 20


class _Tiles(NamedTuple):
    proj_rows: int
    proj_cols: int
    mix_rows: int
    mlp_rows: int
    mlp_cols: int
    filter_rows: int
    hyena_cols: int


def _tiles(L):
    return _Tiles(proj_rows=min(1024, L), proj_cols=1280, mix_rows=min(512, L), mlp_rows=min(512, L),
                  mlp_cols=1024, filter_rows=min(512, L), hyena_cols=min(512, HYENA_LANE_ELEMS // L))


def _params(*sem):
    return pltpu.CompilerParams(dimension_semantics=sem, vmem_limit_bytes=VMEM_LIMIT_BYTES)


def _rmsnorm(x, g):
    ms = jnp.mean(x * x, axis=-1, keepdims=True)
    return x * lax.rsqrt(ms + RMS_EPS) * g


def _dot(a, b):
    return jnp.dot(a, b, preferred_element_type=F32)


def _dot_nt(a, b):
    return lax.dot_general(a, b, (((1,), (1,)), ((), ())), preferred_element_type=F32)


def _norm_matmul_kernel(x_ref, g_ref, w_ref, s_ref, o_ref, h_ref):
    @pl.when(pl.program_id(1) == 0)
    def _():
        g = g_ref[...]
        for r in range(0, h_ref.shape[0], NORM_ROWS):
            rows = slice(r, r + NORM_ROWS)
            h_ref[rows, :] = _rmsnorm(x_ref[rows, :], g).astype(h_ref.dtype)

    o_ref[...] = (_dot(h_ref[...], w_ref[...]) * s_ref[...]).astype(o_ref.dtype)


def _norm_matmul(x, g, w, layer, colscale, tm, tn):
    T, D = x.shape
    N = w.shape[2]
    return pl.pallas_call(
        _norm_matmul_kernel,
        grid=(T // tm, N // tn),
        in_specs=[
            pl.BlockSpec((tm, D), lambda i, j: (i, 0)),
            pl.BlockSpec((1, D), lambda i, j: (0, 0)),
            pl.BlockSpec((None, D, tn), lambda i, j: (layer, 0, j)),
            pl.BlockSpec((1, tn), lambda i, j: (0, j)),
        ],
        out_specs=pl.BlockSpec((tm, tn), lambda i, j: (i, j)),
        out_shape=jax.ShapeDtypeStruct((T, N), BF16),
        scratch_shapes=[pltpu.VMEM((tm, D), BF16)],
        compiler_params=_params("parallel", "arbitrary"),
        name="norm_in_proj",
    )(x, g, w, colscale)


def _na_bias_table(rpb, rows):
    R, KR, W = NA_ROWS_PER_STEP, NA_KEY_ROWS, GRID_W
    kh = min(NA_MAX_KH, rows)
    nblk = rows // R
    qc = np.arange(W)
    cs = np.clip(qc - NA_KW // 2, 0, W - NA_KW)
    kc = np.arange(W)
    col_ok = (kc[None, :] >= cs[:, None]) & (kc[None, :] < cs[:, None] + NA_KW)
    dr_all, ok_all = [], []
    for blk in (0, min(1, nblk - 1), nblk - 1):
        ks = int(np.clip(blk * R - kh // 2, 0, rows - KR))
        r = blk * R + np.arange(R)
        rs = np.clip(r - kh // 2, 0, rows - kh)
        key = ks + np.arange(KR)
        row_ok = (key[None, :] >= rs[:, None]) & (key[None, :] < rs[:, None] + kh)
        dr = np.clip(key[None, :] - r[:, None] + NA_MAX_KH - 1, 0, 2 * NA_MAX_KH - 2)
        dr_all.append(dr)
        ok_all.append(row_ok)
    pad = W - NA_KW
    padded = jnp.pad(rpb.astype(F32) * LOG2E, ((0, 0), (0, 0), (pad, pad)))
    per_dr = jnp.stack([padded[:, :, W - 1 - q:2 * W - 1 - q] for q in range(W)], axis=2)
    per_dr = jnp.where(col_ok, per_dr, NEG_BIG)
    H = rpb.shape[0]

    def assemble(p_ref, o_ref):
        masked = jnp.full((W, W), NEG_BIG, F32)
        block = lambda v, r, k: p_ref[0, int(dr_all[v][r, k])] if ok_all[v][r, k] else masked
        for v in range(3):
            for r in range(R):
                for k in range(0, KR, 2):
                    o_ref[v, 0, r * W:(r + 1) * W, k * W:(k + 2) * W] = jnp.concatenate(
                        [block(v, r, k), block(v, r, k + 1)], axis=1)

    return pl.pallas_call(
        assemble,
        grid=(H,),
        in_specs=[pl.BlockSpec((1,) + per_dr.shape[1:], lambda h: (h, 0, 0, 0))],
        out_specs=pl.BlockSpec((3, 1, R * W, KR * W), lambda h: (0, h, 0, 0)),
        out_shape=jax.ShapeDtypeStruct((3, H, R * W, KR * W), F32),
        compiler_params=_params("parallel"),
        name="na_bias_table",
    )(per_dr)


def _na_kernel(q_ref, k_ref, v_ref, b_ref, o_ref, *, rows):
    R, KR, W = NA_ROWS_PER_STEP, NA_KEY_ROWS, GRID_W
    i = pl.program_id(2)
    ks = jnp.clip(i * R - min(NA_MAX_KH, rows) // 2, 0, rows - KR) * W
    ks = pl.multiple_of(ks, W)
    heads = [slice(h * HEAD_DIM, (h + 1) * HEAD_DIM) for h in range(NA_HEADS_PER_STEP)]
    keys = pl.ds(ks, KR * W)
    scores = [_dot_nt(q_ref[:, c], k_ref[keys, c]) + b_ref[0, h] for h, c in enumerate(heads)]
    probs = []
    for s in scores:
        p = jnp.exp2(s - jnp.max(s, axis=-1, keepdims=True))
        probs.append((p.astype(BF16), jnp.sum(p, axis=-1, keepdims=True)))
    for c, (p, l) in zip(heads, probs):
        o_ref[:, c] = (_dot(p, v_ref[keys, c]) / l).astype(o_ref.dtype)


def _neighbourhood_attention(proj, bias, B, L, mix_w):
    R, KR, W = NA_ROWS_PER_STEP, NA_KEY_ROWS, GRID_W
    rows = L // W
    nblk = rows // R
    hw = NA_HEADS_PER_STEP * HEAD_DIM
    ngrp = mix_w // hw
    assert rows % R == 0 and rows >= KR and nblk >= 2 and R >= NA_MAX_KH // 2

    def variant(i):
        return jnp.where(i == 0, 0, jnp.where(i == nblk - 1, 2, 1))

    return pl.pallas_call(
        functools.partial(_na_kernel, rows=rows),
        grid=(B, ngrp, nblk),
        in_specs=[
            pl.BlockSpec((R * W, hw), lambda b, g, i: (b * nblk + i, g)),
            pl.BlockSpec((L, hw), lambda b, g, i: (b, ngrp + g)),
            pl.BlockSpec((L, hw), lambda b, g, i: (b, 2 * ngrp + g)),
            pl.BlockSpec((1, NA_HEADS_PER_STEP, R * W, KR * W), lambda b, g, i: (variant(i), g, 0, 0)),
        ],
        out_specs=pl.BlockSpec((R * W, hw), lambda b, g, i: (b * nblk + i, g)),
        out_shape=jax.ShapeDtypeStruct((B * L, mix_w), BF16),
        compiler_params=_params("parallel", "parallel", "arbitrary"),
        name="neighbourhood_attention",
    )(proj, proj, proj, bias)


def _position_features(L):
    P = DFT_P
    pos = (jnp.arange(L // P, dtype=F32)[None, :] * P + jnp.arange(P, dtype=F32)[:, None]).reshape(L)
    t = pos / (L - 1)
    wpos = 2.0 * math.pi * pos / L
    fb = jnp.linspace(1e-4, FILTER_BANDS - 1, FILTER_BANDS, dtype=F32)
    ang = wpos[:, None] * fb[None, :]
    z = jnp.concatenate([t[:, None], jnp.cos(ang), -jnp.sin(ang)], axis=-1)
    return jnp.pad(z, ((0, 0), (0, LANES - FILTER_EMB)))


def _hy_filter_kernel(z_ref, w1_ref, b1_ref, w2_ref, b2_ref, w3_ref, fr_ref, dec_ref, h_ref, *, mix_w):
    hp = lax.Precision.HIGHEST
    fr = fr_ref[...]
    z = z_ref[...]
    h = jnp.sin(fr * (jnp.dot(z, w1_ref[...], precision=hp, preferred_element_type=F32) + b1_ref[...]))
    h = jnp.sin(fr * (jnp.dot(h, w2_ref[...], precision=hp, preferred_element_type=F32) + b2_ref[...]))

    def taps(d):
        w3 = w3_ref[:, d * mix_w:(d + 1) * mix_w]
        win = jnp.exp(-z[:, 0:1] * jnp.abs(dec_ref[d:d + 1, :]))
        return jnp.dot(h, w3, precision=hp, preferred_element_type=F32) * win

    tl = z_ref.shape[0]
    lag0 = pl.program_id(0) * tl + lax.broadcasted_iota(jnp.int32, (tl, 1), 0) == 0
    h_fwd, h_bwd = taps(0), taps(1)
    h_ref[0] = h_fwd + jnp.where(lag0, h_bwd, 0.0)
    h_ref[1] = jnp.where(lag0, 0.0, h_bwd)


def _hyena_filter_taps(L, w1, b1, w2, b2, w3, freq, decay, tl):
    mix_w = decay.shape[1]
    order = w2.shape[0]
    w1p = jnp.pad(w1.astype(F32), ((0, LANES - FILTER_EMB), (0, 0)))
    full = lambda shape: pl.BlockSpec(shape, lambda i: (0, 0))
    return pl.pallas_call(
        functools.partial(_hy_filter_kernel, mix_w=mix_w),
        grid=(L // tl,),
        in_specs=[
            pl.BlockSpec((tl, LANES), lambda i: (i, 0)),
            full((LANES, order)), full((1, order)), full((order, order)), full((1, order)),
            full((order, 2 * mix_w)), full((1, order)), full((2, mix_w)),
        ],
        out_specs=pl.BlockSpec((2, tl, mix_w), lambda i: (0, i, 0)),
        out_shape=jax.ShapeDtypeStruct((2, L, mix_w), F32),
        compiler_params=_params("parallel"),
        name="hyena_filter_taps",
    )(_position_features(L), w1p, b1.astype(F32)[None], w2.astype(F32), b2.astype(F32)[None],
      w3.astype(F32), freq.astype(F32)[None], decay.astype(F32))


def _dft_tables(L):
    P, MG = DFT_P, DFT_MG
    A, n = L // P, 2 * L
    Mm, J = 2 * A, P // 2
    NG = Mm // MG
    ar = lambda size: jnp.arange(size, dtype=jnp.int32)
    ang1 = (((2 * ar(Mm) + 1)[:, None] * ar(A)[None, :]) % (2 * Mm)).astype(F32) * (math.pi / Mm)
    w1 = jnp.concatenate([jnp.cos(ang1), -jnp.sin(ang1)], axis=0)
    k = (Mm * ar(J)[None, None, :, None] + MG * ar(NG)[:, None, None, None] + ar(MG)[None, :, None, None])
    ph = ((ar(P)[None, None, None, :] * (2 * k + 1)) % (2 * n)).astype(F32) * (math.pi / n)
    cr, ci = jnp.cos(ph), -jnp.sin(ph)
    eye = jnp.eye(MG, dtype=F32)
    blk = lambda c: c[..., None] * eye[None, :, None, None, :]
    top = jnp.stack([blk(cr), blk(-ci)], axis=3)
    bot = jnp.stack([blk(ci), blk(cr)], axis=3)
    a3 = jnp.stack([top, bot], axis=1).reshape(NG, 2 * MG * J, 2 * P * MG)
    return w1.astype(BF16), w1.T.astype(BF16), a3.astype(BF16), a3.transpose(0, 2, 1).astype(BF16)


def _short_conv(ref, w_ref, b_ref, start, size, L):
    halo, tile = SUBLANES_BF16, SUBLANES_F32
    cur = ref[pl.ds(start, size), :].astype(F32)
    up, dn = pltpu.roll(cur, 1, axis=0), pltpu.roll(cur, size - 1, axis=0)
    prev_start = pl.multiple_of(jnp.maximum(start - halo, 0), halo)
    next_start = pl.multiple_of(jnp.minimum(start + size, L - halo), halo)
    prev = ref[pl.ds(prev_start, halo), :][halo - 1:halo, :].astype(F32)
    nxt = ref[pl.ds(next_start, halo), :][0:1, :].astype(F32)
    prev = jnp.where(start == 0, 0.0, prev)
    nxt = jnp.where(start + size == L, 0.0, nxt)
    row = lax.broadcasted_iota(jnp.int32, (tile, 1), 0)
    up = jnp.concatenate([jnp.where(row == 0, prev, up[:tile]), up[tile:]], axis=0)
    dn = jnp.concatenate([dn[:size - tile], jnp.where(row == tile - 1, nxt, dn[size - tile:])], axis=0)
    return up * w_ref[0:1, :] + cur * w_ref[1:2, :] + dn * w_ref[2:3, :] + b_ref[...]


def _hy_gate_kernel(x0_ref, x1_ref, v_ref, w0_ref, w1_ref, wv_ref, b0_ref, b1_ref, bv_ref,
                    perm_ref, x0c_ref, zb_ref, zp_ref):
    P, MG = DFT_P, DFT_MG
    chunk = P * MG
    L, cb = x0_ref.shape

    def body(c, carry):
        start = pl.multiple_of(c * chunk, chunk)
        rows = pl.ds(start, chunk)
        conv = functools.partial(_short_conv, start=start, size=chunk, L=L)
        x0c_ref[rows, :] = conv(x0_ref, w0_ref, b0_ref).astype(x0c_ref.dtype)
        x1c = conv(x1_ref, w1_ref, b1_ref)
        vc = conv(v_ref, wv_ref, bv_ref)
        z = (x1c * vc).astype(BF16)
        zb_ref[rows, :] = z
        zp = _dot(perm_ref[...], z).reshape(P, MG, cb)
        zp_ref[:, pl.ds(pl.multiple_of(c * MG, MG), MG), :] = zp
        return carry

    lax.fori_loop(0, L // chunk, body, 0, unroll=2)


def _time_split_permutation():
    P, MG = DFT_P, DFT_MG
    r = np.arange(P * MG)
    perm = np.zeros((P * MG, P * MG), np.float32)
    perm[r, (r % MG) * P + r // MG] = 1.0
    return perm


def _hyena_gate(proj, short_w, short_b, B, L, mix_w, cb):
    nc = mix_w // cb
    P = DFT_P
    col = lambda part: pl.BlockSpec((L, cb), lambda b, c: (b, part * nc + c))
    wspec = lambda part: pl.BlockSpec((3, cb), lambda b, c: (0, part * nc + c))
    bspec = lambda part: pl.BlockSpec((1, cb), lambda b, c: (0, part * nc + c))
    perm = jnp.asarray(_time_split_permutation(), BF16)
    seq = pl.BlockSpec((None, None, L, cb), lambda b, c: (b, c, 0, 0))
    return pl.pallas_call(
        _hy_gate_kernel,
        grid=(B, nc),
        in_specs=[col(0), col(1), col(2), wspec(0), wspec(1), wspec(2), bspec(0), bspec(1), bspec(2),
                  pl.BlockSpec(perm.shape, lambda b, c: (0, 0))],
        out_specs=[seq, seq, pl.BlockSpec((None, None, P, L // P, cb), lambda b, c: (b, c, 0, 0, 0))],
        out_shape=[jax.ShapeDtypeStruct((B, nc, L, cb), BF16), jax.ShapeDtypeStruct((B, nc, L, cb), BF16),
                   jax.ShapeDtypeStruct((B, nc, P, L // P, cb), F32)],
        compiler_params=_params("parallel", "parallel"),
        name="hyena_gate",
    )(proj, proj, proj, short_w, short_w, short_w, short_b, short_b, short_b, perm)


def _dft_fwd_kernel(*refs, filtered):
    if filtered:
        z_ref, w1_ref, a3_ref, g_ref, y_ref, v_ref = refs
    else:
        z_ref, w1_ref, a3_ref, y_ref, v_ref = refs
    P, MG = DFT_P, DFT_MG
    cb = z_ref.shape[2]
    ng, rows_g = a3_ref.shape[0], a3_ref.shape[1]
    half = rows_g // 2
    w1 = w1_ref[...]
    for b in range(P):
        v = _dot(w1, z_ref[b].astype(BF16))
        v_ref[:, :, b * MG:(b + 1) * MG, :] = v.reshape(2, ng, MG, cb)

    if not filtered:
        backward = pl.program_id(1) == 1
        scale = 2.0 / y_ref.shape[0]
        im_scale = jnp.where(backward, -scale, scale)

        @pl.when(jnp.logical_not(backward))
        def _():
            y_ref[...] = jnp.zeros_like(y_ref)

    def group(g, carry):
        vg = jnp.concatenate([v_ref[0, g], v_ref[1, g]], axis=0).astype(BF16)
        x = _dot(a3_ref[g], vg)
        rows = pl.ds(pl.multiple_of(g * rows_g, rows_g), rows_g)
        xr, xi = x[:half], x[half:]
        if filtered:
            gr, gi = g_ref[g, :half], g_ref[g, half:]
            y_ref[rows, :] = jnp.concatenate([xr * gr - xi * gi, xr * gi + xi * gr], axis=0).astype(y_ref.dtype)
        else:
            y_ref[rows, :] += jnp.concatenate([xr * scale, xi * im_scale], axis=0)
        return carry

    lax.fori_loop(0, ng, group, 0, unroll=8)


def _dft_forward(z, w1, a3, g, Bz, L, cb, out_dtype):
    ng, rows_g, cols_g = a3.shape
    const = lambda shape: pl.BlockSpec(shape, lambda c, b: (0,) * len(shape), pipeline_mode=pl.Buffered(1))
    P, A = DFT_P, L // DFT_P
    if g is not None:
        nc = z.shape[1]
        in_specs = [pl.BlockSpec((None, None, P, A, cb), lambda c, b: (b, c, 0, 0, 0)), const(w1.shape),
                    const(a3.shape),
                    pl.BlockSpec((ng, rows_g, cb), lambda c, b: (0, 0, c), pipeline_mode=pl.Buffered(1))]
        args = [z, w1, a3, g]
        out_spec = pl.BlockSpec((None, None, 2 * L, cb), lambda c, b: (b, c, 0, 0))
        out_shape = (Bz, nc, 2 * L, cb)
    else:
        assert Bz == 2
        nc = z.shape[3] // cb
        in_specs = [pl.BlockSpec((None, P, A, cb), lambda c, b: (b, 0, 0, c)), const(w1.shape), const(a3.shape)]
        args = [z, w1, a3]
        out_spec = pl.BlockSpec((None, 2 * L, cb), lambda c, b: (0, 0, c))
        out_shape = (1, 2 * L, z.shape[3])
    return pl.pallas_call(
        functools.partial(_dft_fwd_kernel, filtered=g is not None),
        grid=(nc, Bz),
        in_specs=in_specs,
        out_specs=out_spec,
        out_shape=jax.ShapeDtypeStruct(out_shape, out_dtype),
        scratch_shapes=[pltpu.VMEM((2, ng, cols_g // 2, cb), F32)],
        compiler_params=_params("parallel", "arbitrary"),
        name="hyena_dft_forward",
    )(*args)


def _dft_inv_kernel(y_ref, b3_ref, w1t_ref, zb_ref, x0_ref, d_ref, unperm_ref, o_ref, q_ref, t_ref):
    P, MG = DFT_P, DFT_MG
    chunk = P * MG
    cb = y_ref.shape[1]
    ng, rows_g, cols_g = b3_ref.shape
    half = rows_g // 2

    def group(g, carry):
        yg = y_ref[pl.ds(pl.multiple_of(g * cols_g, cols_g), cols_g), :]
        q = _dot(b3_ref[g], yg)
        q_ref[0, g] = q[:half]
        q_ref[1, g] = q[half:]
        return carry

    lax.fori_loop(0, ng, group, 0, unroll=8)
    w1t = w1t_ref[...]
    for b in range(P):
        qb = q_ref[:, :, b * MG:(b + 1) * MG, :].reshape(2 * ng * MG, cb)
        t_ref[b] = _dot(w1t, qb.astype(BF16))

    unperm = unperm_ref[...]

    def gate(c, carry):
        rows = pl.ds(pl.multiple_of(c * chunk, chunk), chunk)
        yp = t_ref[:, pl.ds(pl.multiple_of(c * MG, MG), MG), :].reshape(chunk, cb)
        hi = yp.astype(BF16)
        lo = (yp - hi.astype(F32)).astype(BF16)
        y = _dot(unperm, hi) + _dot(unperm, lo)
        y = y + d_ref[...] * zb_ref[rows, :].astype(F32)
        o_ref[rows, :] = (x0_ref[rows, :].astype(F32) * y).astype(o_ref.dtype)
        return carry

    lax.fori_loop(0, zb_ref.shape[0] // chunk, gate, 0, unroll=4)


def _dft_inverse(y, b3, w1t, zb, x0c, d, B, L, cb):
    nc = zb.shape[1]
    ng, rows_g, _ = b3.shape
    const = lambda shape: pl.BlockSpec(shape, lambda c, b: (0,) * len(shape), pipeline_mode=pl.Buffered(1))
    blocked = lambda rows: pl.BlockSpec((None, None, rows, cb), lambda c, b: (b, c, 0, 0))
    unperm = jnp.asarray(_time_split_permutation().T, BF16)
    return pl.pallas_call(
        _dft_inv_kernel,
        grid=(nc, B),
        in_specs=[blocked(2 * L), const(b3.shape), const(w1t.shape), blocked(L), blocked(L),
                  pl.BlockSpec((1, cb), lambda c, b: (0, c)), const(unperm.shape)],
        out_specs=pl.BlockSpec((L, cb), lambda c, b: (b, c)),
        out_shape=jax.ShapeDtypeStruct((B * L, nc * cb), BF16),
        scratch_shapes=[pltpu.VMEM((2, ng, rows_g // 2, cb), F32),
                        pltpu.VMEM((DFT_P, L // DFT_P, cb), F32)],
        compiler_params=_params("parallel", "arbitrary"),
        name="hyena_dft_inverse",
    )(y, b3, w1t, zb, x0c, d, unperm)


def _hyena(proj, B, L, mix_w, short_w, short_b, w1, b1, w2, b2, w3, freq, decay, d_bias):
    tiles = _tiles(L)
    cb = tiles.hyena_cols
    dw1, dw1t, a3, b3 = _dft_tables(L)
    ng, rows_g, _ = a3.shape
    taps = _hyena_filter_taps(L, w1, b1, w2, b2, w3, freq, decay, tl=tiles.filter_rows)
    g = _dft_forward(taps.reshape(2, DFT_P, L // DFT_P, mix_w), dw1, a3, None, 2, L, cb, F32)
    g = g.reshape(ng, rows_g, mix_w)
    x0c, zb, zp = _hyena_gate(proj, short_w.astype(F32), short_b.astype(F32)[None], B, L, mix_w, cb)
    y = _dft_forward(zp, dw1, a3, g, B, L, cb, BF16)
    return _dft_inverse(y, b3, dw1t, zb, x0c, d_bias.astype(F32)[None], B, L, cb)


def _mem_kv_kernel(m_ref, g_ref, w_ref, o_ref):
    h = _rmsnorm(m_ref[...], g_ref[...]).astype(BF16)
    o_ref[...] = _dot(h, w_ref[...]).astype(o_ref.dtype)


def _mem_kv(mem, g, w, layer, M):
    BM, D = mem.shape
    N = w.shape[2]
    return pl.pallas_call(
        _mem_kv_kernel,
        grid=(BM // M,),
        in_specs=[
            pl.BlockSpec((M, D), lambda b: (b, 0)),
            pl.BlockSpec((1, D), lambda b: (0, 0)),
            pl.BlockSpec((None, D, N), lambda b: (layer, 0, 0)),
        ],
        out_specs=pl.BlockSpec((M, N), lambda b: (b, 0)),
        out_shape=jax.ShapeDtypeStruct((BM, N), BF16),
        compiler_params=_params("parallel"),
        name="memory_kv",
    )(mem, g, w)


def _mix_out_kernel(tok_ref, q_ref, kv_ref, w_ref, x_ref, g_ref, o_ref, mo_ref):
    mem_w = q_ref.shape[1]
    mix_w = tok_ref.shape[1]
    heads = [slice(h * HEAD_DIM, (h + 1) * HEAD_DIM) for h in range(mem_w // HEAD_DIM)]
    scores = [_dot_nt(q_ref[:, c], kv_ref[:, c]) for c in heads]
    o = _dot(tok_ref[...], w_ref[:mix_w, :])
    probs = []
    for s in scores:
        p = jnp.exp2(s - jnp.max(s, axis=-1, keepdims=True))
        probs.append((p.astype(BF16), jnp.sum(p, axis=-1, keepdims=True)))
    for c, (p, l) in zip(heads, probs):
        vcols = slice(mem_w + c.start, mem_w + c.stop)
        mo_ref[:, c] = (_dot(p, kv_ref[:, vcols]) / l).astype(mo_ref.dtype)
    o = o + _dot(mo_ref[...], w_ref[mix_w:, :])
    o_ref[...] = x_ref[...] + _rmsnorm(o, g_ref[...])


def _mix_out(tok, proj, kv, w_out, layer, x, g, L, tm):
    T, D = x.shape
    mix_w = tok.shape[1]
    mem_w = w_out.shape[1] - mix_w
    M = kv.shape[0] // (T // L)
    qblk = (proj.shape[1] - mem_w) // mem_w
    per_b = L // tm
    return pl.pallas_call(
        _mix_out_kernel,
        grid=(T // tm,),
        in_specs=[
            pl.BlockSpec((tm, mix_w), lambda i: (i, 0)),
            pl.BlockSpec((tm, mem_w), lambda i: (i, qblk)),
            pl.BlockSpec((M, 2 * mem_w), lambda i: (i // per_b, 0)),
            pl.BlockSpec((None, mix_w + mem_w, D), lambda i: (layer, 0, 0)),
            pl.BlockSpec((tm, D), lambda i: (i, 0)),
            pl.BlockSpec((1, D), lambda i: (0, 0)),
        ],
        out_specs=pl.BlockSpec((tm, D), lambda i: (i, 0)),
        out_shape=jax.ShapeDtypeStruct((T, D), F32),
        scratch_shapes=[pltpu.VMEM((tm, mem_w), BF16)],
        compiler_params=_params("parallel"),
        name="mix_out_proj",
    )(tok, proj, kv, w_out, x, g)


def _mlp_kernel(x_ref, gpre_ref, wu_ref, wd_ref, gpost_ref, o_ref, h_ref, acc_ref):
    f = pl.program_id(1)
    nf = pl.num_programs(1)
    tm = x_ref.shape[0]

    @pl.when(f == 0)
    def _():
        g = gpre_ref[...]
        for r in range(0, tm, NORM_ROWS):
            rows = slice(r, r + NORM_ROWS)
            h_ref[rows, :] = _rmsnorm(x_ref[rows, :], g).astype(h_ref.dtype)
        acc_ref[...] = jnp.zeros_like(acc_ref)

    h = h_ref[...]
    tf = wu_ref.shape[1]
    cw = tf // MLP_CHUNKS
    chunks = [slice(c * cw, (c + 1) * cw) for c in range(MLP_CHUNKS)]
    ups = [_dot(h, wu_ref[:, c]) for c in chunks]
    acts = [jnp.square(jnp.maximum(u, 0.0)).astype(BF16) for u in ups]
    d = acc_ref[...]
    for a, c in zip(acts, chunks):
        d = d + _dot(a, wd_ref[c, :])
    acc_ref[...] = d

    @pl.when(f == nf - 1)
    def _():
        g = gpost_ref[...]
        for r in range(0, tm, NORM_ROWS):
            rows = slice(r, r + NORM_ROWS)
            o_ref[rows, :] = x_ref[rows, :] + _rmsnorm(acc_ref[rows, :], g)


def _mlp(x, gpre, w_up, w_down, layer, gpost, tm, tf):
    T, D = x.shape
    F = w_up.shape[2]
    return pl.pallas_call(
        _mlp_kernel,
        grid=(T // tm, F // tf),
        in_specs=[
            pl.BlockSpec((tm, D), lambda i, f: (i, 0)),
            pl.BlockSpec((1, D), lambda i, f: (0, 0)),
            pl.BlockSpec((None, D, tf), lambda i, f: (layer, 0, f)),
            pl.BlockSpec((None, tf, D), lambda i, f: (layer, f, 0)),
            pl.BlockSpec((1, D), lambda i, f: (0, 0)),
        ],
        out_specs=pl.BlockSpec((tm, D), lambda i, f: (i, 0)),
        out_shape=jax.ShapeDtypeStruct((T, D), F32),
        scratch_shapes=[pltpu.VMEM((tm, D), BF16), pltpu.VMEM((tm, D), F32)],
        compiler_params=_params("parallel", "arbitrary"),
        name="relu2_mlp",
    )(x, gpre, w_up, w_down, gpost)


def _trunk(x, mem, wts):
    (g_mix_pre, w_in, na_rpb, hy_short_w, hy_short_b, hy_w1, hy_b1, hy_w2, hy_b2, hy_w3, hy_freq,
     hy_decay, hy_d, g_mem, w_mem_kv, w_out, g_mix_post, g_mlp_pre, w_up, w_down, g_mlp_post) = wts
    B, L, D = x.shape
    depth = w_in.shape[0]
    in_w = w_in.shape[2]
    mem_w = w_mem_kv.shape[2] // 2
    mix_w = (in_w - mem_w) // 3
    mem_tokens = mem.shape[1]
    x = x.reshape(B * L, D)
    mem = mem.reshape(-1, D)
    tiles = _tiles(L)
    qk_scale = HEAD_DIM ** -0.5 * LOG2E
    row = lambda v: v.astype(F32)[None]
    for i in range(depth):
        j = i // 2
        na_layer = i % 2 == 0
        colscale = jnp.concatenate([
            jnp.full((mix_w,), qk_scale if na_layer else 1.0, F32),
            jnp.ones((2 * mix_w,), F32),
            jnp.full((mem_w,), qk_scale, F32)])[None]
        proj = _norm_matmul(x, row(g_mix_pre[i]), w_in, i, colscale, tm=tiles.proj_rows, tn=tiles.proj_cols)
        if na_layer:
            bias = _na_bias_table(na_rpb[j], L // GRID_W)
            tok = _neighbourhood_attention(proj, bias, B, L, mix_w)
        else:
            tok = _hyena(proj, B, L, mix_w, hy_short_w[j], hy_short_b[j], hy_w1[j], hy_b1[j], hy_w2[j],
                         hy_b2[j], hy_w3[j], hy_freq[j], hy_decay[j], hy_d[j])
        kv = _mem_kv(mem, row(g_mem[i]), w_mem_kv, i, mem_tokens)
        x = _mix_out(tok, proj, kv, w_out, i, x, row(g_mix_post[i]), L, tm=tiles.mix_rows)
        x = _mlp(x, row(g_mlp_pre[i]), w_up, w_down, i, row(g_mlp_post[i]), tm=tiles.mlp_rows, tf=tiles.mlp_cols)
    return x.reshape(B, L, D)


def kernel(x_prompt, x_sample, mem_prompt, mem_sample, g_mix_pre, w_in, na_rpb, hy_short_w, hy_short_b,
           hy_w1, hy_b1, hy_w2, hy_b2, hy_w3, hy_freq, hy_decay, hy_d, g_mem, w_mem_kv, w_out,
           g_mix_post, g_mlp_pre, w_up, w_down, g_mlp_post):
    bf = lambda w: w.astype(BF16)
    wts = (g_mix_pre, bf(w_in), na_rpb, hy_short_w, hy_short_b, hy_w1, hy_b1, hy_w2, hy_b2, hy_w3, hy_freq,
           hy_decay, hy_d, g_mem, bf(w_mem_kv), bf(w_out), g_mix_post, g_mlp_pre, bf(w_up), bf(w_down), g_mlp_post)
    return (_trunk(x_prompt, mem_prompt, wts), _trunk(x_sample, mem_sample, wts))
```

```python
import functools
import math
from typing import NamedTuple

import numpy as np
import jax
import jax.numpy as jnp
from jax import lax
from jax.experimental import pallas as pl
from jax.experimental.pallas import tpu as pltpu

F32 = jnp.float32
BF16 = jnp.bfloat16

GRID_W = 64
HEAD_DIM = 128
NA_MAX_KH = 8
NA_KW = 16
FILTER_EMB = 33
FILTER_BANDS = (FILTER_EMB - 1) // 2
RMS_EPS = 1e-6
NEG_BIG = -1e30
LOG2E = math.log2(math.e)

VMEM_LIMIT_BYTES = 56 * 1024 * 1024
LANES = 128
SUBLANES_F32 = 8
SUBLANES_BF16 = 16

NA_ROWS_PER_STEP = 4
NA_HEADS_PER_STEP = 6
NA_KEY_ROWS = NA_ROWS_PER_STEP + NA_MAX_KH
MLP_CHUNKS = 2
NORM_ROWS = 64
DFT_P = 32
DFT_MG = SUBLANES_F32
HYENA_LANE_ELEMS = 1 << 20


class _Tiles(NamedTuple):
    proj_rows: int
    proj_cols: int
    mix_rows: int
    mlp_rows: int
    mlp_cols: int
    filter_rows: int
    hyena_cols: int
    kv_rows: int


def _tiles(L):
    return _Tiles(proj_rows=min(1024, L), proj_cols=1280, mix_rows=min(512, L), mlp_rows=min(512, L),
                  mlp_cols=1024, filter_rows=min(512, L), hyena_cols=min(512, HYENA_LANE_ELEMS // L),
                  kv_rows=1024)


def _params(*sem):
    return pltpu.CompilerParams(dimension_semantics=sem, vmem_limit_bytes=VMEM_LIMIT_BYTES)


def _rmsnorm(x, g):
    ms = jnp.mean(x * x, axis=-1, keepdims=True)
    return x * lax.rsqrt(ms + RMS_EPS) * g


def _dot(a, b):
    return jnp.dot(a, b, preferred_element_type=F32)


def _dot_nt(a, b):
    return lax.dot_general(a, b, (((1,), (1,)), ((), ())), preferred_element_type=F32)


def _norm_matmul_kernel(x_ref, g_ref, w_ref, s_ref, o_ref, h_ref):
    @pl.when(pl.program_id(1) == 0)
    def _():
        g = g_ref[...]
        for r in range(0, h_ref.shape[0], NORM_ROWS):
            rows = slice(r, r + NORM_ROWS)
            h_ref[rows, :] = _rmsnorm(x_ref[rows, :], g).astype(h_ref.dtype)

    o_ref[...] = (_dot(h_ref[...], w_ref[...]) * s_ref[...]).astype(o_ref.dtype)


def _norm_matmul(x, g, w, layer, colscale, tm, tn):
    T, D = x.shape
    N = w.shape[2]
    return pl.pallas_call(
        _norm_matmul_kernel,
        grid=(T // tm, N // tn),
        in_specs=[
            pl.BlockSpec((tm, D), lambda i, j: (i, 0)),
            pl.BlockSpec((1, D), lambda i, j: (0, 0)),
            pl.BlockSpec((None, D, tn), lambda i, j: (layer, 0, j)),
            pl.BlockSpec((1, tn), lambda i, j: (0, j)),
        ],
        out_specs=pl.BlockSpec((tm, tn), lambda i, j: (i, j)),
        out_shape=jax.ShapeDtypeStruct((T, N), BF16),
        scratch_shapes=[pltpu.VMEM((tm, D), BF16)],
        compiler_params=_params("parallel", "arbitrary"),
        name="norm_in_proj",
    )(x, g, w, colscale)


def _na_bias_table(rpb, rows):
    R, KR, W = NA_ROWS_PER_STEP, NA_KEY_ROWS, GRID_W
    kh = min(NA_MAX_KH, rows)
    nblk = rows // R
    qc = np.arange(W)
    cs = np.clip(qc - NA_KW // 2, 0, W - NA_KW)
    kc = np.arange(W)
    col_ok = (kc[None, :] >= cs[:, None]) & (kc[None, :] < cs[:, None] + NA_KW)
    dr_all, ok_all = [], []
    for blk in (0, min(1, nblk - 1), nblk - 1):
        ks = int(np.clip(blk * R - kh // 2, 0, rows - KR))
        r = blk * R + np.arange(R)
        rs = np.clip(r - kh // 2, 0, rows - kh)
        key = ks + np.arange(KR)
        row_ok = (key[None, :] >= rs[:, None]) & (key[None, :] < rs[:, None] + kh)
        dr = np.clip(key[None, :] - r[:, None] + NA_MAX_KH - 1, 0, 2 * NA_MAX_KH - 2)
        dr_all.append(dr)
        ok_all.append(row_ok)
    pad = W - NA_KW
    padded = jnp.pad(rpb.astype(F32) * LOG2E, ((0, 0), (0, 0), (pad, pad)))
    per_dr = jnp.stack([padded[:, :, W - 1 - q:2 * W - 1 - q] for q in range(W)], axis=2)
    per_dr = jnp.where(col_ok, per_dr, NEG_BIG)
    H = rpb.shape[0]

    def assemble(p_ref, o_ref):
        masked = jnp.full((W, W), NEG_BIG, F32)
        block = lambda v, r, k: p_ref[0, int(dr_all[v][r, k])] if ok_all[v][r, k] else masked
        for v in range(3):
            for r in range(R):
                for k in range(0, KR, 2):
                    o_ref[v, 0, r * W:(r + 1) * W, k * W:(k + 2) * W] = jnp.concatenate(
                        [block(v, r, k), block(v, r, k + 1)], axis=1)

    return pl.pallas_call(
        assemble,
        grid=(H,),
        in_specs=[pl.BlockSpec((1,) + per_dr.shape[1:], lambda h: (h, 0, 0, 0))],
        out_specs=pl.BlockSpec((3, 1, R * W, KR * W), lambda h: (0, h, 0, 0)),
        out_shape=jax.ShapeDtypeStruct((3, H, R * W, KR * W), F32),
        compiler_params=_params("parallel"),
        name="na_bias_table",
    )(per_dr)


def _na_kernel(q_ref, k_ref, v_ref, b_ref, o_ref, *, rows):
    R, KR, W = NA_ROWS_PER_STEP, NA_KEY_ROWS, GRID_W
    i = pl.program_id(2)
    ks = jnp.clip(i * R - min(NA_MAX_KH, rows) // 2, 0, rows - KR) * W
    ks = pl.multiple_of(ks, W)
    heads = [slice(h * HEAD_DIM, (h + 1) * HEAD_DIM) for h in range(NA_HEADS_PER_STEP)]
    keys = pl.ds(ks, KR * W)
    scores = [_dot_nt(q_ref[:, c], k_ref[keys, c]) + b_ref[0, h] for h, c in enumerate(heads)]
    probs = []
    for s in scores:
        p = jnp.exp2(s - jnp.max(s, axis=-1, keepdims=True))
        probs.append((p.astype(BF16), jnp.sum(p, axis=-1, keepdims=True)))
    for c, (p, l) in zip(heads, probs):
        o_ref[:, c] = (_dot(p, v_ref[keys, c]) / l).astype(o_ref.dtype)


def _neighbourhood_attention(proj, bias, B, L, mix_w):
    R, KR, W = NA_ROWS_PER_STEP, NA_KEY_ROWS, GRID_W
    rows = L // W
    nblk = rows // R
    hw = NA_HEADS_PER_STEP * HEAD_DIM
    ngrp = mix_w // hw
    assert rows % R == 0 and rows >= KR and nblk >= 2 and R >= NA_MAX_KH // 2

    def variant(i):
        return jnp.where(i == 0, 0, jnp.where(i == nblk - 1, 2, 1))

    return pl.pallas_call(
        functools.partial(_na_kernel, rows=rows),
        grid=(B, ngrp, nblk),
        in_specs=[
            pl.BlockSpec((R * W, hw), lambda b, g, i: (b * nblk + i, g)),
            pl.BlockSpec((L, hw), lambda b, g, i: (b, ngrp + g)),
            pl.BlockSpec((L, hw), lambda b, g, i: (b, 2 * ngrp + g)),
            pl.BlockSpec((1, NA_HEADS_PER_STEP, R * W, KR * W), lambda b, g, i: (variant(i), g, 0, 0)),
        ],
        out_specs=pl.BlockSpec((R * W, hw), lambda b, g, i: (b * nblk + i, g)),
        out_shape=jax.ShapeDtypeStruct((B * L, mix_w), BF16),
        compiler_params=_params("parallel", "parallel", "arbitrary"),
        name="neighbourhood_attention",
    )(proj, proj, proj, bias)


def _position_features(L):
    P = DFT_P
    pos = (jnp.arange(L // P, dtype=F32)[None, :] * P + jnp.arange(P, dtype=F32)[:, None]).reshape(L)
    t = pos / (L - 1)
    wpos = 2.0 * math.pi * pos / L
    fb = jnp.linspace(1e-4, FILTER_BANDS - 1, FILTER_BANDS, dtype=F32)
    ang = wpos[:, None] * fb[None, :]
    z = jnp.concatenate([t[:, None], jnp.cos(ang), -jnp.sin(ang)], axis=-1)
    return jnp.pad(z, ((0, 0), (0, LANES - FILTER_EMB)))


def _hy_filter_kernel(z_ref, w1_ref, b1_ref, w2_ref, b2_ref, w3_ref, fr_ref, dec_ref, h_ref, *, mix_w):
    hp = lax.Precision.HIGHEST
    fr = fr_ref[...]
    z = z_ref[...]
    h = jnp.sin(fr * (jnp.dot(z, w1_ref[...], precision=hp, preferred_element_type=F32) + b1_ref[...]))
    h = jnp.sin(fr * (jnp.dot(h, w2_ref[...], precision=hp, preferred_element_type=F32) + b2_ref[...]))

    def taps(d):
        w3 = w3_ref[:, d * mix_w:(d + 1) * mix_w]
        win = jnp.exp(-z[:, 0:1] * jnp.abs(dec_ref[d:d + 1, :]))
        return jnp.dot(h, w3, precision=hp, preferred_element_type=F32) * win

    tl = z_ref.shape[0]
    lag0 = pl.program_id(0) * tl + lax.broadcasted_iota(jnp.int32, (tl, 1), 0) == 0
    h_fwd, h_bwd = taps(0), taps(1)
    h_ref[0] = h_fwd + jnp.where(lag0, h_bwd, 0.0)
    h_ref[1] = jnp.where(lag0, 0.0, h_bwd)


def _hyena_filter_taps(L, w1, b1, w2, b2, w3, freq, decay, tl):
    mix_w = decay.shape[1]
    order = w2.shape[0]
    w1p = jnp.pad(w1.astype(F32), ((0, LANES - FILTER_EMB), (0, 0)))
    full = lambda shape: pl.BlockSpec(shape, lambda i: (0, 0))
    return pl.pallas_call(
        functools.partial(_hy_filter_kernel, mix_w=mix_w),
        grid=(L // tl,),
        in_specs=[
            pl.BlockSpec((tl, LANES), lambda i: (i, 0)),
            full((LANES, order)), full((1, order)), full((order, order)), full((1, order)),
            full((order, 2 * mix_w)), full((1, order)), full((2, mix_w)),
        ],
        out_specs=pl.BlockSpec((2, tl, mix_w), lambda i: (0, i, 0)),
        out_shape=jax.ShapeDtypeStruct((2, L, mix_w), F32),
        compiler_params=_params("parallel"),
        name="hyena_filter_taps",
    )(_position_features(L), w1p, b1.astype(F32)[None], w2.astype(F32), b2.astype(F32)[None],
      w3.astype(F32), freq.astype(F32)[None], decay.astype(F32))


def _dft_tables(L):
    P, MG = DFT_P, DFT_MG
    A, n = L // P, 2 * L
    Mm, J = 2 * A, P // 2
    NG = Mm // MG
    ar = lambda size: jnp.arange(size, dtype=jnp.int32)
    ang1 = (((2 * ar(Mm) + 1)[:, None] * ar(A)[None, :]) % (2 * Mm)).astype(F32) * (math.pi / Mm)
    w1 = jnp.concatenate([jnp.cos(ang1), -jnp.sin(ang1)], axis=0)
    k = (Mm * ar(J)[None, None, :, None] + MG * ar(NG)[:, None, None, None] + ar(MG)[None, :, None, None])
    ph = ((ar(P)[None, None, None, :] * (2 * k + 1)) % (2 * n)).astype(F32) * (math.pi / n)
    cr, ci = lax.optimization_barrier((jnp.cos(ph), -jnp.sin(ph)))
    eye = jnp.eye(MG, dtype=F32)
    blk = lambda c: c[..., None] * eye[None, :, None, None, :]
    top = jnp.stack([blk(cr), blk(-ci)], axis=3)
    bot = jnp.stack([blk(ci), blk(cr)], axis=3)
    a3 = jnp.stack([top, bot], axis=1).reshape(NG, 2 * MG * J, 2 * P * MG)
    return w1.astype(BF16), w1.T.astype(BF16), a3.astype(BF16), a3.transpose(0, 2, 1).astype(BF16)


def _short_conv(ref, w_ref, b_ref, start, size, L):
    halo, tile = SUBLANES_BF16, SUBLANES_F32
    cur = ref[pl.ds(start, size), :].astype(F32)
    up, dn = pltpu.roll(cur, 1, axis=0), pltpu.roll(cur, size - 1, axis=0)
    prev_start = pl.multiple_of(jnp.maximum(start - halo, 0), halo)
    next_start = pl.multiple_of(jnp.minimum(start + size, L - halo), halo)
    prev = ref[pl.ds(prev_start, halo), :][halo - 1:halo, :].astype(F32)
    nxt = ref[pl.ds(next_start, halo), :][0:1, :].astype(F32)
    prev = jnp.where(start == 0, 0.0, prev)
    nxt = jnp.where(start + size == L, 0.0, nxt)
    row = lax.broadcasted_iota(jnp.int32, (tile, 1), 0)
    up = jnp.concatenate([jnp.where(row == 0, prev, up[:tile]), up[tile:]], axis=0)
    dn = jnp.concatenate([dn[:size - tile], jnp.where(row == tile - 1, nxt, dn[size - tile:])], axis=0)
    return up * w_ref[0:1, :] + cur * w_ref[1:2, :] + dn * w_ref[2:3, :] + b_ref[...]


def _hy_gate_kernel(x0_ref, x1_ref, v_ref, w0_ref, w1_ref, wv_ref, b0_ref, b1_ref, bv_ref,
                    perm_ref, x0c_ref, zb_ref, zp_ref):
    P, MG = DFT_P, DFT_MG
    chunk = P * MG
    L, cb = x0_ref.shape

    def body(c, carry):
        start = pl.multiple_of(c * chunk, chunk)
        rows = pl.ds(start, chunk)
        conv = functools.partial(_short_conv, start=start, size=chunk, L=L)
        x0c_ref[rows, :] = conv(x0_ref, w0_ref, b0_ref).astype(x0c_ref.dtype)
        x1c = conv(x1_ref, w1_ref, b1_ref)
        vc = conv(v_ref, wv_ref, bv_ref)
        z = (x1c * vc).astype(BF16)
        zb_ref[rows, :] = z
        zp = _dot(perm_ref[...], z).reshape(P, MG, cb)
        zp_ref[:, pl.ds(pl.multiple_of(c * MG, MG), MG), :] = zp
        return carry

    lax.fori_loop(0, L // chunk, body, 0, unroll=2)


def _time_split_permutation():
    P, MG = DFT_P, DFT_MG
    r = np.arange(P * MG)
    perm = np.zeros((P * MG, P * MG), np.float32)
    perm[r, (r % MG) * P + r // MG] = 1.0
    return perm


def _hyena_gate(proj, short_w, short_b, B, L, mix_w, cb):
    nc = mix_w // cb
    P = DFT_P
    col = lambda part: pl.BlockSpec((L, cb), lambda b, c: (b, part * nc + c))
    wspec = lambda part: pl.BlockSpec((3, cb), lambda b, c: (0, part * nc + c))
    bspec = lambda part: pl.BlockSpec((1, cb), lambda b, c: (0, part * nc + c))
    perm = jnp.asarray(_time_split_permutation(), BF16)
    seq = pl.BlockSpec((None, None, L, cb), lambda b, c: (b, c, 0, 0))
    return pl.pallas_call(
        _hy_gate_kernel,
        grid=(B, nc),
        in_specs=[col(0), col(1), col(2), wspec(0), wspec(1), wspec(2), bspec(0), bspec(1), bspec(2),
                  pl.BlockSpec(perm.shape, lambda b, c: (0, 0))],
        out_specs=[seq, seq, pl.BlockSpec((None, None, P, L // P, cb), lambda b, c: (b, c, 0, 0, 0))],
        out_shape=[jax.ShapeDtypeStruct((B, nc, L, cb), BF16), jax.ShapeDtypeStruct((B, nc, L, cb), BF16),
                   jax.ShapeDtypeStruct((B, nc, P, L // P, cb), F32)],
        compiler_params=_params("parallel", "parallel"),
        name="hyena_gate",
    )(proj, proj, proj, short_w, short_w, short_w, short_b, short_b, short_b, perm)


def _dft_fwd_kernel(*refs, filtered):
    if filtered:
        z_ref, w1_ref, a3_ref, g_ref, y_ref, v_ref = refs
    else:
        z_ref, w1_ref, a3_ref, y_ref, v_ref = refs
    P, MG = DFT_P, DFT_MG
    cb = z_ref.shape[2]
    ng, rows_g = a3_ref.shape[0], a3_ref.shape[1]
    half = rows_g // 2
    w1 = w1_ref[...]
    for b in range(P):
        v = _dot(w1, z_ref[b].astype(BF16))
        v_ref[:, :, b * MG:(b + 1) * MG, :] = v.reshape(2, ng, MG, cb)

    if not filtered:
        backward = pl.program_id(1) == 1
        scale = 2.0 / y_ref.shape[0]
        im_scale = jnp.where(backward, -scale, scale)

        @pl.when(jnp.logical_not(backward))
        def _():
            y_ref[...] = jnp.zeros_like(y_ref)

    def group(g, carry):
        vg = jnp.concatenate([v_ref[0, g], v_ref[1, g]], axis=0).astype(BF16)
        x = _dot(a3_ref[g], vg)
        rows = pl.ds(pl.multiple_of(g * rows_g, rows_g), rows_g)
        xr, xi = x[:half], x[half:]
        if filtered:
            gr, gi = g_ref[g, :half], g_ref[g, half:]
            y_ref[rows, :] = jnp.concatenate([xr * gr - xi * gi, xr * gi + xi * gr], axis=0).astype(y_ref.dtype)
        else:
            y_ref[rows, :] += jnp.concatenate([xr * scale, xi * im_scale], axis=0)
        return carry

    lax.fori_loop(0, ng, group, 0, unroll=8)


def _dft_forward(z, w1, a3, g, Bz, L, cb, out_dtype):
    ng, rows_g, cols_g = a3.shape
    const = lambda shape: pl.BlockSpec(shape, lambda c, b: (0,) * len(shape), pipeline_mode=pl.Buffered(1))
    P, A = DFT_P, L // DFT_P
    if g is not None:
        nc = z.shape[1]
        in_specs = [pl.BlockSpec((None, None, P, A, cb), lambda c, b: (b, c, 0, 0, 0)), const(w1.shape),
                    const(a3.shape),
                    pl.BlockSpec((ng, rows_g, cb), lambda c, b: (0, 0, c), pipeline_mode=pl.Buffered(1))]
        args = [z, w1, a3, g]
        out_spec = pl.BlockSpec((None, None, 2 * L, cb), lambda c, b: (b, c, 0, 0))
        out_shape = (Bz, nc, 2 * L, cb)
    else:
        assert Bz == 2
        nc = z.shape[3] // cb
        in_specs = [pl.BlockSpec((None, P, A, cb), lambda c, b: (b, 0, 0, c)), const(w1.shape), const(a3.shape)]
        args = [z, w1, a3]
        out_spec = pl.BlockSpec((None, 2 * L, cb), lambda c, b: (0, 0, c))
        out_shape = (1, 2 * L, z.shape[3])
    return pl.pallas_call(
        functools.partial(_dft_fwd_kernel, filtered=g is not None),
        grid=(nc, Bz),
        in_specs=in_specs,
        out_specs=out_spec,
        out_shape=jax.ShapeDtypeStruct(out_shape, out_dtype),
        scratch_shapes=[pltpu.VMEM((2, ng, cols_g // 2, cb), F32)],
        compiler_params=_params("parallel", "arbitrary"),
        name="hyena_dft_forward",
    )(*args)


def _dft_inv_kernel(y_ref, b3_ref, w1t_ref, zb_ref, x0_ref, d_ref, unperm_ref, o_ref, q_ref, t_ref):
    P, MG = DFT_P, DFT_MG
    chunk = P * MG
    cb = y_ref.shape[1]
    ng, rows_g, cols_g = b3_ref.shape
    half = rows_g // 2

    def group(g, carry):
        yg = y_ref[pl.ds(pl.multiple_of(g * cols_g, cols_g), cols_g), :]
        q = _dot(b3_ref[g], yg)
        q_ref[0, g] = q[:half]
        q_ref[1, g] = q[half:]
        return carry

    lax.fori_loop(0, ng, group, 0, unroll=8)
    w1t = w1t_ref[...]
    for b in range(P):
        qb = q_ref[:, :, b * MG:(b + 1) * MG, :].reshape(2 * ng * MG, cb)
        t_ref[b] = _dot(w1t, qb.astype(BF16))

    unperm = unperm_ref[...]

    def gate(c, carry):
        rows = pl.ds(pl.multiple_of(c * chunk, chunk), chunk)
        yp = t_ref[:, pl.ds(pl.multiple_of(c * MG, MG), MG), :].reshape(chunk, cb)
        hi = yp.astype(BF16)
        lo = (yp - hi.astype(F32)).astype(BF16)
        y = _dot(unperm, hi) + _dot(unperm, lo)
        y = y + d_ref[...] * zb_ref[rows, :].astype(F32)
        o_ref[rows, :] = (x0_ref[rows, :].astype(F32) * y).astype(o_ref.dtype)
        return carry

    lax.fori_loop(0, zb_ref.shape[0] // chunk, gate, 0, unroll=4)


def _dft_inverse(y, b3, w1t, zb, x0c, d, B, L, cb):
    nc = zb.shape[1]
    ng, rows_g, _ = b3.shape
    const = lambda shape: pl.BlockSpec(shape, lambda c, b: (0,) * len(shape), pipeline_mode=pl.Buffered(1))
    blocked = lambda rows: pl.BlockSpec((None, None, rows, cb), lambda c, b: (b, c, 0, 0))
    unperm = jnp.asarray(_time_split_permutation().T, BF16)
    return pl.pallas_call(
        _dft_inv_kernel,
        grid=(nc, B),
        in_specs=[blocked(2 * L), const(b3.shape), const(w1t.shape), blocked(L), blocked(L),
                  pl.BlockSpec((1, cb), lambda c, b: (0, c)), const(unperm.shape)],
        out_specs=pl.BlockSpec((L, cb), lambda c, b: (b, c)),
        out_shape=jax.ShapeDtypeStruct((B * L, nc * cb), BF16),
        scratch_shapes=[pltpu.VMEM((2, ng, rows_g // 2, cb), F32),
                        pltpu.VMEM((DFT_P, L // DFT_P, cb), F32)],
        compiler_params=_params("parallel", "arbitrary"),
        name="hyena_dft_inverse",
    )(y, b3, w1t, zb, x0c, d, unperm)


def _hyena(proj, B, L, mix_w, short_w, short_b, w1, b1, w2, b2, w3, freq, decay, d_bias):
    tiles = _tiles(L)
    cb = tiles.hyena_cols
    dw1, dw1t, a3, b3 = _dft_tables(L)
    ng, rows_g, _ = a3.shape
    taps = _hyena_filter_taps(L, w1, b1, w2, b2, w3, freq, decay, tl=tiles.filter_rows)
    g = _dft_forward(taps.reshape(2, DFT_P, L // DFT_P, mix_w), dw1, a3, None, 2, L, cb, F32)
    g = g.reshape(ng, rows_g, mix_w)
    x0c, zb, zp = _hyena_gate(proj, short_w.astype(F32), short_b.astype(F32)[None], B, L, mix_w, cb)
    y = _dft_forward(zp, dw1, a3, g, B, L, cb, BF16)
    return _dft_inverse(y, b3, dw1t, zb, x0c, d_bias.astype(F32)[None], B, L, cb)


def _mem_kv_kernel(m_ref, g_ref, w_ref, o_ref):
    h = _rmsnorm(m_ref[...], g_ref[...]).astype(BF16)
    o_ref[...] = _dot(h, w_ref[...]).astype(o_ref.dtype)


def _mem_kv(mem, g, w, layer, M):
    BM, D = mem.shape
    N = w.shape[2]
    return pl.pallas_call(
        _mem_kv_kernel,
        grid=(BM // M,),
        in_specs=[
            pl.BlockSpec((M, D), lambda b: (b, 0)),
            pl.BlockSpec((1, D), lambda b: (0, 0)),
            pl.BlockSpec((None, D, N), lambda b: (layer, 0, 0)),
        ],
        out_specs=pl.BlockSpec((M, N), lambda b: (b, 0)),
        out_shape=jax.ShapeDtypeStruct((BM, N), BF16),
        compiler_params=_params("parallel"),
        name="memory_kv",
    )(mem, g, w)


def _mix_out_kernel(tok_ref, q_ref, kv_ref, w_ref, x_ref, g_ref, o_ref, mo_ref):
    mem_w = q_ref.shape[1]
    mix_w = tok_ref.shape[1]
    heads = [slice(h * HEAD_DIM, (h + 1) * HEAD_DIM) for h in range(mem_w // HEAD_DIM)]
    scores = [_dot_nt(q_ref[:, c], kv_ref[:, c]) for c in heads]
    o = _dot(tok_ref[...], w_ref[:mix_w, :])
    probs = []
    for s in scores:
        p = jnp.exp2(s - jnp.max(s, axis=-1, keepdims=True))
        probs.append((p.astype(BF16), jnp.sum(p, axis=-1, keepdims=True)))
    for c, (p, l) in zip(heads, probs):
        vcols = slice(mem_w + c.start, mem_w + c.stop)
        mo_ref[:, c] = (_dot(p, kv_ref[:, vcols]) / l).astype(mo_ref.dtype)
    o = o + _dot(mo_ref[...], w_ref[mix_w:, :])
    o_ref[...] = x_ref[...] + _rmsnorm(o, g_ref[...])


def _mix_out(tok, proj, kv, w_out, layer, x, g, L, tm):
    T, D = x.shape
    mix_w = tok.shape[1]
    mem_w = w_out.shape[1] - mix_w
    M = kv.shape[0] // (T // L)
    qblk = (proj.shape[1] - mem_w) // mem_w
    per_b = L // tm
    return pl.pallas_call(
        _mix_out_kernel,
        grid=(T // tm,),
        in_specs=[
            pl.BlockSpec((tm, mix_w), lambda i: (i, 0)),
            pl.BlockSpec((tm, mem_w), lambda i: (i, qblk)),
            pl.BlockSpec((M, 2 * mem_w), lambda i: (i // per_b, 0)),
            pl.BlockSpec((None, mix_w + mem_w, D), lambda i: (layer, 0, 0)),
            pl.BlockSpec((tm, D), lambda i: (i, 0)),
            pl.BlockSpec((1, D), lambda i: (0, 0)),
        ],
        out_specs=pl.BlockSpec((tm, D), lambda i: (i, 0)),
        out_shape=jax.ShapeDtypeStruct((T, D), F32),
        scratch_shapes=[pltpu.VMEM((tm, mem_w), BF16)],
        compiler_params=_params("parallel"),
        name="mix_out_proj",
    )(tok, proj, kv, w_out, x, g)


def _mlp_kernel(x_ref, gpre_ref, wu_ref, wd_ref, gpost_ref, o_ref, h_ref, acc_ref):
    f = pl.program_id(1)
    nf = pl.num_programs(1)
    tm = x_ref.shape[0]

    @pl.when(f == 0)
    def _():
        g = gpre_ref[...]
        for r in range(0, tm, NORM_ROWS):
            rows = slice(r, r + NORM_ROWS)
            h_ref[rows, :] = _rmsnorm(x_ref[rows, :], g).astype(h_ref.dtype)
        acc_ref[...] = jnp.zeros_like(acc_ref)

    h = h_ref[...]
    tf = wu_ref.shape[1]
    cw = tf // MLP_CHUNKS
    chunks = [slice(c * cw, (c + 1) * cw) for c in range(MLP_CHUNKS)]
    ups = [_dot(h, wu_ref[:, c]) for c in chunks]
    acts = [jnp.square(jnp.maximum(u, 0.0)).astype(BF16) for u in ups]
    d = acc_ref[...]
    for a, c in zip(acts, chunks):
        d = d + _dot(a, wd_ref[c, :])
    acc_ref[...] = d

    @pl.when(f == nf - 1)
    def _():
        g = gpost_ref[...]
        for r in range(0, tm, NORM_ROWS):
            rows = slice(r, r + NORM_ROWS)
            o_ref[rows, :] = x_ref[rows, :] + _rmsnorm(acc_ref[rows, :], g)


def _mlp(x, gpre, w_up, w_down, layer, gpost, tm, tf):
    T, D = x.shape
    F = w_up.shape[2]
    return pl.pallas_call(
        _mlp_kernel,
        grid=(T // tm, F // tf),
        in_specs=[
            pl.BlockSpec((tm, D), lambda i, f: (i, 0)),
            pl.BlockSpec((1, D), lambda i, f: (0, 0)),
            pl.BlockSpec((None, D, tf), lambda i, f: (layer, 0, f)),
            pl.BlockSpec((None, tf, D), lambda i, f: (layer, f, 0)),
            pl.BlockSpec((1, D), lambda i, f: (0, 0)),
        ],
        out_specs=pl.BlockSpec((tm, D), lambda i, f: (i, 0)),
        out_shape=jax.ShapeDtypeStruct((T, D), F32),
        scratch_shapes=[pltpu.VMEM((tm, D), BF16), pltpu.VMEM((tm, D), F32)],
        compiler_params=_params("parallel", "arbitrary"),
        name="relu2_mlp",
    )(x, gpre, w_up, w_down, gpost)


def _trunk(x, mem, wts):
    (g_mix_pre, w_in, na_rpb, hy_short_w, hy_short_b, hy_w1, hy_b1, hy_w2, hy_b2, hy_w3, hy_freq,
     hy_decay, hy_d, g_mem, w_mem_kv, w_out, g_mix_post, g_mlp_pre, w_up, w_down, g_mlp_post) = wts
    B, L, D = x.shape
    depth = w_in.shape[0]
    in_w = w_in.shape[2]
    mem_w = w_mem_kv.shape[2] // 2
    mix_w = (in_w - mem_w) // 3
    x = x.reshape(B * L, D)
    mem = mem.reshape(-1, D)
    tiles = _tiles(L)
    qk_scale = HEAD_DIM ** -0.5 * LOG2E
    row = lambda v: v.astype(F32)[None]
    for i in range(depth):
        j = i // 2
        na_layer = i % 2 == 0
        colscale = jnp.concatenate([
            jnp.full((mix_w,), qk_scale if na_layer else 1.0, F32),
            jnp.ones((2 * mix_w,), F32),
            jnp.full((mem_w,), qk_scale, F32)])[None]
        proj = _norm_matmul(x, row(g_mix_pre[i]), w_in, i, colscale, tm=tiles.proj_rows, tn=tiles.proj_cols)
        if na_layer:
            bias = _na_bias_table(na_rpb[j], L // GRID_W)
            tok = _neighbourhood_attention(proj, bias, B, L, mix_w)
        else:
            tok = _hyena(proj, B, L, mix_w, hy_short_w[j], hy_short_b[j], hy_w1[j], hy_b1[j], hy_w2[j],
                         hy_b2[j], hy_w3[j], hy_freq[j], hy_decay[j], hy_d[j])
        kv = _mem_kv(mem, row(g_mem[i]), w_mem_kv, i, min(tiles.kv_rows, mem.shape[0]))
        x = _mix_out(tok, proj, kv, w_out, i, x, row(g_mix_post[i]), L, tm=tiles.mix_rows)
        x = _mlp(x, row(g_mlp_pre[i]), w_up, w_down, i, row(g_mlp_post[i]), tm=tiles.mlp_rows, tf=tiles.mlp_cols)
    return x.reshape(B, L, D)


def kernel(x_prompt, x_sample, mem_prompt, mem_sample, g_mix_pre, w_in, na_rpb, hy_short_w, hy_short_b,
           hy_w1, hy_b1, hy_w2, hy_b2, hy_w3, hy_freq, hy_decay, hy_d, g_mem, w_mem_kv, w_out,
           g_mix_post, g_mlp_pre, w_up, w_down, g_mlp_post):
    bf = lambda w: w.astype(BF16)
    wts = (g_mix_pre, bf(w_in), na_rpb, hy_short_w, hy_short_b, hy_w1, hy_b1, hy_w2, hy_b2, hy_w3, hy_freq,
           hy_decay, hy_d, g_mem, bf(w_mem_kv), bf(w_out), g_mix_post, g_mlp_pre, bf(w_up), bf(w_down), g_mlp_post)
    return (_trunk(x_prompt, mem_prompt, wts), _trunk(x_sample, mem_sample, wts))
```

```python
import functools
import math
from typing import NamedTuple

import numpy as np
import jax
import jax.numpy as jnp
from jax import lax
from jax.experimental import pallas as pl
from jax.experimental.pallas import tpu as pltpu

F32 = jnp.float32
BF16 = jnp.bfloat16

GRID_W = 64
HEAD_DIM = 128
NA_MAX_KH = 8
NA_KW = 16
FILTER_EMB = 33
FILTER_BANDS = (FILTER_EMB - 1) // 2
RMS_EPS = 1e-6
NEG_BIG = -1e30
LOG2E = math.log2(math.e)

VMEM_LIMIT_BYTES = 56 * 1024 * 1024
LANES = 128
SUBLANES_F32 = 8
SUBLANES_BF16 = 16

NA_ROWS_PER_STEP = 4
NA_HEADS_PER_STEP = 6
NA_KEY_ROWS = NA_ROWS_PER_STEP + NA_MAX_KH
MLP_CHUNKS = 2
NORM_ROWS = 64
DFT_P = 32
DFT_MG = SUBLANES_F32
HYENA_LANE_ELEMS = 1 << 20


class _Tiles(NamedTuple):
    proj_rows: int
    proj_cols: int
    mix_rows: int
    mlp_rows: int
    mlp_cols: int
    filter_rows: int
    hyena_cols: int
    kv_rows: int


def _tiles(L):
    return _Tiles(proj_rows=min(1024, L), proj_cols=1280, mix_rows=min(512, L), mlp_rows=min(512, L),
                  mlp_cols=1024, filter_rows=min(512, L), hyena_cols=min(512, HYENA_LANE_ELEMS // L),
                  kv_rows=1024)


def _params(*sem):
    return pltpu.CompilerParams(dimension_semantics=sem, vmem_limit_bytes=VMEM_LIMIT_BYTES)


def _rmsnorm(x, g):
    ms = jnp.mean(x * x, axis=-1, keepdims=True)
    return x * lax.rsqrt(ms + RMS_EPS) * g


def _dot(a, b):
    return jnp.dot(a, b, preferred_element_type=F32)


def _dot_nt(a, b):
    return lax.dot_general(a, b, (((1,), (1,)), ((), ())), preferred_element_type=F32)


def _norm_matmul_kernel(x_ref, g_ref, w_ref, s_ref, o_ref, h_ref):
    @pl.when(pl.program_id(1) == 0)
    def _():
        g = g_ref[...]
        for r in range(0, h_ref.shape[0], NORM_ROWS):
            rows = slice(r, r + NORM_ROWS)
            h_ref[rows, :] = _rmsnorm(x_ref[rows, :], g).astype(h_ref.dtype)

    o_ref[...] = (_dot(h_ref[...], w_ref[...]) * s_ref[...]).astype(o_ref.dtype)


def _norm_matmul(x, g, w, layer, colscale, tm, tn):
    T, D = x.shape
    N = w.shape[2]
    return pl.pallas_call(
        _norm_matmul_kernel,
        grid=(T // tm, N // tn),
        in_specs=[
            pl.BlockSpec((tm, D), lambda i, j: (i, 0)),
            pl.BlockSpec((1, D), lambda i, j: (0, 0)),
            pl.BlockSpec((None, D, tn), lambda i, j: (layer, 0, j)),
            pl.BlockSpec((1, tn), lambda i, j: (0, j)),
        ],
        out_specs=pl.BlockSpec((tm, tn), lambda i, j: (i, j)),
        out_shape=jax.ShapeDtypeStruct((T, N), BF16),
        scratch_shapes=[pltpu.VMEM((tm, D), BF16)],
        compiler_params=_params("parallel", "arbitrary"),
        name="norm_in_proj",
    )(x, g, w, colscale)


def _na_bias_table(rpb, rows):
    R, KR, W = NA_ROWS_PER_STEP, NA_KEY_ROWS, GRID_W
    kh = min(NA_MAX_KH, rows)
    nblk = rows // R
    qc = np.arange(W)
    cs = np.clip(qc - NA_KW // 2, 0, W - NA_KW)
    kc = np.arange(W)
    col_ok = (kc[None, :] >= cs[:, None]) & (kc[None, :] < cs[:, None] + NA_KW)
    dr_all, ok_all = [], []
    for blk in (0, min(1, nblk - 1), nblk - 1):
        ks = int(np.clip(blk * R - kh // 2, 0, rows - KR))
        r = blk * R + np.arange(R)
        rs = np.clip(r - kh // 2, 0, rows - kh)
        key = ks + np.arange(KR)
        row_ok = (key[None, :] >= rs[:, None]) & (key[None, :] < rs[:, None] + kh)
        dr = np.clip(key[None, :] - r[:, None] + NA_MAX_KH - 1, 0, 2 * NA_MAX_KH - 2)
        dr_all.append(dr)
        ok_all.append(row_ok)
    pad = W - NA_KW
    padded = jnp.pad(rpb.astype(F32) * LOG2E, ((0, 0), (0, 0), (pad, pad)))
    per_dr = jnp.stack([padded[:, :, W - 1 - q:2 * W - 1 - q] for q in range(W)], axis=2)
    per_dr = jnp.where(col_ok, per_dr, NEG_BIG)
    H = rpb.shape[0]

    def assemble(p_ref, o_ref):
        masked = jnp.full((W, W), NEG_BIG, F32)
        block = lambda v, r, k: p_ref[0, int(dr_all[v][r, k])] if ok_all[v][r, k] else masked
        for v in range(3):
            for r in range(R):
                for k in range(0, KR, 2):
                    o_ref[v, 0, r * W:(r + 1) * W, k * W:(k + 2) * W] = jnp.concatenate(
                        [block(v, r, k), block(v, r, k + 1)], axis=1)

    return pl.pallas_call(
        assemble,
        grid=(H,),
        in_specs=[pl.BlockSpec((1,) + per_dr.shape[1:], lambda h: (h, 0, 0, 0))],
        out_specs=pl.BlockSpec((3, 1, R * W, KR * W), lambda h: (0, h, 0, 0)),
        out_shape=jax.ShapeDtypeStruct((3, H, R * W, KR * W), F32),
        compiler_params=_params("parallel"),
        name="na_bias_table",
    )(per_dr)


def _na_kernel(q_ref, k_ref, v_ref, b_ref, o_ref, *, rows):
    R, KR, W = NA_ROWS_PER_STEP, NA_KEY_ROWS, GRID_W
    i = pl.program_id(2)
    ks = jnp.clip(i * R - min(NA_MAX_KH, rows) // 2, 0, rows - KR) * W
    ks = pl.multiple_of(ks, W)
    heads = [slice(h * HEAD_DIM, (h + 1) * HEAD_DIM) for h in range(NA_HEADS_PER_STEP)]
    keys = pl.ds(ks, KR * W)
    scores = [_dot_nt(q_ref[:, c], k_ref[keys, c]) + b_ref[0, h] for h, c in enumerate(heads)]
    probs = []
    for s in scores:
        p = jnp.exp2(s - jnp.max(s, axis=-1, keepdims=True))
        probs.append((p.astype(BF16), jnp.sum(p, axis=-1, keepdims=True)))
    for c, (p, l) in zip(heads, probs):
        o_ref[:, c] = (_dot(p, v_ref[keys, c]) / l).astype(o_ref.dtype)


def _neighbourhood_attention(proj, bias, B, L, mix_w):
    R, KR, W = NA_ROWS_PER_STEP, NA_KEY_ROWS, GRID_W
    rows = L // W
    nblk = rows // R
    hw = NA_HEADS_PER_STEP * HEAD_DIM
    ngrp = mix_w // hw
    assert rows % R == 0 and rows >= KR and nblk >= 2 and R >= NA_MAX_KH // 2

    def variant(i):
        return jnp.where(i == 0, 0, jnp.where(i == nblk - 1, 2, 1))

    return pl.pallas_call(
        functools.partial(_na_kernel, rows=rows),
        grid=(B, ngrp, nblk),
        in_specs=[
            pl.BlockSpec((R * W, hw), lambda b, g, i: (b * nblk + i, g)),
            pl.BlockSpec((L, hw), lambda b, g, i: (b, ngrp + g)),
            pl.BlockSpec((L, hw), lambda b, g, i: (b, 2 * ngrp + g)),
            pl.BlockSpec((1, NA_HEADS_PER_STEP, R * W, KR * W), lambda b, g, i: (variant(i), g, 0, 0)),
        ],
        out_specs=pl.BlockSpec((R * W, hw), lambda b, g, i: (b * nblk + i, g)),
        out_shape=jax.ShapeDtypeStruct((B * L, mix_w), BF16),
        compiler_params=_params("parallel", "parallel", "arbitrary"),
        name="neighbourhood_attention",
    )(proj, proj, proj, bias)


def _position_features(L):
    P = DFT_P
    pos = (jnp.arange(L // P, dtype=F32)[None, :] * P + jnp.arange(P, dtype=F32)[:, None]).reshape(L)
    t = pos / (L - 1)
    wpos = 2.0 * math.pi * pos / L
    fb = jnp.linspace(1e-4, FILTER_BANDS - 1, FILTER_BANDS, dtype=F32)
    ang = wpos[:, None] * fb[None, :]
    z = jnp.concatenate([t[:, None], jnp.cos(ang), -jnp.sin(ang)], axis=-1)
    return jnp.pad(z, ((0, 0), (0, LANES - FILTER_EMB)))


def _hy_filter_kernel(z_ref, w1_ref, b1_ref, w2_ref, b2_ref, w3_ref, fr_ref, dec_ref, h_ref, *, mix_w):
    hp = lax.Precision.HIGHEST
    fr = fr_ref[...]
    z = z_ref[...]
    h = jnp.sin(fr * (jnp.dot(z, w1_ref[...], precision=hp, preferred_element_type=F32) + b1_ref[...]))
    h = jnp.sin(fr * (jnp.dot(h, w2_ref[...], precision=hp, preferred_element_type=F32) + b2_ref[...]))

    def taps(d):
        w3 = w3_ref[:, d * mix_w:(d + 1) * mix_w]
        win = jnp.exp(-z[:, 0:1] * jnp.abs(dec_ref[d:d + 1, :]))
        return jnp.dot(h, w3, precision=hp, preferred_element_type=F32) * win

    tl = z_ref.shape[0]
    lag0 = pl.program_id(0) * tl + lax.broadcasted_iota(jnp.int32, (tl, 1), 0) == 0
    h_fwd, h_bwd = taps(0), taps(1)
    h_ref[0] = h_fwd + jnp.where(lag0, h_bwd, 0.0)
    h_ref[1] = jnp.where(lag0, 0.0, h_bwd)


def _hyena_filter_taps(L, w1, b1, w2, b2, w3, freq, decay, tl):
    mix_w = decay.shape[1]
    order = w2.shape[0]
    w1p = jnp.pad(w1.astype(F32), ((0, LANES - FILTER_EMB), (0, 0)))
    full = lambda shape: pl.BlockSpec(shape, lambda i: (0, 0))
    return pl.pallas_call(
        functools.partial(_hy_filter_kernel, mix_w=mix_w),
        grid=(L // tl,),
        in_specs=[
            pl.BlockSpec((tl, LANES), lambda i: (i, 0)),
            full((LANES, order)), full((1, order)), full((order, order)), full((1, order)),
            full((order, 2 * mix_w)), full((1, order)), full((2, mix_w)),
        ],
        out_specs=pl.BlockSpec((2, tl, mix_w), lambda i: (0, i, 0)),
        out_shape=jax.ShapeDtypeStruct((2, L, mix_w), F32),
        compiler_params=_params("parallel"),
        name="hyena_filter_taps",
    )(_position_features(L), w1p, b1.astype(F32)[None], w2.astype(F32), b2.astype(F32)[None],
      w3.astype(F32), freq.astype(F32)[None], decay.astype(F32))


def _dft_tables(L):
    P, MG = DFT_P, DFT_MG
    A, n = L // P, 2 * L
    Mm, J = 2 * A, P // 2
    NG = Mm // MG
    ar = lambda size: jnp.arange(size, dtype=jnp.int32)
    ang1 = (((2 * ar(Mm) + 1)[:, None] * ar(A)[None, :]) % (2 * Mm)).astype(F32) * (math.pi / Mm)
    w1 = jnp.concatenate([jnp.cos(ang1), -jnp.sin(ang1)], axis=0)
    k = (Mm * ar(J)[None, None, :, None] + MG * ar(NG)[:, None, None, None] + ar(MG)[None, :, None, None])
    ph = ((ar(P)[None, None, None, :] * (2 * k + 1)) % (2 * n)).astype(F32) * (math.pi / n)
    cr, ci = jnp.cos(ph), -jnp.sin(ph)
    c4 = jnp.stack([jnp.stack([cr, -ci], axis=3), jnp.stack([ci, cr], axis=3)], axis=2)
    c4 = c4.reshape(NG, MG, 2 * J, 2 * P)
    spread_cols = np.zeros((MG, 2 * P, 2 * P * MG), np.float32)
    spread_rows = np.zeros((MG, 2 * J, 2 * J * MG), np.float32)
    for m in range(MG):
        spread_cols[m, np.arange(2 * P), np.arange(2 * P) * MG + m] = 1.0
        spread_rows[m, np.arange(2 * J), np.arange(2 * J) * MG + m] = 1.0
    a3 = jnp.einsum('gmrc,mcd->grmd', c4, jnp.asarray(spread_cols))
    b3 = jnp.einsum('gmrc,mrd->gcmd', c4, jnp.asarray(spread_rows))
    a3 = a3.astype(BF16).reshape(NG, 2 * J * MG, 2 * P * MG)
    b3 = b3.astype(BF16).reshape(NG, 2 * P * MG, 2 * J * MG)
    return w1.astype(BF16), w1.T.astype(BF16), a3, b3


def _short_conv(ref, w_ref, b_ref, start, size, L):
    halo, tile = SUBLANES_BF16, SUBLANES_F32
    cur = ref[pl.ds(start, size), :].astype(F32)
    up, dn = pltpu.roll(cur, 1, axis=0), pltpu.roll(cur, size - 1, axis=0)
    prev_start = pl.multiple_of(jnp.maximum(start - halo, 0), halo)
    next_start = pl.multiple_of(jnp.minimum(start + size, L - halo), halo)
    prev = ref[pl.ds(prev_start, halo), :][halo - 1:halo, :].astype(F32)
    nxt = ref[pl.ds(next_start, halo), :][0:1, :].astype(F32)
    prev = jnp.where(start == 0, 0.0, prev)
    nxt = jnp.where(start + size == L, 0.0, nxt)
    row = lax.broadcasted_iota(jnp.int32, (tile, 1), 0)
    up = jnp.concatenate([jnp.where(row == 0, prev, up[:tile]), up[tile:]], axis=0)
    dn = jnp.concatenate([dn[:size - tile], jnp.where(row == tile - 1, nxt, dn[size - tile:])], axis=0)
    return up * w_ref[0:1, :] + cur * w_ref[1:2, :] + dn * w_ref[2:3, :] + b_ref[...]


def _hy_gate_kernel(x0_ref, x1_ref, v_ref, w0_ref, w1_ref, wv_ref, b0_ref, b1_ref, bv_ref,
                    perm_ref, x0c_ref, zb_ref, zp_ref):
    P, MG = DFT_P, DFT_MG
    chunk = P * MG
    L, cb = x0_ref.shape

    def body(c, carry):
        start = pl.multiple_of(c * chunk, chunk)
        rows = pl.ds(start, chunk)
        conv = functools.partial(_short_conv, start=start, size=chunk, L=L)
        x0c_ref[rows, :] = conv(x0_ref, w0_ref, b0_ref).astype(x0c_ref.dtype)
        x1c = conv(x1_ref, w1_ref, b1_ref)
        vc = conv(v_ref, wv_ref, bv_ref)
        z = (x1c * vc).astype(BF16)
        zb_ref[rows, :] = z
        zp = _dot(perm_ref[...], z).reshape(P, MG, cb)
        zp_ref[:, pl.ds(pl.multiple_of(c * MG, MG), MG), :] = zp
        return carry

    lax.fori_loop(0, L // chunk, body, 0, unroll=2)


def _time_split_permutation():
    P, MG = DFT_P, DFT_MG
    r = np.arange(P * MG)
    perm = np.zeros((P * MG, P * MG), np.float32)
    perm[r, (r % MG) * P + r // MG] = 1.0
    return perm


def _hyena_gate(proj, short_w, short_b, B, L, mix_w, cb):
    nc = mix_w // cb
    P = DFT_P
    col = lambda part: pl.BlockSpec((L, cb), lambda b, c: (b, part * nc + c))
    wspec = lambda part: pl.BlockSpec((3, cb), lambda b, c: (0, part * nc + c))
    bspec = lambda part: pl.BlockSpec((1, cb), lambda b, c: (0, part * nc + c))
    perm = jnp.asarray(_time_split_permutation(), BF16)
    seq = pl.BlockSpec((None, None, L, cb), lambda b, c: (b, c, 0, 0))
    return pl.pallas_call(
        _hy_gate_kernel,
        grid=(B, nc),
        in_specs=[col(0), col(1), col(2), wspec(0), wspec(1), wspec(2), bspec(0), bspec(1), bspec(2),
                  pl.BlockSpec(perm.shape, lambda b, c: (0, 0))],
        out_specs=[seq, seq, pl.BlockSpec((None, None, P, L // P, cb), lambda b, c: (b, c, 0, 0, 0))],
        out_shape=[jax.ShapeDtypeStruct((B, nc, L, cb), BF16), jax.ShapeDtypeStruct((B, nc, L, cb), BF16),
                   jax.ShapeDtypeStruct((B, nc, P, L // P, cb), F32)],
        compiler_params=_params("parallel", "parallel"),
        name="hyena_gate",
    )(proj, proj, proj, short_w, short_w, short_w, short_b, short_b, short_b, perm)


def _dft_fwd_kernel(*refs, filtered):
    if filtered:
        z_ref, w1_ref, a3_ref, g_ref, y_ref, v_ref = refs
    else:
        z_ref, w1_ref, a3_ref, y_ref, v_ref = refs
    P, MG = DFT_P, DFT_MG
    cb = z_ref.shape[2]
    ng, rows_g = a3_ref.shape[0], a3_ref.shape[1]
    half = rows_g // 2
    w1 = w1_ref[...]
    for b in range(P):
        v = _dot(w1, z_ref[b].astype(BF16))
        v_ref[:, :, b * MG:(b + 1) * MG, :] = v.reshape(2, ng, MG, cb)

    if not filtered:
        backward = pl.program_id(1) == 1
        scale = 2.0 / y_ref.shape[0]
        im_scale = jnp.where(backward, -scale, scale)

        @pl.when(jnp.logical_not(backward))
        def _():
            y_ref[...] = jnp.zeros_like(y_ref)

    def group(g, carry):
        vg = jnp.concatenate([v_ref[0, g], v_ref[1, g]], axis=0).astype(BF16)
        x = _dot(a3_ref[g], vg)
        rows = pl.ds(pl.multiple_of(g * rows_g, rows_g), rows_g)
        xr, xi = x[:half], x[half:]
        if filtered:
            gr, gi = g_ref[g, :half], g_ref[g, half:]
            y_ref[rows, :] = jnp.concatenate([xr * gr - xi * gi, xr * gi + xi * gr], axis=0).astype(y_ref.dtype)
        else:
            y_ref[rows, :] += jnp.concatenate([xr * scale, xi * im_scale], axis=0)
        return carry

    lax.fori_loop(0, ng, group, 0, unroll=8)


def _dft_forward(z, w1, a3, g, Bz, L, cb, out_dtype):
    ng, rows_g, cols_g = a3.shape
    const = lambda shape: pl.BlockSpec(shape, lambda c, b: (0,) * len(shape), pipeline_mode=pl.Buffered(1))
    P, A = DFT_P, L // DFT_P
    if g is not None:
        nc = z.shape[1]
        in_specs = [pl.BlockSpec((None, None, P, A, cb), lambda c, b: (b, c, 0, 0, 0)), const(w1.shape),
                    const(a3.shape),
                    pl.BlockSpec((ng, rows_g, cb), lambda c, b: (0, 0, c), pipeline_mode=pl.Buffered(1))]
        args = [z, w1, a3, g]
        out_spec = pl.BlockSpec((None, None, 2 * L, cb), lambda c, b: (b, c, 0, 0))
        out_shape = (Bz, nc, 2 * L, cb)
    else:
        assert Bz == 2
        nc = z.shape[3] // cb
        in_specs = [pl.BlockSpec((None, P, A, cb), lambda c, b: (b, 0, 0, c)), const(w1.shape), const(a3.shape)]
        args = [z, w1, a3]
        out_spec = pl.BlockSpec((None, 2 * L, cb), lambda c, b: (0, 0, c))
        out_shape = (1, 2 * L, z.shape[3])
    return pl.pallas_call(
        functools.partial(_dft_fwd_kernel, filtered=g is not None),
        grid=(nc, Bz),
        in_specs=in_specs,
        out_specs=out_spec,
        out_shape=jax.ShapeDtypeStruct(out_shape, out_dtype),
        scratch_shapes=[pltpu.VMEM((2, ng, cols_g // 2, cb), F32)],
        compiler_params=_params("parallel", "arbitrary"),
        name="hyena_dft_forward",
    )(*args)


def _dft_inv_kernel(y_ref, b3_ref, w1t_ref, zb_ref, x0_ref, d_ref, unperm_ref, o_ref, q_ref, t_ref):
    P, MG = DFT_P, DFT_MG
    chunk = P * MG
    cb = y_ref.shape[1]
    ng, rows_g, cols_g = b3_ref.shape
    half = rows_g // 2

    def group(g, carry):
        yg = y_ref[pl.ds(pl.multiple_of(g * cols_g, cols_g), cols_g), :]
        q = _dot(b3_ref[g], yg)
        q_ref[0, g] = q[:half]
        q_ref[1, g] = q[half:]
        return carry

    lax.fori_loop(0, ng, group, 0, unroll=8)
    w1t = w1t_ref[...]
    for b in range(P):
        qb = q_ref[:, :, b * MG:(b + 1) * MG, :].reshape(2 * ng * MG, cb)
        t_ref[b] = _dot(w1t, qb.astype(BF16))

    unperm = unperm_ref[...]

    def gate(c, carry):
        rows = pl.ds(pl.multiple_of(c * chunk, chunk), chunk)
        yp = t_ref[:, pl.ds(pl.multiple_of(c * MG, MG), MG), :].reshape(chunk, cb)
        hi = yp.astype(BF16)
        lo = (yp - hi.astype(F32)).astype(BF16)
        y = _dot(unperm, hi) + _dot(unperm, lo)
        y = y + d_ref[...] * zb_ref[rows, :].astype(F32)
        o_ref[rows, :] = (x0_ref[rows, :].astype(F32) * y).astype(o_ref.dtype)
        return carry

    lax.fori_loop(0, zb_ref.shape[0] // chunk, gate, 0, unroll=4)


def _dft_inverse(y, b3, w1t, zb, x0c, d, B, L, cb):
    nc = zb.shape[1]
    ng, rows_g, _ = b3.shape
    const = lambda shape: pl.BlockSpec(shape, lambda c, b: (0,) * len(shape), pipeline_mode=pl.Buffered(1))
    blocked = lambda rows: pl.BlockSpec((None, None, rows, cb), lambda c, b: (b, c, 0, 0))
    unperm = jnp.asarray(_time_split_permutation().T, BF16)
    return pl.pallas_call(
        _dft_inv_kernel,
        grid=(nc, B),
        in_specs=[blocked(2 * L), const(b3.shape), const(w1t.shape), blocked(L), blocked(L),
                  pl.BlockSpec((1, cb), lambda c, b: (0, c)), const(unperm.shape)],
        out_specs=pl.BlockSpec((L, cb), lambda c, b: (b, c)),
        out_shape=jax.ShapeDtypeStruct((B * L, nc * cb), BF16),
        scratch_shapes=[pltpu.VMEM((2, ng, rows_g // 2, cb), F32),
                        pltpu.VMEM((DFT_P, L // DFT_P, cb), F32)],
        compiler_params=_params("parallel", "arbitrary"),
        name="hyena_dft_inverse",
    )(y, b3, w1t, zb, x0c, d, unperm)


def _hyena(proj, B, L, mix_w, short_w, short_b, w1, b1, w2, b2, w3, freq, decay, d_bias):
    tiles = _tiles(L)
    cb = tiles.hyena_cols
    dw1, dw1t, a3, b3 = _dft_tables(L)
    ng, rows_g, _ = a3.shape
    taps = _hyena_filter_taps(L, w1, b1, w2, b2, w3, freq, decay, tl=tiles.filter_rows)
    g = _dft_forward(taps.reshape(2, DFT_P, L // DFT_P, mix_w), dw1, a3, None, 2, L, cb, F32)
    g = g.reshape(ng, rows_g, mix_w)
    x0c, zb, zp = _hyena_gate(proj, short_w.astype(F32), short_b.astype(F32)[None], B, L, mix_w, cb)
    y = _dft_forward(zp, dw1, a3, g, B, L, cb, BF16)
    return _dft_inverse(y, b3, dw1t, zb, x0c, d_bias.astype(F32)[None], B, L, cb)


def _mem_kv_kernel(m_ref, g_ref, w_ref, o_ref):
    h = _rmsnorm(m_ref[...], g_ref[...]).astype(BF16)
    o_ref[...] = _dot(h, w_ref[...]).astype(o_ref.dtype)


def _mem_kv(mem, g, w, layer, M):
    BM, D = mem.shape
    N = w.shape[2]
    return pl.pallas_call(
        _mem_kv_kernel,
        grid=(BM // M,),
        in_specs=[
            pl.BlockSpec((M, D), lambda b: (b, 0)),
            pl.BlockSpec((1, D), lambda b: (0, 0)),
            pl.BlockSpec((None, D, N), lambda b: (layer, 0, 0)),
        ],
        out_specs=pl.BlockSpec((M, N), lambda b: (b, 0)),
        out_shape=jax.ShapeDtypeStruct((BM, N), BF16),
        compiler_params=_params("parallel"),
        name="memory_kv",
    )(mem, g, w)


def _mix_out_kernel(tok_ref, q_ref, kv_ref, w_ref, x_ref, g_ref, o_ref, mo_ref):
    mem_w = q_ref.shape[1]
    mix_w = tok_ref.shape[1]
    heads = [slice(h * HEAD_DIM, (h + 1) * HEAD_DIM) for h in range(mem_w // HEAD_DIM)]
    scores = [_dot_nt(q_ref[:, c], kv_ref[:, c]) for c in heads]
    o = _dot(tok_ref[...], w_ref[:mix_w, :])
    probs = []
    for s in scores:
        p = jnp.exp2(s - jnp.max(s, axis=-1, keepdims=True))
        probs.append((p.astype(BF16), jnp.sum(p, axis=-1, keepdims=True)))
    for c, (p, l) in zip(heads, probs):
        vcols = slice(mem_w + c.start, mem_w + c.stop)
        mo_ref[:, c] = (_dot(p, kv_ref[:, vcols]) / l).astype(mo_ref.dtype)
    o = o + _dot(mo_ref[...], w_ref[mix_w:, :])
    o_ref[...] = x_ref[...] + _rmsnorm(o, g_ref[...])


def _mix_out(tok, proj, kv, w_out, layer, x, g, L, tm):
    T, D = x.shape
    mix_w = tok.shape[1]
    mem_w = w_out.shape[1] - mix_w
    M = kv.shape[0] // (T // L)
    qblk = (proj.shape[1] - mem_w) // mem_w
    per_b = L // tm
    return pl.pallas_call(
        _mix_out_kernel,
        grid=(T // tm,),
        in_specs=[
            pl.BlockSpec((tm, mix_w), lambda i: (i, 0)),
            pl.BlockSpec((tm, mem_w), lambda i: (i, qblk)),
            pl.BlockSpec((M, 2 * mem_w), lambda i: (i // per_b, 0)),
            pl.BlockSpec((None, mix_w + mem_w, D), lambda i: (layer, 0, 0)),
            pl.BlockSpec((tm, D), lambda i: (i, 0)),
            pl.BlockSpec((1, D), lambda i: (0, 0)),
        ],
        out_specs=pl.BlockSpec((tm, D), lambda i: (i, 0)),
        out_shape=jax.ShapeDtypeStruct((T, D), F32),
        scratch_shapes=[pltpu.VMEM((tm, mem_w), BF16)],
        compiler_params=_params("parallel"),
        name="mix_out_proj",
    )(tok, proj, kv, w_out, x, g)


def _mlp_kernel(x_ref, gpre_ref, wu_ref, wd_ref, gpost_ref, o_ref, h_ref, acc_ref):
    f = pl.program_id(1)
    nf = pl.num_programs(1)
    tm = x_ref.shape[0]

    @pl.when(f == 0)
    def _():
        g = gpre_ref[...]
        for r in range(0, tm, NORM_ROWS):
            rows = slice(r, r + NORM_ROWS)
            h_ref[rows, :] = _rmsnorm(x_ref[rows, :], g).astype(h_ref.dtype)
        acc_ref[...] = jnp.zeros_like(acc_ref)

    h = h_ref[...]
    tf = wu_ref.shape[1]
    cw = tf // MLP_CHUNKS
    chunks = [slice(c * cw, (c + 1) * cw) for c in range(MLP_CHUNKS)]
    ups = [_dot(h, wu_ref[:, c]) for c in chunks]
    acts = [jnp.square(jnp.maximum(u, 0.0)).astype(BF16) for u in ups]
    d = acc_ref[...]
    for a, c in zip(acts, chunks):
        d = d + _dot(a, wd_ref[c, :])
    acc_ref[...] = d

    @pl.when(f == nf - 1)
    def _():
        g = gpost_ref[...]
        for r in range(0, tm, NORM_ROWS):
            rows = slice(r, r + NORM_ROWS)
            o_ref[rows, :] = x_ref[rows, :] + _rmsnorm(acc_ref[rows, :], g)


def _mlp(x, gpre, w_up, w_down, layer, gpost, tm, tf):
    T, D = x.shape
    F = w_up.shape[2]
    return pl.pallas_call(
        _mlp_kernel,
        grid=(T // tm, F // tf),
        in_specs=[
            pl.BlockSpec((tm, D), lambda i, f: (i, 0)),
            pl.BlockSpec((1, D), lambda i, f: (0, 0)),
            pl.BlockSpec((None, D, tf), lambda i, f: (layer, 0, f)),
            pl.BlockSpec((None, tf, D), lambda i, f: (layer, f, 0)),
            pl.BlockSpec((1, D), lambda i, f: (0, 0)),
        ],
        out_specs=pl.BlockSpec((tm, D), lambda i, f: (i, 0)),
        out_shape=jax.ShapeDtypeStruct((T, D), F32),
        scratch_shapes=[pltpu.VMEM((tm, D), BF16), pltpu.VMEM((tm, D), F32)],
        compiler_params=_params("parallel", "arbitrary"),
        name="relu2_mlp",
    )(x, gpre, w_up, w_down, gpost)


def _trunk(x, mem, wts):
    (g_mix_pre, w_in, na_rpb, hy_short_w, hy_short_b, hy_w1, hy_b1, hy_w2, hy_b2, hy_w3, hy_freq,
     hy_decay, hy_d, g_mem, w_mem_kv, w_out, g_mix_post, g_mlp_pre, w_up, w_down, g_mlp_post) = wts
    B, L, D = x.shape
    depth = w_in.shape[0]
    in_w = w_in.shape[2]
    mem_w = w_mem_kv.shape[2] // 2
    mix_w = (in_w - mem_w) // 3
    x = x.reshape(B * L, D)
    mem = mem.reshape(-1, D)
    tiles = _tiles(L)
    qk_scale = HEAD_DIM ** -0.5 * LOG2E
    row = lambda v: v.astype(F32)[None]
    for i in range(depth):
        j = i // 2
        na_layer = i % 2 == 0
        colscale = jnp.concatenate([
            jnp.full((mix_w,), qk_scale if na_layer else 1.0, F32),
            jnp.ones((2 * mix_w,), F32),
            jnp.full((mem_w,), qk_scale, F32)])[None]
        proj = _norm_matmul(x, row(g_mix_pre[i]), w_in, i, colscale, tm=tiles.proj_rows, tn=tiles.proj_cols)
        if na_layer:
            bias = _na_bias_table(na_rpb[j], L // GRID_W)
            tok = _neighbourhood_attention(proj, bias, B, L, mix_w)
        else:
            tok = _hyena(proj, B, L, mix_w, hy_short_w[j], hy_short_b[j], hy_w1[j], hy_b1[j], hy_w2[j],
                         hy_b2[j], hy_w3[j], hy_freq[j], hy_decay[j], hy_d[j])
        kv = _mem_kv(mem, row(g_mem[i]), w_mem_kv, i, min(tiles.kv_rows, mem.shape[0]))
        x = _mix_out(tok, proj, kv, w_out, i, x, row(g_mix_post[i]), L, tm=tiles.mix_rows)
        x = _mlp(x, row(g_mlp_pre[i]), w_up, w_down, i, row(g_mlp_post[i]), tm=tiles.mlp_rows, tf=tiles.mlp_cols)
    return x.reshape(B, L, D)


def kernel(x_prompt, x_sample, mem_prompt, mem_sample, g_mix_pre, w_in, na_rpb, hy_short_w, hy_short_b,
           hy_w1, hy_b1, hy_w2, hy_b2, hy_w3, hy_freq, hy_decay, hy_d, g_mem, w_mem_kv, w_out,
           g_mix_post, g_mlp_pre, w_up, w_down, g_mlp_post):
    bf = lambda w: w.astype(BF16)
    wts = (g_mix_pre, bf(w_in), na_rpb, hy_short_w, hy_short_b, hy_w1, hy_b1, hy_w2, hy_b2, hy_w3, hy_freq,
           hy_decay, hy_d, g_mem, bf(w_mem_kv), bf(w_out), g_mix_post, g_mlp_pre, bf(w_up), bf(w_down), g_mlp_post)
    return (_trunk(x_prompt, mem_prompt, wts), _trunk(x_sample, mem_sample, wts))
```

```python
import functools
import math
from typing import NamedTuple

import numpy as np
import jax
import jax.numpy as jnp
from jax import lax
from jax.experimental import pallas as pl
from jax.experimental.pallas import tpu as pltpu

F32 = jnp.float32
BF16 = jnp.bfloat16

GRID_W = 64
HEAD_DIM = 128
NA_MAX_KH = 8
NA_KW = 16
FILTER_EMB = 33
FILTER_BANDS = (FILTER_EMB - 1) // 2
RMS_EPS = 1e-6
NEG_BIG = -1e30
LOG2E = math.log2(math.e)

VMEM_LIMIT_BYTES = 56 * 1024 * 1024
LANES = 128
SUBLANES_F32 = 8
SUBLANES_BF16 = 16

NA_ROWS_PER_STEP = 4
NA_HEADS_PER_STEP = 6
NA_KEY_ROWS = NA_ROWS_PER_STEP + NA_MAX_KH
MLP_CHUNKS = 2
NORM_ROWS = 64
DFT_P = 32
DFT_MG = SUBLANES_F32
HYENA_LANE_ELEMS = 1 << 20


class _Tiles(NamedTuple):
    proj_rows: int
    proj_cols: int
    mix_rows: int
    mlp_rows: int
    mlp_cols: int
    filter_rows: int
    hyena_cols: int
    kv_rows: int


def _tiles(L):
    return _Tiles(proj_rows=min(1024, L), proj_cols=1280, mix_rows=min(512, L), mlp_rows=min(512, L),
                  mlp_cols=1024, filter_rows=min(512, L), hyena_cols=min(512, HYENA_LANE_ELEMS // L),
                  kv_rows=1024)


def _params(*sem):
    return pltpu.CompilerParams(dimension_semantics=sem, vmem_limit_bytes=VMEM_LIMIT_BYTES)


def _rmsnorm(x, g):
    ms = jnp.mean(x * x, axis=-1, keepdims=True)
    return x * lax.rsqrt(ms + RMS_EPS) * g


def _dot(a, b):
    return jnp.dot(a, b, preferred_element_type=F32)


def _dot_nt(a, b):
    return lax.dot_general(a, b, (((1,), (1,)), ((), ())), preferred_element_type=F32)


def _norm_matmul_kernel(x_ref, g_ref, w_ref, s_ref, o_ref, h0_ref, h1_ref, *, nt, nj):
    s, j = pl.program_id(0), pl.program_id(1)
    part = x_ref.shape[0] // nj
    rows = pl.ds(pl.multiple_of(j * part, part), part)

    def normalise_into(h_ref):
        h_ref[rows, :] = _rmsnorm(x_ref[rows, :], g_ref[...]).astype(h_ref.dtype)

    def project(h_ref):
        o_ref[...] = (_dot(h_ref[...], w_ref[...]) * s_ref[...]).astype(o_ref.dtype)

    even = s % 2 == 0

    @pl.when(s == 0)
    def _():
        normalise_into(h0_ref)

    @pl.when((s > 0) & (s < nt) & even)
    def _():
        project(h1_ref)
        normalise_into(h0_ref)

    @pl.when((s < nt) & jnp.logical_not(even))
    def _():
        project(h0_ref)
        normalise_into(h1_ref)

    @pl.when(s == nt)
    def _():
        project(h0_ref if nt % 2 == 1 else h1_ref)


def _norm_matmul(x, g, w, layer, colscale, tm, tn):
    T, D = x.shape
    N = w.shape[2]
    nt, nj = T // tm, N // tn
    assert tm % (nj * SUBLANES_BF16) == 0
    return pl.pallas_call(
        functools.partial(_norm_matmul_kernel, nt=nt, nj=nj),
        grid=(nt + 1, nj),
        in_specs=[
            pl.BlockSpec((tm, D), lambda s, j: (jnp.minimum(s, nt - 1), 0)),
            pl.BlockSpec((1, D), lambda s, j: (0, 0)),
            pl.BlockSpec((None, D, tn), lambda s, j: (layer, 0, j)),
            pl.BlockSpec((1, tn), lambda s, j: (0, j)),
        ],
        out_specs=pl.BlockSpec((tm, tn), lambda s, j: (jnp.maximum(s - 1, 0), jnp.where(s == 0, 0, j))),
        out_shape=jax.ShapeDtypeStruct((T, N), BF16),
        scratch_shapes=[pltpu.VMEM((tm, D), BF16), pltpu.VMEM((tm, D), BF16)],
        compiler_params=_params("arbitrary", "arbitrary"),
        name="norm_in_proj",
    )(x, g, w, colscale)


def _na_bias_table(rpb, rows):
    R, KR, W = NA_ROWS_PER_STEP, NA_KEY_ROWS, GRID_W
    kh = min(NA_MAX_KH, rows)
    nblk = rows // R
    qc = np.arange(W)
    cs = np.clip(qc - NA_KW // 2, 0, W - NA_KW)
    kc = np.arange(W)
    col_ok = (kc[None, :] >= cs[:, None]) & (kc[None, :] < cs[:, None] + NA_KW)
    dr_all, ok_all = [], []
    for blk in (0, min(1, nblk - 1), nblk - 1):
        ks = int(np.clip(blk * R - kh // 2, 0, rows - KR))
        r = blk * R + np.arange(R)
        rs = np.clip(r - kh // 2, 0, rows - kh)
        key = ks + np.arange(KR)
        row_ok = (key[None, :] >= rs[:, None]) & (key[None, :] < rs[:, None] + kh)
        dr = np.clip(key[None, :] - r[:, None] + NA_MAX_KH - 1, 0, 2 * NA_MAX_KH - 2)
        dr_all.append(dr)
        ok_all.append(row_ok)
    pad = W - NA_KW
    padded = jnp.pad(rpb.astype(F32) * LOG2E, ((0, 0), (0, 0), (pad, pad)))
    per_dr = jnp.stack([padded[:, :, W - 1 - q:2 * W - 1 - q] for q in range(W)], axis=2)
    per_dr = jnp.where(col_ok, per_dr, NEG_BIG)
    H = rpb.shape[0]

    def assemble(p_ref, o_ref):
        masked = jnp.full((W, W), NEG_BIG, F32)
        block = lambda v, r, k: p_ref[0, int(dr_all[v][r, k])] if ok_all[v][r, k] else masked
        for v in range(3):
            for r in range(R):
                for k in range(0, KR, 2):
                    o_ref[v, 0, r * W:(r + 1) * W, k * W:(k + 2) * W] = jnp.concatenate(
                        [block(v, r, k), block(v, r, k + 1)], axis=1)

    return pl.pallas_call(
        assemble,
        grid=(H,),
        in_specs=[pl.BlockSpec((1,) + per_dr.shape[1:], lambda h: (h, 0, 0, 0))],
        out_specs=pl.BlockSpec((3, 1, R * W, KR * W), lambda h: (0, h, 0, 0)),
        out_shape=jax.ShapeDtypeStruct((3, H, R * W, KR * W), F32),
        compiler_params=_params("parallel"),
        name="na_bias_table",
    )(per_dr)


def _na_kernel(q_ref, k_ref, v_ref, b_ref, o_ref, *, rows):
    R, KR, W = NA_ROWS_PER_STEP, NA_KEY_ROWS, GRID_W
    i = pl.program_id(2)
    ks = jnp.clip(i * R - min(NA_MAX_KH, rows) // 2, 0, rows - KR) * W
    ks = pl.multiple_of(ks, W)
    heads = [slice(h * HEAD_DIM, (h + 1) * HEAD_DIM) for h in range(NA_HEADS_PER_STEP)]
    keys = pl.ds(ks, KR * W)
    scores = [_dot_nt(q_ref[:, c], k_ref[keys, c]) + b_ref[0, h] for h, c in enumerate(heads)]
    probs = []
    for s in scores:
        p = jnp.exp2(s - jnp.max(s, axis=-1, keepdims=True))
        probs.append((p.astype(BF16), jnp.sum(p, axis=-1, keepdims=True)))
    for c, (p, l) in zip(heads, probs):
        o_ref[:, c] = (_dot(p, v_ref[keys, c]) / l).astype(o_ref.dtype)


def _neighbourhood_attention(proj, bias, B, L, mix_w):
    R, KR, W = NA_ROWS_PER_STEP, NA_KEY_ROWS, GRID_W
    rows = L // W
    nblk = rows // R
    hw = NA_HEADS_PER_STEP * HEAD_DIM
    ngrp = mix_w // hw
    assert rows % R == 0 and rows >= KR and nblk >= 2 and R >= NA_MAX_KH // 2

    def variant(i):
        return jnp.where(i == 0, 0, jnp.where(i == nblk - 1, 2, 1))

    return pl.pallas_call(
        functools.partial(_na_kernel, rows=rows),
        grid=(B, ngrp, nblk),
        in_specs=[
            pl.BlockSpec((R * W, hw), lambda b, g, i: (b * nblk + i, g)),
            pl.BlockSpec((L, hw), lambda b, g, i: (b, ngrp + g)),
            pl.BlockSpec((L, hw), lambda b, g, i: (b, 2 * ngrp + g)),
            pl.BlockSpec((1, NA_HEADS_PER_STEP, R * W, KR * W), lambda b, g, i: (variant(i), g, 0, 0)),
        ],
        out_specs=pl.BlockSpec((R * W, hw), lambda b, g, i: (b * nblk + i, g)),
        out_shape=jax.ShapeDtypeStruct((B * L, mix_w), BF16),
        compiler_params=_params("parallel", "parallel", "arbitrary"),
        name="neighbourhood_attention",
    )(proj, proj, proj, bias)


def _position_features(L):
    P = DFT_P
    pos = (jnp.arange(L // P, dtype=F32)[None, :] * P + jnp.arange(P, dtype=F32)[:, None]).reshape(L)
    t = pos / (L - 1)
    wpos = 2.0 * math.pi * pos / L
    fb = jnp.linspace(1e-4, FILTER_BANDS - 1, FILTER_BANDS, dtype=F32)
    ang = wpos[:, None] * fb[None, :]
    z = jnp.concatenate([t[:, None], jnp.cos(ang), -jnp.sin(ang)], axis=-1)
    return jnp.pad(z, ((0, 0), (0, LANES - FILTER_EMB)))


def _hy_filter_kernel(z_ref, w1_ref, b1_ref, w2_ref, b2_ref, w3_ref, fr_ref, dec_ref, h_ref, *, mix_w):
    hp = lax.Precision.HIGHEST
    fr = fr_ref[...]
    z = z_ref[...]
    h = jnp.sin(fr * (jnp.dot(z, w1_ref[...], precision=hp, preferred_element_type=F32) + b1_ref[...]))
    h = jnp.sin(fr * (jnp.dot(h, w2_ref[...], precision=hp, preferred_element_type=F32) + b2_ref[...]))

    def taps(d):
        w3 = w3_ref[:, d * mix_w:(d + 1) * mix_w]
        win = jnp.exp(-z[:, 0:1] * jnp.abs(dec_ref[d:d + 1, :]))
        return jnp.dot(h, w3, precision=hp, preferred_element_type=F32) * win

    tl = z_ref.shape[0]
    lag0 = pl.program_id(0) * tl + lax.broadcasted_iota(jnp.int32, (tl, 1), 0) == 0
    h_fwd, h_bwd = taps(0), taps(1)
    h_ref[0] = h_fwd + jnp.where(lag0, h_bwd, 0.0)
    h_ref[1] = jnp.where(lag0, 0.0, h_bwd)


def _hyena_filter_taps(L, w1, b1, w2, b2, w3, freq, decay, tl):
    mix_w = decay.shape[1]
    order = w2.shape[0]
    w1p = jnp.pad(w1.astype(F32), ((0, LANES - FILTER_EMB), (0, 0)))
    full = lambda shape: pl.BlockSpec(shape, lambda i: (0, 0))
    return pl.pallas_call(
        functools.partial(_hy_filter_kernel, mix_w=mix_w),
        grid=(L // tl,),
        in_specs=[
            pl.BlockSpec((tl, LANES), lambda i: (i, 0)),
            full((LANES, order)), full((1, order)), full((order, order)), full((1, order)),
            full((order, 2 * mix_w)), full((1, order)), full((2, mix_w)),
        ],
        out_specs=pl.BlockSpec((2, tl, mix_w), lambda i: (0, i, 0)),
        out_shape=jax.ShapeDtypeStruct((2, L, mix_w), F32),
        compiler_params=_params("parallel"),
        name="hyena_filter_taps",
    )(_position_features(L), w1p, b1.astype(F32)[None], w2.astype(F32), b2.astype(F32)[None],
      w3.astype(F32), freq.astype(F32)[None], decay.astype(F32))


def _dft_tables(L):
    P, MG = DFT_P, DFT_MG
    A, n = L // P, 2 * L
    Mm, J = 2 * A, P // 2
    NG = Mm // MG
    ar = lambda size: jnp.arange(size, dtype=jnp.int32)
    ang1 = (((2 * ar(Mm) + 1)[:, None] * ar(A)[None, :]) % (2 * Mm)).astype(F32) * (math.pi / Mm)
    w1 = jnp.concatenate([jnp.cos(ang1), -jnp.sin(ang1)], axis=0)
    k = (Mm * ar(J)[None, None, :, None] + MG * ar(NG)[:, None, None, None] + ar(MG)[None, :, None, None])
    ph = ((ar(P)[None, None, None, :] * (2 * k + 1)) % (2 * n)).astype(F32) * (math.pi / n)
    cr, ci = jnp.cos(ph), -jnp.sin(ph)
    c4 = jnp.stack([jnp.stack([cr, -ci], axis=3), jnp.stack([ci, cr], axis=3)], axis=2)
    c4 = c4.reshape(NG, MG, 2 * J, 2 * P)
    spread_cols = np.zeros((MG, 2 * P, 2 * P * MG), np.float32)
    spread_rows = np.zeros((MG, 2 * J, 2 * J * MG), np.float32)
    for m in range(MG):
        spread_cols[m, np.arange(2 * P), np.arange(2 * P) * MG + m] = 1.0
        spread_rows[m, np.arange(2 * J), np.arange(2 * J) * MG + m] = 1.0
    a3 = jnp.einsum('gmrc,mcd->grmd', c4, jnp.asarray(spread_cols))
    b3 = jnp.einsum('gmrc,mrd->gcmd', c4, jnp.asarray(spread_rows))
    a3 = a3.astype(BF16).reshape(NG, 2 * J * MG, 2 * P * MG)
    b3 = b3.astype(BF16).reshape(NG, 2 * P * MG, 2 * J * MG)
    return w1.astype(BF16), w1.T.astype(BF16), a3, b3


def _short_conv(ref, w_ref, b_ref, shift_ref, start, size, L):
    halo, tile = SUBLANES_BF16, SUBLANES_F32
    cur_b = ref[pl.ds(start, size), :]
    cur = cur_b.astype(F32)
    up, dn = _dot(shift_ref[...], cur_b), pltpu.roll(cur, size - 1, axis=0)
    prev_start = pl.multiple_of(jnp.maximum(start - halo, 0), halo)
    next_start = pl.multiple_of(jnp.minimum(start + size, L - halo), halo)
    prev = ref[pl.ds(prev_start, halo), :][halo - 1:halo, :].astype(F32)
    nxt = ref[pl.ds(next_start, halo), :][0:1, :].astype(F32)
    prev = jnp.where(start == 0, 0.0, prev)
    nxt = jnp.where(start + size == L, 0.0, nxt)
    row = lax.broadcasted_iota(jnp.int32, (tile, 1), 0)
    up = jnp.concatenate([jnp.where(row == 0, prev, up[:tile]), up[tile:]], axis=0)
    dn = jnp.concatenate([dn[:size - tile], jnp.where(row == tile - 1, nxt, dn[size - tile:])], axis=0)
    return up * w_ref[0:1, :] + cur * w_ref[1:2, :] + dn * w_ref[2:3, :] + b_ref[...]


def _hy_gate_kernel(x0_ref, x1_ref, v_ref, w0_ref, w1_ref, wv_ref, b0_ref, b1_ref, bv_ref,
                    perm_ref, shift_ref, x0c_ref, zb_ref, zp_ref):
    P, MG = DFT_P, DFT_MG
    chunk = P * MG
    L, cb = x0_ref.shape

    def body(c, carry):
        start = pl.multiple_of(c * chunk, chunk)
        rows = pl.ds(start, chunk)
        conv = functools.partial(_short_conv, shift_ref=shift_ref, start=start, size=chunk, L=L)
        x0c_ref[rows, :] = conv(x0_ref, w0_ref, b0_ref).astype(x0c_ref.dtype)
        x1c = conv(x1_ref, w1_ref, b1_ref)
        vc = conv(v_ref, wv_ref, bv_ref)
        z = (x1c * vc).astype(BF16)
        zb_ref[rows, :] = z
        zp = _dot(perm_ref[...], z).reshape(P, MG, cb)
        zp_ref[:, pl.ds(pl.multiple_of(c * MG, MG), MG), :] = zp
        return carry

    lax.fori_loop(0, L // chunk, body, 0, unroll=4)


def _time_split_permutation():
    P, MG = DFT_P, DFT_MG
    r = np.arange(P * MG)
    perm = np.zeros((P * MG, P * MG), np.float32)
    perm[r, (r % MG) * P + r // MG] = 1.0
    return perm


def _hyena_gate(proj, short_w, short_b, B, L, mix_w, cb):
    nc = mix_w // cb
    P = DFT_P
    col = lambda part: pl.BlockSpec((L, cb), lambda b, c: (b, part * nc + c))
    wspec = lambda part: pl.BlockSpec((3, cb), lambda b, c: (0, part * nc + c))
    bspec = lambda part: pl.BlockSpec((1, cb), lambda b, c: (0, part * nc + c))
    perm = jnp.asarray(_time_split_permutation(), BF16)
    shift = jnp.asarray(np.eye(P * DFT_MG, k=-1, dtype=np.float32), BF16)
    seq = pl.BlockSpec((None, None, L, cb), lambda b, c: (b, c, 0, 0))
    return pl.pallas_call(
        _hy_gate_kernel,
        grid=(B, nc),
        in_specs=[col(0), col(1), col(2), wspec(0), wspec(1), wspec(2), bspec(0), bspec(1), bspec(2),
                  pl.BlockSpec(perm.shape, lambda b, c: (0, 0)), pl.BlockSpec(shift.shape, lambda b, c: (0, 0))],
        out_specs=[seq, seq, pl.BlockSpec((None, None, P, L // P, cb), lambda b, c: (b, c, 0, 0, 0))],
        out_shape=[jax.ShapeDtypeStruct((B, nc, L, cb), BF16), jax.ShapeDtypeStruct((B, nc, L, cb), BF16),
                   jax.ShapeDtypeStruct((B, nc, P, L // P, cb), F32)],
        compiler_params=_params("parallel", "parallel"),
        name="hyena_gate",
    )(proj, proj, proj, short_w, short_w, short_w, short_b, short_b, short_b, perm, shift)


def _dft_fwd_kernel(*refs, filtered):
    if filtered:
        z_ref, w1_ref, a3_ref, g_ref, y_ref, v_ref = refs
    else:
        z_ref, w1_ref, a3_ref, y_ref, v_ref = refs
    P, MG = DFT_P, DFT_MG
    cb = z_ref.shape[2]
    ng, rows_g = a3_ref.shape[0], a3_ref.shape[1]
    half = rows_g // 2
    w1 = w1_ref[...]
    for b in range(P):
        v = _dot(w1, z_ref[b].astype(BF16))
        v_ref[:, :, b * MG:(b + 1) * MG, :] = v.reshape(2, ng, MG, cb)

    if not filtered:
        backward = pl.program_id(1) == 1
        scale = 2.0 / y_ref.shape[0]
        im_scale = jnp.where(backward, -scale, scale)

        @pl.when(jnp.logical_not(backward))
        def _():
            y_ref[...] = jnp.zeros_like(y_ref)

    def group(g, carry):
        vg = jnp.concatenate([v_ref[0, g], v_ref[1, g]], axis=0).astype(BF16)
        x = _dot(a3_ref[g], vg)
        rows = pl.ds(pl.multiple_of(g * rows_g, rows_g), rows_g)
        xr, xi = x[:half], x[half:]
        if filtered:
            gr, gi = g_ref[g, :half], g_ref[g, half:]
            y_ref[rows, :] = jnp.concatenate([xr * gr - xi * gi, xr * gi + xi * gr], axis=0).astype(y_ref.dtype)
        else:
            y_ref[rows, :] += jnp.concatenate([xr * scale, xi * im_scale], axis=0)
        return carry

    lax.fori_loop(0, ng, group, 0, unroll=8)


def _dft_forward(z, w1, a3, g, Bz, L, cb, out_dtype):
    ng, rows_g, cols_g = a3.shape
    const = lambda shape: pl.BlockSpec(shape, lambda c, b: (0,) * len(shape), pipeline_mode=pl.Buffered(1))
    P, A = DFT_P, L // DFT_P
    if g is not None:
        nc = z.shape[1]
        in_specs = [pl.BlockSpec((None, None, P, A, cb), lambda c, b: (b, c, 0, 0, 0)), const(w1.shape),
                    const(a3.shape),
                    pl.BlockSpec((ng, rows_g, cb), lambda c, b: (0, 0, c), pipeline_mode=pl.Buffered(1))]
        args = [z, w1, a3, g]
        out_spec = pl.BlockSpec((None, None, 2 * L, cb), lambda c, b: (b, c, 0, 0))
        out_shape = (Bz, nc, 2 * L, cb)
    else:
        assert Bz == 2
        nc = z.shape[3] // cb
        in_specs = [pl.BlockSpec((None, P, A, cb), lambda c, b: (b, 0, 0, c)), const(w1.shape), const(a3.shape)]
        args = [z, w1, a3]
        out_spec = pl.BlockSpec((None, 2 * L, cb), lambda c, b: (0, 0, c))
        out_shape = (1, 2 * L, z.shape[3])
    return pl.pallas_call(
        functools.partial(_dft_fwd_kernel, filtered=g is not None),
        grid=(nc, Bz),
        in_specs=in_specs,
        out_specs=out_spec,
        out_shape=jax.ShapeDtypeStruct(out_shape, out_dtype),
        scratch_shapes=[pltpu.VMEM((2, ng, cols_g // 2, cb), F32)],
        compiler_params=_params("parallel", "arbitrary"),
        name="hyena_dft_forward",
    )(*args)


def _dft_inv_kernel(y_ref, b3_ref, w1t_ref, zb_ref, x0_ref, d_ref, unperm_ref, o_ref, q_ref, t_ref):
    P, MG = DFT_P, DFT_MG
    chunk = P * MG
    cb = y_ref.shape[1]
    ng, rows_g, cols_g = b3_ref.shape
    half = rows_g // 2

    def group(g, carry):
        yg = y_ref[pl.ds(pl.multiple_of(g * cols_g, cols_g), cols_g), :]
        q = _dot(b3_ref[g], yg)
        q_ref[0, g] = q[:half]
        q_ref[1, g] = q[half:]
        return carry

    lax.fori_loop(0, ng, group, 0, unroll=8)
    w1t = w1t_ref[...]
    for b in range(P):
        qb = q_ref[:, :, b * MG:(b + 1) * MG, :].reshape(2 * ng * MG, cb)
        t_ref[b] = _dot(w1t, qb.astype(BF16))

    unperm = unperm_ref[...]

    def gate(c, carry):
        rows = pl.ds(pl.multiple_of(c * chunk, chunk), chunk)
        yp = t_ref[:, pl.ds(pl.multiple_of(c * MG, MG), MG), :].reshape(chunk, cb)
        hi = yp.astype(BF16)
        lo = (yp - hi.astype(F32)).astype(BF16)
        y = _dot(unperm, hi) + _dot(unperm, lo)
        y = y + d_ref[...] * zb_ref[rows, :].astype(F32)
        o_ref[rows, :] = (x0_ref[rows, :].astype(F32) * y).astype(o_ref.dtype)
        return carry

    lax.fori_loop(0, zb_ref.shape[0] // chunk, gate, 0, unroll=4)


def _dft_inverse(y, b3, w1t, zb, x0c, d, B, L, cb):
    nc = zb.shape[1]
    ng, rows_g, _ = b3.shape
    const = lambda shape: pl.BlockSpec(shape, lambda c, b: (0,) * len(shape), pipeline_mode=pl.Buffered(1))
    blocked = lambda rows: pl.BlockSpec((None, None, rows, cb), lambda c, b: (b, c, 0, 0))
    unperm = jnp.asarray(_time_split_permutation().T, BF16)
    return pl.pallas_call(
        _dft_inv_kernel,
        grid=(nc, B),
        in_specs=[blocked(2 * L), const(b3.shape), const(w1t.shape), blocked(L), blocked(L),
                  pl.BlockSpec((1, cb), lambda c, b: (0, c)), const(unperm.shape)],
        out_specs=pl.BlockSpec((L, cb), lambda c, b: (b, c)),
        out_shape=jax.ShapeDtypeStruct((B * L, nc * cb), BF16),
        scratch_shapes=[pltpu.VMEM((2, ng, rows_g // 2, cb), F32),
                        pltpu.VMEM((DFT_P, L // DFT_P, cb), F32)],
        compiler_params=_params("parallel", "arbitrary"),
        name="hyena_dft_inverse",
    )(y, b3, w1t, zb, x0c, d, unperm)


def _hyena(proj, B, L, mix_w, short_w, short_b, w1, b1, w2, b2, w3, freq, decay, d_bias):
    tiles = _tiles(L)
    cb = tiles.hyena_cols
    dw1, dw1t, a3, b3 = _dft_tables(L)
    ng, rows_g, _ = a3.shape
    taps = _hyena_filter_taps(L, w1, b1, w2, b2, w3, freq, decay, tl=tiles.filter_rows)
    g = _dft_forward(taps.reshape(2, DFT_P, L // DFT_P, mix_w), dw1, a3, None, 2, L, cb, F32)
    g = g.reshape(ng, rows_g, mix_w)
    x0c, zb, zp = _hyena_gate(proj, short_w.astype(F32), short_b.astype(F32)[None], B, L, mix_w, cb)
    y = _dft_forward(zp, dw1, a3, g, B, L, cb, BF16)
    return _dft_inverse(y, b3, dw1t, zb, x0c, d_bias.astype(F32)[None], B, L, cb)


def _mem_kv_kernel(m_ref, g_ref, w_ref, o_ref):
    h = _rmsnorm(m_ref[...], g_ref[...]).astype(BF16)
    o_ref[...] = _dot(h, w_ref[...]).astype(o_ref.dtype)


def _mem_kv(mem, g, w, layer, M):
    BM, D = mem.shape
    N = w.shape[2]
    return pl.pallas_call(
        _mem_kv_kernel,
        grid=(BM // M,),
        in_specs=[
            pl.BlockSpec((M, D), lambda b: (b, 0)),
            pl.BlockSpec((1, D), lambda b: (0, 0)),
            pl.BlockSpec((None, D, N), lambda b: (layer, 0, 0)),
        ],
        out_specs=pl.BlockSpec((M, N), lambda b: (b, 0)),
        out_shape=jax.ShapeDtypeStruct((BM, N), BF16),
        compiler_params=_params("parallel"),
        name="memory_kv",
    )(mem, g, w)


def _mix_out_kernel(tok_ref, q_ref, kv_ref, w_ref, x_ref, g_ref, o_ref, mo_ref):
    mem_w = q_ref.shape[1]
    mix_w = tok_ref.shape[1]
    heads = [slice(h * HEAD_DIM, (h + 1) * HEAD_DIM) for h in range(mem_w // HEAD_DIM)]
    scores = [_dot_nt(q_ref[:, c], kv_ref[:, c]) for c in heads]
    o = _dot(tok_ref[...], w_ref[:mix_w, :])
    probs = []
    for s in scores:
        p = jnp.exp2(s - jnp.max(s, axis=-1, keepdims=True))
        probs.append((p.astype(BF16), jnp.sum(p, axis=-1, keepdims=True)))
    for c, (p, l) in zip(heads, probs):
        vcols = slice(mem_w + c.start, mem_w + c.stop)
        mo_ref[:, c] = (_dot(p, kv_ref[:, vcols]) / l).astype(mo_ref.dtype)
    o = o + _dot(mo_ref[...], w_ref[mix_w:, :])
    o_ref[...] = x_ref[...] + _rmsnorm(o, g_ref[...])


def _mix_out(tok, proj, kv, w_out, layer, x, g, L, tm):
    T, D = x.shape
    mix_w = tok.shape[1]
    mem_w = w_out.shape[1] - mix_w
    M = kv.shape[0] // (T // L)
    qblk = (proj.shape[1] - mem_w) // mem_w
    per_b = L // tm
    return pl.pallas_call(
        _mix_out_kernel,
        grid=(T // tm,),
        in_specs=[
            pl.BlockSpec((tm, mix_w), lambda i: (i, 0)),
            pl.BlockSpec((tm, mem_w), lambda i: (i, qblk)),
            pl.BlockSpec((M, 2 * mem_w), lambda i: (i // per_b, 0)),
            pl.BlockSpec((None, mix_w + mem_w, D), lambda i: (layer, 0, 0)),
            pl.BlockSpec((tm, D), lambda i: (i, 0)),
            pl.BlockSpec((1, D), lambda i: (0, 0)),
        ],
        out_specs=pl.BlockSpec((tm, D), lambda i: (i, 0)),
        out_shape=jax.ShapeDtypeStruct((T, D), F32),
        scratch_shapes=[pltpu.VMEM((tm, mem_w), BF16)],
        compiler_params=_params("parallel"),
        name="mix_out_proj",
    )(tok, proj, kv, w_out, x, g)


def _mlp_kernel(x_ref, gpre_ref, wu_ref, wd_ref, gpost_ref, o_ref, h_ref, acc_ref):
    f = pl.program_id(1)
    nf = pl.num_programs(1)
    tm = x_ref.shape[0]

    @pl.when(f == 0)
    def _():
        g = gpre_ref[...]
        for r in range(0, tm, NORM_ROWS):
            rows = slice(r, r + NORM_ROWS)
            h_ref[rows, :] = _rmsnorm(x_ref[rows, :], g).astype(h_ref.dtype)
        acc_ref[...] = jnp.zeros_like(acc_ref)

    h = h_ref[...]
    tf = wu_ref.shape[1]
    cw = tf // MLP_CHUNKS
    chunks = [slice(c * cw, (c + 1) * cw) for c in range(MLP_CHUNKS)]
    ups = [_dot(h, wu_ref[:, c]) for c in chunks]
    acts = [jnp.square(jnp.maximum(u, 0.0)).astype(BF16) for u in ups]
    d = acc_ref[...]
    for a, c in zip(acts, chunks):
        d = d + _dot(a, wd_ref[c, :])
    acc_ref[...] = d

    @pl.when(f == nf - 1)
    def _():
        g = gpost_ref[...]
        for r in range(0, tm, NORM_ROWS):
            rows = slice(r, r + NORM_ROWS)
            o_ref[rows, :] = x_ref[rows, :] + _rmsnorm(acc_ref[rows, :], g)


def _mlp(x, gpre, w_up, w_down, layer, gpost, tm, tf):
    T, D = x.shape
    F = w_up.shape[2]
    return pl.pallas_call(
        _mlp_kernel,
        grid=(T // tm, F // tf),
        in_specs=[
            pl.BlockSpec((tm, D), lambda i, f: (i, 0)),
            pl.BlockSpec((1, D), lambda i, f: (0, 0)),
            pl.BlockSpec((None, D, tf), lambda i, f: (layer, 0, f)),
            pl.BlockSpec((None, tf, D), lambda i, f: (layer, f, 0)),
            pl.BlockSpec((1, D), lambda i, f: (0, 0)),
        ],
        out_specs=pl.BlockSpec((tm, D), lambda i, f: (i, 0)),
        out_shape=jax.ShapeDtypeStruct((T, D), F32),
        scratch_shapes=[pltpu.VMEM((tm, D), BF16), pltpu.VMEM((tm, D), F32)],
        compiler_params=_params("parallel", "arbitrary"),
        name="relu2_mlp",
    )(x, gpre, w_up, w_down, gpost)


def _trunk(x, mem, wts):
    (g_mix_pre, w_in, na_rpb, hy_short_w, hy_short_b, hy_w1, hy_b1, hy_w2, hy_b2, hy_w3, hy_freq,
     hy_decay, hy_d, g_mem, w_mem_kv, w_out, g_mix_post, g_mlp_pre, w_up, w_down, g_mlp_post) = wts
    B, L, D = x.shape
    depth = w_in.shape[0]
    in_w = w_in.shape[2]
    mem_w = w_mem_kv.shape[2] // 2
    mix_w = (in_w - mem_w) // 3
    x = x.reshape(B * L, D)
    mem = mem.reshape(-1, D)
    tiles = _tiles(L)
    qk_scale = HEAD_DIM ** -0.5 * LOG2E
    row = lambda v: v.astype(F32)[None]
    for i in range(depth):
        j = i // 2
        na_layer = i % 2 == 0
        colscale = jnp.concatenate([
            jnp.full((mix_w,), qk_scale if na_layer else 1.0, F32),
            jnp.ones((2 * mix_w,), F32),
            jnp.full((mem_w,), qk_scale, F32)])[None]
        proj = _norm_matmul(x, row(g_mix_pre[i]), w_in, i, colscale, tm=tiles.proj_rows, tn=tiles.proj_cols)
        if na_layer:
            bias = _na_bias_table(na_rpb[j], L // GRID_W)
            tok = _neighbourhood_attention(proj, bias, B, L, mix_w)
        else:
            tok = _hyena(proj, B, L, mix_w, hy_short_w[j], hy_short_b[j], hy_w1[j], hy_b1[j], hy_w2[j],
                         hy_b2[j], hy_w3[j], hy_freq[j], hy_decay[j], hy_d[j])
        kv = _mem_kv(mem, row(g_mem[i]), w_mem_kv, i, min(tiles.kv_rows, mem.shape[0]))
        x = _mix_out(tok, proj, kv, w_out, i, x, row(g_mix_post[i]), L, tm=tiles.mix_rows)
        x = _mlp(x, row(g_mlp_pre[i]), w_up, w_down, i, row(g_mlp_post[i]), tm=tiles.mlp_rows, tf=tiles.mlp_cols)
    return x.reshape(B, L, D)


def kernel(x_prompt, x_sample, mem_prompt, mem_sample, g_mix_pre, w_in, na_rpb, hy_short_w, hy_short_b,
           hy_w1, hy_b1, hy_w2, hy_b2, hy_w3, hy_freq, hy_decay, hy_d, g_mem, w_mem_kv, w_out,
           g_mix_post, g_mlp_pre, w_up, w_down, g_mlp_post):
    bf = lambda w: w.astype(BF16)
    wts = (g_mix_pre, bf(w_in), na_rpb, hy_short_w, hy_short_b, hy_w1, hy_b1, hy_w2, hy_b2, hy_w3, hy_freq,
           hy_decay, hy_d, g_mem, bf(w_mem_kv), bf(w_out), g_mix_post, g_mlp_pre, bf(w_up), bf(w_down), g_mlp_post)
    return (_trunk(x_prompt, mem_prompt, wts), _trunk(x_sample, mem_sample, wts))
```

```python
import functools
import math
from typing import NamedTuple

import numpy as np
import jax
import jax.numpy as jnp
from jax import lax
from jax.experimental import pallas as pl
from jax.experimental.pallas import tpu as pltpu

F32 = jnp.float32
BF16 = jnp.bfloat16

GRID_W = 64
HEAD_DIM = 128
NA_MAX_KH = 8
NA_KW = 16
FILTER_EMB = 33
FILTER_BANDS = (FILTER_EMB - 1) // 2
RMS_EPS = 1e-6
NEG_BIG = -1e30
LOG2E = math.log2(math.e)

VMEM_LIMIT_BYTES = 56 * 1024 * 1024
LANES = 128
SUBLANES_F32 = 8
SUBLANES_BF16 = 16

NA_ROWS_PER_STEP = 4
NA_HEADS_PER_STEP = 6
NA_HEADS_PER_WAVE = 2
NA_KEY_ROWS = NA_ROWS_PER_STEP + NA_MAX_KH
NORM_ROWS = 64
DFT_P = 32
DFT_MG = SUBLANES_F32
HYENA_LANE_ELEMS = 1 << 20


class _Tiles(NamedTuple):
    proj_rows: int
    proj_cols: int
    mix_rows: int
    mlp_rows: int
    mlp_cols: int
    filter_rows: int
    hyena_cols: int
    kv_rows: int


def _tiles(L):
    return _Tiles(proj_rows=min(1024, L), proj_cols=1280, mix_rows=min(512, L), mlp_rows=min(512, L),
                  mlp_cols=1024, filter_rows=min(512, L), hyena_cols=min(512, HYENA_LANE_ELEMS // L),
                  kv_rows=1024)


def _params(*sem):
    return pltpu.CompilerParams(dimension_semantics=sem, vmem_limit_bytes=VMEM_LIMIT_BYTES)


def _rmsnorm(x, g):
    ms = jnp.mean(x * x, axis=-1, keepdims=True)
    return x * lax.rsqrt(ms + RMS_EPS) * g


def _dot(a, b):
    return jnp.dot(a, b, preferred_element_type=F32)


def _dot_nt(a, b):
    return lax.dot_general(a, b, (((1,), (1,)), ((), ())), preferred_element_type=F32)


def _norm_matmul_kernel(x_ref, g_ref, w_ref, s_ref, o_ref, h0_ref, h1_ref, *, nt, nj):
    s, j = pl.program_id(0), pl.program_id(1)
    part = x_ref.shape[0] // nj
    rows = pl.ds(pl.multiple_of(j * part, part), part)

    def normalise_into(h_ref):
        h_ref[rows, :] = _rmsnorm(x_ref[rows, :], g_ref[...]).astype(h_ref.dtype)

    def project(h_ref):
        o_ref[...] = (_dot(h_ref[...], w_ref[...]) * s_ref[...]).astype(o_ref.dtype)

    even = s % 2 == 0

    @pl.when(s == 0)
    def _():
        normalise_into(h0_ref)

    @pl.when((s > 0) & (s < nt) & even)
    def _():
        project(h1_ref)
        normalise_into(h0_ref)

    @pl.when((s < nt) & jnp.logical_not(even))
    def _():
        project(h0_ref)
        normalise_into(h1_ref)

    @pl.when(s == nt)
    def _():
        project(h0_ref if nt % 2 == 1 else h1_ref)


def _norm_matmul(x, g, w, layer, colscale, tm, tn):
    T, D = x.shape
    N = w.shape[2]
    nt, nj = T // tm, N // tn
    assert tm % (nj * SUBLANES_BF16) == 0
    return pl.pallas_call(
        functools.partial(_norm_matmul_kernel, nt=nt, nj=nj),
        grid=(nt + 1, nj),
        in_specs=[
            pl.BlockSpec((tm, D), lambda s, j: (jnp.minimum(s, nt - 1), 0)),
            pl.BlockSpec((1, D), lambda s, j: (0, 0)),
            pl.BlockSpec((None, D, tn), lambda s, j: (layer, 0, j)),
            pl.BlockSpec((1, tn), lambda s, j: (0, j)),
        ],
        out_specs=pl.BlockSpec((tm, tn), lambda s, j: (jnp.maximum(s - 1, 0), jnp.where(s == 0, 0, j))),
        out_shape=jax.ShapeDtypeStruct((T, N), BF16),
        scratch_shapes=[pltpu.VMEM((tm, D), BF16), pltpu.VMEM((tm, D), BF16)],
        compiler_params=_params("arbitrary", "arbitrary"),
        name="norm_in_proj",
    )(x, g, w, colscale)


def _na_bias_table(rpb, rows):
    R, KR, W = NA_ROWS_PER_STEP, NA_KEY_ROWS, GRID_W
    kh = min(NA_MAX_KH, rows)
    nblk = rows // R
    qc = np.arange(W)
    cs = np.clip(qc - NA_KW // 2, 0, W - NA_KW)
    kc = np.arange(W)
    col_ok = (kc[None, :] >= cs[:, None]) & (kc[None, :] < cs[:, None] + NA_KW)
    dr_all, ok_all = [], []
    for blk in (0, min(1, nblk - 1), nblk - 1):
        ks = int(np.clip(blk * R - kh // 2, 0, rows - KR))
        r = blk * R + np.arange(R)
        rs = np.clip(r - kh // 2, 0, rows - kh)
        key = ks + np.arange(KR)
        row_ok = (key[None, :] >= rs[:, None]) & (key[None, :] < rs[:, None] + kh)
        dr = np.clip(key[None, :] - r[:, None] + NA_MAX_KH - 1, 0, 2 * NA_MAX_KH - 2)
        dr_all.append(dr)
        ok_all.append(row_ok)
    pad = W - NA_KW
    padded = jnp.pad(rpb.astype(F32) * LOG2E, ((0, 0), (0, 0), (pad, pad)))
    per_dr = jnp.stack([padded[:, :, W - 1 - q:2 * W - 1 - q] for q in range(W)], axis=2)
    per_dr = jnp.where(col_ok, per_dr, NEG_BIG)
    H = rpb.shape[0]

    def assemble(p_ref, o_ref):
        masked = jnp.full((W, W), NEG_BIG, F32)
        block = lambda v, r, k: p_ref[0, int(dr_all[v][r, k])] if ok_all[v][r, k] else masked
        for v in range(3):
            for r in range(R):
                for k in range(0, KR, 2):
                    o_ref[v, 0, r * W:(r + 1) * W, k * W:(k + 2) * W] = jnp.concatenate(
                        [block(v, r, k), block(v, r, k + 1)], axis=1)

    return pl.pallas_call(
        assemble,
        grid=(H,),
        in_specs=[pl.BlockSpec((1,) + per_dr.shape[1:], lambda h: (h, 0, 0, 0))],
        out_specs=pl.BlockSpec((3, 1, R * W, KR * W), lambda h: (0, h, 0, 0)),
        out_shape=jax.ShapeDtypeStruct((3, H, R * W, KR * W), F32),
        compiler_params=_params("parallel"),
        name="na_bias_table",
    )(per_dr)


def _na_kernel(q_ref, k_ref, v_ref, b_ref, o_ref, *, rows):
    R, KR, W = NA_ROWS_PER_STEP, NA_KEY_ROWS, GRID_W
    i = pl.program_id(2)
    ks = jnp.clip(i * R - min(NA_MAX_KH, rows) // 2, 0, rows - KR) * W
    ks = pl.multiple_of(ks, W)
    heads = [slice(h * HEAD_DIM, (h + 1) * HEAD_DIM) for h in range(NA_HEADS_PER_STEP)]
    keys = pl.ds(ks, KR * W)
    for w in range(0, len(heads), NA_HEADS_PER_WAVE):
        wave = list(enumerate(heads))[w:w + NA_HEADS_PER_WAVE]
        scores = [_dot_nt(q_ref[:, c], k_ref[keys, c]) + b_ref[0, h] for h, c in wave]
        probs = []
        for s in scores:
            p = jnp.exp2(s - jnp.max(s, axis=-1, keepdims=True))
            probs.append((p.astype(BF16), jnp.sum(p, axis=-1, keepdims=True)))
        for (h, c), (p, l) in zip(wave, probs):
            o_ref[:, c] = (_dot(p, v_ref[keys, c]) / l).astype(o_ref.dtype)


def _neighbourhood_attention(proj, bias, B, L, mix_w):
    R, KR, W = NA_ROWS_PER_STEP, NA_KEY_ROWS, GRID_W
    rows = L // W
    nblk = rows // R
    hw = NA_HEADS_PER_STEP * HEAD_DIM
    ngrp = mix_w // hw
    assert rows % R == 0 and rows >= KR and nblk >= 2 and R >= NA_MAX_KH // 2

    def variant(i):
        return jnp.where(i == 0, 0, jnp.where(i == nblk - 1, 2, 1))

    return pl.pallas_call(
        functools.partial(_na_kernel, rows=rows),
        grid=(B, ngrp, nblk),
        in_specs=[
            pl.BlockSpec((R * W, hw), lambda b, g, i: (b * nblk + i, g)),
            pl.BlockSpec((L, hw), lambda b, g, i: (b, ngrp + g)),
            pl.BlockSpec((L, hw), lambda b, g, i: (b, 2 * ngrp + g)),
            pl.BlockSpec((1, NA_HEADS_PER_STEP, R * W, KR * W), lambda b, g, i: (variant(i), g, 0, 0)),
        ],
        out_specs=pl.BlockSpec((R * W, hw), lambda b, g, i: (b * nblk + i, g)),
        out_shape=jax.ShapeDtypeStruct((B * L, mix_w), BF16),
        compiler_params=_params("parallel", "parallel", "arbitrary"),
        name="neighbourhood_attention",
    )(proj, proj, proj, bias)


def _position_features(L):
    P = DFT_P
    pos = (jnp.arange(L // P, dtype=F32)[None, :] * P + jnp.arange(P, dtype=F32)[:, None]).reshape(L)
    t = pos / (L - 1)
    wpos = 2.0 * math.pi * pos / L
    fb = jnp.linspace(1e-4, FILTER_BANDS - 1, FILTER_BANDS, dtype=F32)
    ang = wpos[:, None] * fb[None, :]
    z = jnp.concatenate([t[:, None], jnp.cos(ang), -jnp.sin(ang)], axis=-1)
    return jnp.pad(z, ((0, 0), (0, LANES - FILTER_EMB)))


def _hy_filter_kernel(z_ref, w1_ref, b1_ref, w2_ref, b2_ref, w3_ref, fr_ref, dec_ref, h_ref, *, mix_w):
    hp = lax.Precision.HIGHEST
    fr = fr_ref[...]
    z = z_ref[...]
    h = jnp.sin(fr * (jnp.dot(z, w1_ref[...], precision=hp, preferred_element_type=F32) + b1_ref[...]))
    h = jnp.sin(fr * (jnp.dot(h, w2_ref[...], precision=hp, preferred_element_type=F32) + b2_ref[...]))

    def taps(d):
        w3 = w3_ref[:, d * mix_w:(d + 1) * mix_w]
        win = jnp.exp(-z[:, 0:1] * jnp.abs(dec_ref[d:d + 1, :]))
        return jnp.dot(h, w3, precision=hp, preferred_element_type=F32) * win

    tl = z_ref.shape[0]
    lag0 = pl.program_id(0) * tl + lax.broadcasted_iota(jnp.int32, (tl, 1), 0) == 0
    h_fwd, h_bwd = taps(0), taps(1)
    h_ref[0] = h_fwd + jnp.where(lag0, h_bwd, 0.0)
    h_ref[1] = jnp.where(lag0, 0.0, h_bwd)


def _hyena_filter_taps(L, w1, b1, w2, b2, w3, freq, decay, tl):
    mix_w = decay.shape[1]
    order = w2.shape[0]
    w1p = jnp.pad(w1.astype(F32), ((0, LANES - FILTER_EMB), (0, 0)))
    full = lambda shape: pl.BlockSpec(shape, lambda i: (0, 0))
    return pl.pallas_call(
        functools.partial(_hy_filter_kernel, mix_w=mix_w),
        grid=(L // tl,),
        in_specs=[
            pl.BlockSpec((tl, LANES), lambda i: (i, 0)),
            full((LANES, order)), full((1, order)), full((order, order)), full((1, order)),
            full((order, 2 * mix_w)), full((1, order)), full((2, mix_w)),
        ],
        out_specs=pl.BlockSpec((2, tl, mix_w), lambda i: (0, i, 0)),
        out_shape=jax.ShapeDtypeStruct((2, L, mix_w), F32),
        compiler_params=_params("parallel"),
        name="hyena_filter_taps",
    )(_position_features(L), w1p, b1.astype(F32)[None], w2.astype(F32), b2.astype(F32)[None],
      w3.astype(F32), freq.astype(F32)[None], decay.astype(F32))


def _dft_tables(L):
    P, MG = DFT_P, DFT_MG
    A, n = L // P, 2 * L
    Mm, J = 2 * A, P // 2
    NG = Mm // MG
    ar = lambda size: jnp.arange(size, dtype=jnp.int32)
    ang1 = (((2 * ar(Mm) + 1)[:, None] * ar(A)[None, :]) % (2 * Mm)).astype(F32) * (math.pi / Mm)
    w1 = jnp.concatenate([jnp.cos(ang1), -jnp.sin(ang1)], axis=0)
    k = (Mm * ar(J)[None, None, :, None] + MG * ar(NG)[:, None, None, None] + ar(MG)[None, :, None, None])
    ph = ((ar(P)[None, None, None, :] * (2 * k + 1)) % (2 * n)).astype(F32) * (math.pi / n)
    cr, ci = jnp.cos(ph), -jnp.sin(ph)
    c4 = jnp.stack([jnp.stack([cr, -ci], axis=3), jnp.stack([ci, cr], axis=3)], axis=2)
    c4 = c4.reshape(NG, MG, 2 * J, 2 * P)
    spread_cols = np.zeros((MG, 2 * P, 2 * P * MG), np.float32)
    spread_rows = np.zeros((MG, 2 * J, 2 * J * MG), np.float32)
    for m in range(MG):
        spread_cols[m, np.arange(2 * P), np.arange(2 * P) * MG + m] = 1.0
        spread_rows[m, np.arange(2 * J), np.arange(2 * J) * MG + m] = 1.0
    a3 = jnp.einsum('gmrc,mcd->grmd', c4, jnp.asarray(spread_cols))
    b3 = jnp.einsum('gmrc,mrd->gcmd', c4, jnp.asarray(spread_rows))
    a3 = a3.astype(BF16).reshape(NG, 2 * J * MG, 2 * P * MG)
    b3 = b3.astype(BF16).reshape(NG, 2 * P * MG, 2 * J * MG)
    return w1.astype(BF16), w1.T.astype(BF16), a3, b3


def _short_conv(ref, w_ref, b_ref, start, size, L):
    halo, tile = SUBLANES_BF16, SUBLANES_F32
    cur = ref[pl.ds(start, size), :].astype(F32)
    up, dn = pltpu.roll(cur, 1, axis=0), pltpu.roll(cur, size - 1, axis=0)
    prev_start = pl.multiple_of(jnp.maximum(start - halo, 0), halo)
    next_start = pl.multiple_of(jnp.minimum(start + size, L - halo), halo)
    prev = ref[pl.ds(prev_start, halo), :][halo - 1:halo, :].astype(F32)
    nxt = ref[pl.ds(next_start, halo), :][0:1, :].astype(F32)
    prev = jnp.where(start == 0, 0.0, prev)
    nxt = jnp.where(start + size == L, 0.0, nxt)
    row = lax.broadcasted_iota(jnp.int32, (tile, 1), 0)
    up = jnp.concatenate([jnp.where(row == 0, prev, up[:tile]), up[tile:]], axis=0)
    dn = jnp.concatenate([dn[:size - tile], jnp.where(row == tile - 1, nxt, dn[size - tile:])], axis=0)
    return up * w_ref[0:1, :] + cur * w_ref[1:2, :] + dn * w_ref[2:3, :] + b_ref[...]


def _hy_gate_kernel(x0_ref, x1_ref, v_ref, w0_ref, w1_ref, wv_ref, b0_ref, b1_ref, bv_ref,
                    perm_ref, x0c_ref, zb_ref, zp_ref):
    P, MG = DFT_P, DFT_MG
    chunk = P * MG
    L, cb = x0_ref.shape

    def body(c, carry):
        start = pl.multiple_of(c * chunk, chunk)
        rows = pl.ds(start, chunk)
        conv = functools.partial(_short_conv, start=start, size=chunk, L=L)
        x0c_ref[rows, :] = conv(x0_ref, w0_ref, b0_ref).astype(x0c_ref.dtype)
        x1c = conv(x1_ref, w1_ref, b1_ref)
        vc = conv(v_ref, wv_ref, bv_ref)
        z = (x1c * vc).astype(BF16)
        zb_ref[rows, :] = z
        zp = _dot(perm_ref[...], z).reshape(P, MG, cb)
        zp_ref[:, pl.ds(pl.multiple_of(c * MG, MG), MG), :] = zp
        return carry

    lax.fori_loop(0, L // chunk, body, 0, unroll=2)


def _time_split_permutation():
    P, MG = DFT_P, DFT_MG
    r = np.arange(P * MG)
    perm = np.zeros((P * MG, P * MG), np.float32)
    perm[r, (r % MG) * P + r // MG] = 1.0
    return perm


def _hyena_gate(proj, short_w, short_b, B, L, mix_w, cb):
    nc = mix_w // cb
    P = DFT_P
    col = lambda part: pl.BlockSpec((L, cb), lambda b, c: (b, part * nc + c))
    wspec = lambda part: pl.BlockSpec((3, cb), lambda b, c: (0, part * nc + c))
    bspec = lambda part: pl.BlockSpec((1, cb), lambda b, c: (0, part * nc + c))
    perm = jnp.asarray(_time_split_permutation(), BF16)
    seq = pl.BlockSpec((None, None, L, cb), lambda b, c: (b, c, 0, 0))
    return pl.pallas_call(
        _hy_gate_kernel,
        grid=(B, nc),
        in_specs=[col(0), col(1), col(2), wspec(0), wspec(1), wspec(2), bspec(0), bspec(1), bspec(2),
                  pl.BlockSpec(perm.shape, lambda b, c: (0, 0))],
        out_specs=[seq, seq, pl.BlockSpec((None, None, P, L // P, cb), lambda b, c: (b, c, 0, 0, 0))],
        out_shape=[jax.ShapeDtypeStruct((B, nc, L, cb), BF16), jax.ShapeDtypeStruct((B, nc, L, cb), BF16),
                   jax.ShapeDtypeStruct((B, nc, P, L // P, cb), F32)],
        compiler_params=_params("parallel", "parallel"),
        name="hyena_gate",
    )(proj, proj, proj, short_w, short_w, short_w, short_b, short_b, short_b, perm)


def _dft_fwd_kernel(*refs, filtered):
    if filtered:
        z_ref, w1_ref, a3_ref, g_ref, y_ref, v_ref = refs
    else:
        z_ref, w1_ref, a3_ref, y_ref, v_ref = refs
    P, MG = DFT_P, DFT_MG
    cb = z_ref.shape[2]
    ng, rows_g = a3_ref.shape[0], a3_ref.shape[1]
    half = rows_g // 2
    w1 = w1_ref[...]
    for b in range(P):
        v = _dot(w1, z_ref[b].astype(BF16))
        v_ref[:, :, b * MG:(b + 1) * MG, :] = v.reshape(2, ng, MG, cb)

    if not filtered:
        backward = pl.program_id(1) == 1
        scale = 2.0 / y_ref.shape[0]
        im_scale = jnp.where(backward, -scale, scale)

        @pl.when(jnp.logical_not(backward))
        def _():
            y_ref[...] = jnp.zeros_like(y_ref)

    def group(g, carry):
        vg = jnp.concatenate([v_ref[0, g], v_ref[1, g]], axis=0).astype(BF16)
        x = _dot(a3_ref[g], vg)
        rows = pl.ds(pl.multiple_of(g * rows_g, rows_g), rows_g)
        xr, xi = x[:half], x[half:]
        if filtered:
            gr, gi = g_ref[g, :half], g_ref[g, half:]
            y_ref[rows, :] = jnp.concatenate([xr * gr - xi * gi, xr * gi + xi * gr], axis=0).astype(y_ref.dtype)
        else:
            y_ref[rows, :] += jnp.concatenate([xr * scale, xi * im_scale], axis=0)
        return carry

    lax.fori_loop(0, ng, group, 0, unroll=8)


def _dft_forward(z, w1, a3, g, Bz, L, cb, out_dtype):
    ng, rows_g, cols_g = a3.shape
    const = lambda shape: pl.BlockSpec(shape, lambda c, b: (0,) * len(shape), pipeline_mode=pl.Buffered(1))
    P, A = DFT_P, L // DFT_P
    if g is not None:
        nc = z.shape[1]
        in_specs = [pl.BlockSpec((None, None, P, A, cb), lambda c, b: (b, c, 0, 0, 0)), const(w1.shape),
                    const(a3.shape),
                    pl.BlockSpec((ng, rows_g, cb), lambda c, b: (0, 0, c), pipeline_mode=pl.Buffered(1))]
        args = [z, w1, a3, g]
        out_spec = pl.BlockSpec((None, None, 2 * L, cb), lambda c, b: (b, c, 0, 0))
        out_shape = (Bz, nc, 2 * L, cb)
    else:
        assert Bz == 2
        nc = z.shape[3] // cb
        in_specs = [pl.BlockSpec((None, P, A, cb), lambda c, b: (b, 0, 0, c)), const(w1.shape), const(a3.shape)]
        args = [z, w1, a3]
        out_spec = pl.BlockSpec((None, 2 * L, cb), lambda c, b: (0, 0, c))
        out_shape = (1, 2 * L, z.shape[3])
    return pl.pallas_call(
        functools.partial(_dft_fwd_kernel, filtered=g is not None),
        grid=(nc, Bz),
        in_specs=in_specs,
        out_specs=out_spec,
        out_shape=jax.ShapeDtypeStruct(out_shape, out_dtype),
        scratch_shapes=[pltpu.VMEM((2, ng, cols_g // 2, cb), F32)],
        compiler_params=_params("parallel", "arbitrary"),
        name="hyena_dft_forward",
    )(*args)


def _dft_inv_kernel(y_ref, b3_ref, w1t_ref, zb_ref, x0_ref, d_ref, unperm_ref, o_ref, q_ref, t_ref):
    P, MG = DFT_P, DFT_MG
    chunk = P * MG
    cb = y_ref.shape[1]
    ng, rows_g, cols_g = b3_ref.shape
    half = rows_g // 2

    def group(g, carry):
        yg = y_ref[pl.ds(pl.multiple_of(g * cols_g, cols_g), cols_g), :]
        q = _dot(b3_ref[g], yg)
        q_ref[0, g] = q[:half]
        q_ref[1, g] = q[half:]
        return carry

    lax.fori_loop(0, ng, group, 0, unroll=8)
    w1t = w1t_ref[...]
    for b in range(P):
        qb = q_ref[:, :, b * MG:(b + 1) * MG, :].reshape(2 * ng * MG, cb)
        t_ref[b] = _dot(w1t, qb.astype(BF16))

    unperm = unperm_ref[...]

    def gate(c, carry):
        rows = pl.ds(pl.multiple_of(c * chunk, chunk), chunk)
        yp = t_ref[:, pl.ds(pl.multiple_of(c * MG, MG), MG), :].reshape(chunk, cb)
        hi = yp.astype(BF16)
        lo = (yp - hi.astype(F32)).astype(BF16)
        y = _dot(unperm, hi) + _dot(unperm, lo)
        y = y + d_ref[...] * zb_ref[rows, :].astype(F32)
        o_ref[rows, :] = (x0_ref[rows, :].astype(F32) * y).astype(o_ref.dtype)
        return carry

    lax.fori_loop(0, zb_ref.shape[0] // chunk, gate, 0, unroll=4)


def _dft_inverse(y, b3, w1t, zb, x0c, d, B, L, cb):
    nc = zb.shape[1]
    ng, rows_g, _ = b3.shape
    const = lambda shape: pl.BlockSpec(shape, lambda c, b: (0,) * len(shape), pipeline_mode=pl.Buffered(1))
    blocked = lambda rows: pl.BlockSpec((None, None, rows, cb), lambda c, b: (b, c, 0, 0))
    unperm = jnp.asarray(_time_split_permutation().T, BF16)
    return pl.pallas_call(
        _dft_inv_kernel,
        grid=(nc, B),
        in_specs=[blocked(2 * L), const(b3.shape), const(w1t.shape), blocked(L), blocked(L),
                  pl.BlockSpec((1, cb), lambda c, b: (0, c)), const(unperm.shape)],
        out_specs=pl.BlockSpec((L, cb), lambda c, b: (b, c)),
        out_shape=jax.ShapeDtypeStruct((B * L, nc * cb), BF16),
        scratch_shapes=[pltpu.VMEM((2, ng, rows_g // 2, cb), F32),
                        pltpu.VMEM((DFT_P, L // DFT_P, cb), F32)],
        compiler_params=_params("parallel", "arbitrary"),
        name="hyena_dft_inverse",
    )(y, b3, w1t, zb, x0c, d, unperm)


def _hyena(proj, B, L, mix_w, short_w, short_b, w1, b1, w2, b2, w3, freq, decay, d_bias):
    tiles = _tiles(L)
    cb = tiles.hyena_cols
    dw1, dw1t, a3, b3 = _dft_tables(L)
    ng, rows_g, _ = a3.shape
    taps = _hyena_filter_taps(L, w1, b1, w2, b2, w3, freq, decay, tl=tiles.filter_rows)
    g = _dft_forward(taps.reshape(2, DFT_P, L // DFT_P, mix_w), dw1, a3, None, 2, L, cb, F32)
    g = g.reshape(ng, rows_g, mix_w)
    x0c, zb, zp = _hyena_gate(proj, short_w.astype(F32), short_b.astype(F32)[None], B, L, mix_w, cb)
    y = _dft_forward(zp, dw1, a3, g, B, L, cb, BF16)
    return _dft_inverse(y, b3, dw1t, zb, x0c, d_bias.astype(F32)[None], B, L, cb)


def _mem_kv_kernel(m_ref, g_ref, w_ref, o_ref):
    h = _rmsnorm(m_ref[...], g_ref[...]).astype(BF16)
    o_ref[...] = _dot(h, w_ref[...]).astype(o_ref.dtype)


def _mem_kv(mem, g, w, layer, M):
    BM, D = mem.shape
    N = w.shape[2]
    return pl.pallas_call(
        _mem_kv_kernel,
        grid=(BM // M,),
        in_specs=[
            pl.BlockSpec((M, D), lambda b: (b, 0)),
            pl.BlockSpec((1, D), lambda b: (0, 0)),
            pl.BlockSpec((None, D, N), lambda b: (layer, 0, 0)),
        ],
        out_specs=pl.BlockSpec((M, N), lambda b: (b, 0)),
        out_shape=jax.ShapeDtypeStruct((BM, N), BF16),
        compiler_params=_params("parallel"),
        name="memory_kv",
    )(mem, g, w)


def _mix_out_kernel(tok_ref, q_ref, kv_ref, w_ref, x_ref, g_ref, o_ref, mo_ref):
    mem_w = q_ref.shape[1]
    mix_w = tok_ref.shape[1]
    heads = [slice(h * HEAD_DIM, (h + 1) * HEAD_DIM) for h in range(mem_w // HEAD_DIM)]
    scores = [_dot_nt(q_ref[:, c], kv_ref[:, c]) for c in heads]
    o = _dot(tok_ref[...], w_ref[:mix_w, :])
    probs = []
    for s in scores:
        p = jnp.exp2(s - jnp.max(s, axis=-1, keepdims=True))
        probs.append((p.astype(BF16), jnp.sum(p, axis=-1, keepdims=True)))
    for c, (p, l) in zip(heads, probs):
        vcols = slice(mem_w + c.start, mem_w + c.stop)
        mo_ref[:, c] = (_dot(p, kv_ref[:, vcols]) / l).astype(mo_ref.dtype)
    o = o + _dot(mo_ref[...], w_ref[mix_w:, :])
    o_ref[...] = x_ref[...] + _rmsnorm(o, g_ref[...])


def _mix_out(tok, proj, kv, w_out, layer, x, g, L, tm):
    T, D = x.shape
    mix_w = tok.shape[1]
    mem_w = w_out.shape[1] - mix_w
    M = kv.shape[0] // (T // L)
    qblk = (proj.shape[1] - mem_w) // mem_w
    per_b = L // tm
    return pl.pallas_call(
        _mix_out_kernel,
        grid=(T // tm,),
        in_specs=[
            pl.BlockSpec((tm, mix_w), lambda i: (i, 0)),
            pl.BlockSpec((tm, mem_w), lambda i: (i, qblk)),
            pl.BlockSpec((M, 2 * mem_w), lambda i: (i // per_b, 0)),
            pl.BlockSpec((None, mix_w + mem_w, D), lambda i: (layer, 0, 0)),
            pl.BlockSpec((tm, D), lambda i: (i, 0)),
            pl.BlockSpec((1, D), lambda i: (0, 0)),
        ],
        out_specs=pl.BlockSpec((tm, D), lambda i: (i, 0)),
        out_shape=jax.ShapeDtypeStruct((T, D), F32),
        scratch_shapes=[pltpu.VMEM((tm, mem_w), BF16)],
        compiler_params=_params("parallel"),
        name="mix_out_proj",
    )(tok, proj, kv, w_out, x, g)


def _mlp_kernel(x_ref, gpre_ref, wu_ref, wd_ref, gpost_ref, o_ref, h_ref, acc_ref):
    f = pl.program_id(1)
    nf = pl.num_programs(1)
    tm = x_ref.shape[0]

    @pl.when(f == 0)
    def _():
        g = gpre_ref[...]
        for r in range(0, tm, NORM_ROWS):
            rows = slice(r, r + NORM_ROWS)
            h_ref[rows, :] = _rmsnorm(x_ref[rows, :], g).astype(h_ref.dtype)
        acc_ref[...] = jnp.zeros_like(acc_ref)

    a = jnp.square(jnp.maximum(_dot(h_ref[...], wu_ref[...]), 0.0)).astype(BF16)
    acc_ref[...] += _dot(a, wd_ref[...])

    @pl.when(f == nf - 1)
    def _():
        g = gpost_ref[...]
        for r in range(0, tm, NORM_ROWS):
            rows = slice(r, r + NORM_ROWS)
            o_ref[rows, :] = x_ref[rows, :] + _rmsnorm(acc_ref[rows, :], g)


def _mlp(x, gpre, w_up, w_down, layer, gpost, tm, tf):
    T, D = x.shape
    F = w_up.shape[2]
    return pl.pallas_call(
        _mlp_kernel,
        grid=(T // tm, F // tf),
        in_specs=[
            pl.BlockSpec((tm, D), lambda i, f: (i, 0)),
            pl.BlockSpec((1, D), lambda i, f: (0, 0)),
            pl.BlockSpec((None, D, tf), lambda i, f: (layer, 0, f)),
            pl.BlockSpec((None, tf, D), lambda i, f: (layer, f, 0)),
            pl.BlockSpec((1, D), lambda i, f: (0, 0)),
        ],
        out_specs=pl.BlockSpec((tm, D), lambda i, f: (i, 0)),
        out_shape=jax.ShapeDtypeStruct((T, D), F32),
        scratch_shapes=[pltpu.VMEM((tm, D), BF16), pltpu.VMEM((tm, D), F32)],
        compiler_params=_params("parallel", "arbitrary"),
        name="relu2_mlp",
    )(x, gpre, w_up, w_down, gpost)


def _trunk(x, mem, wts):
    (g_mix_pre, w_in, na_rpb, hy_short_w, hy_short_b, hy_w1, hy_b1, hy_w2, hy_b2, hy_w3, hy_freq,
     hy_decay, hy_d, g_mem, w_mem_kv, w_out, g_mix_post, g_mlp_pre, w_up, w_down, g_mlp_post) = wts
    B, L, D = x.shape
    depth = w_in.shape[0]
    in_w = w_in.shape[2]
    mem_w = w_mem_kv.shape[2] // 2
    mix_w = (in_w - mem_w) // 3
    x = x.reshape(B * L, D)
    mem = mem.reshape(-1, D)
    tiles = _tiles(L)
    qk_scale = HEAD_DIM ** -0.5 * LOG2E
    row = lambda v: v.astype(F32)[None]
    for i in range(depth):
        j = i // 2
        na_layer = i % 2 == 0
        colscale = jnp.concatenate([
            jnp.full((mix_w,), qk_scale if na_layer else 1.0, F32),
            jnp.ones((2 * mix_w,), F32),
            jnp.full((mem_w,), qk_scale, F32)])[None]
        proj = _norm_matmul(x, row(g_mix_pre[i]), w_in, i, colscale, tm=tiles.proj_rows, tn=tiles.proj_cols)
        if na_layer:
            bias = _na_bias_table(na_rpb[j], L // GRID_W)
            tok = _neighbourhood_attention(proj, bias, B, L, mix_w)
        else:
            tok = _hyena(proj, B, L, mix_w, hy_short_w[j], hy_short_b[j], hy_w1[j], hy_b1[j], hy_w2[j],
                         hy_b2[j], hy_w3[j], hy_freq[j], hy_decay[j], hy_d[j])
        kv = _mem_kv(mem, row(g_mem[i]), w_mem_kv, i, min(tiles.kv_rows, mem.shape[0]))
        x = _mix_out(tok, proj, kv, w_out, i, x, row(g_mix_post[i]), L, tm=tiles.mix_rows)
        x = _mlp(x, row(g_mlp_pre[i]), w_up, w_down, i, row(g_mlp_post[i]), tm=tiles.mlp_rows, tf=tiles.mlp_cols)
    return x.reshape(B, L, D)


def kernel(x_prompt, x_sample, mem_prompt, mem_sample, g_mix_pre, w_in, na_rpb, hy_short_w, hy_short_b,
           hy_w1, hy_b1, hy_w2, hy_b2, hy_w3, hy_freq, hy_decay, hy_d, g_mem, w_mem_kv, w_out,
           g_mix_post, g_mlp_pre, w_up, w_down, g_mlp_post):
    bf = lambda w: w.astype(BF16)
    wts = (g_mix_pre, bf(w_in), na_rpb, hy_short_w, hy_short_b, hy_w1, hy_b1, hy_w2, hy_b2, hy_w3, hy_freq,
           hy_decay, hy_d, g_mem, bf(w_mem_kv), bf(w_out), g_mix_post, g_mlp_pre, bf(w_up), bf(w_down), g_mlp_post)
    return (_trunk(x_prompt, mem_prompt, wts), _trunk(x_sample, mem_sample, wts))
```
